```python
import jax, jax.numpy as jnp
from jax import lax
import numpy as np

D_MODEL = 4096
BATCH = 4
SEQ = 2048
DEPTH = 1

N_META = 16
SSD_D_INNER = D_MODEL
SSD_HEAD_DIM = 64
SSD_HEADS = SSD_D_INNER // SSD_HEAD_DIM
SSD_GROUPS = 8
SSD_STATE = 128
SSD_CONV = 4
SSD_CHUNK = 128
SSD_CONV_DIM = SSD_D_INNER + 2 * SSD_GROUPS * SSD_STATE
SC_WIDTH = D_MODEL
SC_CONV = 3
N_EXPERTS = 32
TOP_K = 4
D_EXPERT = 1536
SWIGLU_LIMIT = 7.0
SWIGLU_ALPHA = 1.702
MOE_BLOCK = 256
EPS = 1e-5

Z_OFF = 0
XBC_OFF = Z_OFF + SSD_D_INNER
DT_OFF = XBC_OFF + SSD_CONV_DIM
SC_OFF = DT_OFF + SSD_HEADS
GATE_OFF = SC_OFF + 3 * SC_WIDTH
IN_PROJ_DIM = GATE_OFF + 2 * D_MODEL

kernel_name = "hybrid_ssd_shortconv_moe_meta"


def rms_norm(x, g):
    xf = x.astype(jnp.float32)
    y = xf * lax.rsqrt(jnp.mean(xf * xf, axis=-1, keepdims=True) + EPS)
    return (y * g.astype(jnp.float32)).astype(x.dtype)


def causal_depthwise_conv(u, w):
    K, C = w.shape
    return lax.conv_general_dilated(
        u, w[:, None, :].astype(u.dtype), window_strides=(1,), padding=[(K - 1, 0)],
        dimension_numbers=("NWC", "WIO", "NWC"), feature_group_count=C)


def segsum(a):
    T = a.shape[-1]
    aa = jnp.broadcast_to(a[..., :, None], a.shape + (T,))
    aa = jnp.where(jnp.tril(jnp.ones((T, T), bool), -1), aa, 0.0)
    s = jnp.cumsum(aa, axis=-2)
    return jnp.where(jnp.tril(jnp.ones((T, T), bool)), s, -jnp.inf)


def ssd_chunked(x, dt, A, Bm, Cm):
    b, T, H, P = x.shape
    G, N = Bm.shape[-2], Bm.shape[-1]
    J = H // G
    c = T // SSD_CHUNK
    X = (x * dt[..., None]).reshape(b, c, SSD_CHUNK, G, J, P)
    Ad = (dt * A).reshape(b, c, SSD_CHUNK, H).transpose(0, 3, 1, 2)
    Bc = Bm.reshape(b, c, SSD_CHUNK, G, N)
    Cc = Cm.reshape(b, c, SSD_CHUNK, G, N)
    A_cum = jnp.cumsum(Ad, axis=-1)
    Lmat = jnp.exp(segsum(Ad)).reshape(b, G, J, c, SSD_CHUNK, SSD_CHUNK)
    CB = jnp.einsum("bclgn,bcsgn->bgcls", Cc, Bc)
    y_diag = jnp.einsum("bgcls,bgjcls,bcsgjp->bclgjp", CB, Lmat, X)
    decay_states = jnp.exp(A_cum[..., -1:] - A_cum).reshape(b, G, J, c, SSD_CHUNK)
    states = jnp.einsum("bclgn,bgjcl,bclgjp->bcgjpn", Bc, decay_states, X)
    states = jnp.concatenate([jnp.zeros_like(states[:, :1]), states], axis=1)
    chunk_tot = jnp.pad(A_cum[..., -1], ((0, 0), (0, 0), (1, 0)))
    decay_chunk = jnp.exp(segsum(chunk_tot)).reshape(b, G, J, c + 1, c + 1)
    new_states = jnp.einsum("bgjzc,bcgjpn->bzgjpn", decay_chunk, states)[:, :-1]
    state_decay_out = jnp.exp(A_cum).reshape(b, G, J, c, SSD_CHUNK)
    y_off = jnp.einsum("bclgn,bcgjpn,bgjcl->bclgjp", Cc, new_states, state_decay_out)
    return (y_diag + y_off).reshape(b, T, H, P)


def ssd_branch(z, xbc, dt_raw, conv_w, conv_b, dt_bias, a_log, d_skip, norm_g, w_out):
    Bsz, L, _ = z.shape
    out_dtype = z.dtype
    xbc = jax.nn.silu(causal_depthwise_conv(xbc, conv_w) + conv_b.astype(xbc.dtype))
    xbc = xbc.astype(jnp.float32)
    gn = SSD_GROUPS * SSD_STATE
    xs = xbc[..., :SSD_D_INNER].reshape(Bsz, L, SSD_HEADS, SSD_HEAD_DIM)
    Bm = xbc[..., SSD_D_INNER:SSD_D_INNER + gn].reshape(Bsz, L, SSD_GROUPS, SSD_STATE)
    Cm = xbc[..., SSD_D_INNER + gn:].reshape(Bsz, L, SSD_GROUPS, SSD_STATE)
    dt = jax.nn.softplus(dt_raw.astype(jnp.float32) + dt_bias.astype(jnp.float32))
    A = -jnp.exp(a_log.astype(jnp.float32))
    T_pad = -(-L // SSD_CHUNK) * SSD_CHUNK
    pad = T_pad - L
    padt = lambda u: jnp.pad(u, ((0, 0), (pad, 0)) + ((0, 0),) * (u.ndim - 2))
    y = ssd_chunked(padt(xs), padt(dt), A, padt(Bm), padt(Cm))[:, pad:]
    y = y + d_skip.astype(jnp.float32)[:, None] * xs
    y = y.reshape(Bsz, L, SSD_D_INNER)
    g = (y * jax.nn.silu(z.astype(jnp.float32))).reshape(Bsz, L, SSD_GROUPS, SSD_D_INNER // SSD_GROUPS)
    g = g * lax.rsqrt(jnp.mean(g * g, axis=-1, keepdims=True) + EPS)
    g = g.reshape(Bsz, L, SSD_D_INNER) * norm_g.astype(jnp.float32)
    return g.astype(out_dtype) @ w_out


def short_conv_branch(sc_in, conv_w, w_out):
    b_gate, c_gate, v = jnp.split(sc_in, 3, axis=-1)
    u = causal_depthwise_conv(c_gate * v, conv_w)
    return (b_gate * u) @ w_out


def hybrid_mixer(h, w_in, ssd_conv_w, ssd_conv_b, dt_bias, a_log, d_skip, ssd_norm_g, w_ssd_out,
                 sc_conv_w, w_sc_out, w_o):
    proj = h @ w_in
    y_a = ssd_branch(proj[..., Z_OFF:XBC_OFF], proj[..., XBC_OFF:DT_OFF], proj[..., DT_OFF:SC_OFF],
                     ssd_conv_w, ssd_conv_b, dt_bias, a_log, d_skip, ssd_norm_g, w_ssd_out)
    y_b = short_conv_branch(proj[..., SC_OFF:GATE_OFF], sc_conv_w, w_sc_out)
    g_a = proj[..., GATE_OFF:GATE_OFF + D_MODEL]
    g_b = proj[..., GATE_OFF + D_MODEL:]
    mixed = jax.nn.sigmoid(g_a) * y_a + jax.nn.sigmoid(g_b) * y_b
    return mixed @ w_o


def moe_ffn(h, w_router, b_router, w_gate_up, b_gate_up, w_down, b_down):
    Bsz, L, D = h.shape
    T = Bsz * L
    xt = h.reshape(T, D)
    logits = xt.astype(jnp.float32) @ w_router.astype(jnp.float32) + b_router.astype(jnp.float32)
    top_val, top_idx = lax.top_k(logits, TOP_K)
    probs = jax.nn.softmax(top_val, axis=-1)
    P = T * TOP_K
    e_flat = top_idx.reshape(P).astype(jnp.int32)
    tok_flat = jnp.arange(P, dtype=jnp.int32) // TOP_K
    w_flat = probs.reshape(P)
    order = jnp.argsort(e_flat)
    e_sorted, tok_sorted, w_sorted = e_flat[order], tok_flat[order], w_flat[order]
    counts = jnp.bincount(e_flat, length=N_EXPERTS)
    starts = jnp.cumsum(counts) - counts
    padded = (counts + MOE_BLOCK - 1) // MOE_BLOCK * MOE_BLOCK
    pad_ends = jnp.cumsum(padded)
    pad_starts = pad_ends - padded
    dest = pad_starts[e_sorted] + jnp.arange(P, dtype=jnp.int32) - starts[e_sorted]
    n_blocks = -(-P // MOE_BLOCK) + N_EXPERTS
    cap = n_blocks * MOE_BLOCK
    tok_pad = jnp.full((cap,), T, jnp.int32).at[dest].set(tok_sorted)
    w_pad = jnp.zeros((cap,), jnp.float32).at[dest].set(w_sorted)
    block_start = jnp.arange(n_blocks, dtype=jnp.int32) * MOE_BLOCK
    block_e = jnp.minimum(jnp.searchsorted(pad_ends, block_start, side="right"), N_EXPERTS - 1)
    x_ext = jnp.concatenate([xt, jnp.zeros((1, D), xt.dtype)], axis=0)

    def run_block(args):
        tok, e = args
        xb = x_ext[tok]
        gu = xb @ w_gate_up[e] + b_gate_up[e]
        gate = jnp.minimum(gu[:, :D_EXPERT], SWIGLU_LIMIT)
        up = jnp.clip(gu[:, D_EXPERT:], -SWIGLU_LIMIT, SWIGLU_LIMIT)
        act = (up + 1.0) * (gate * jax.nn.sigmoid(gate * SWIGLU_ALPHA))
        return act @ w_down[e] + b_down[e]

    yb = lax.map(run_block, (tok_pad.reshape(n_blocks, MOE_BLOCK), block_e))
    yb = yb.reshape(cap, D) * w_pad[:, None].astype(yb.dtype)
    out = jnp.zeros((T + 1, D), yb.dtype).at[tok_pad].add(yb)[:T]
    return out.reshape(Bsz, L, D).astype(h.dtype)


def setup_inputs(seed: int = 0) -> dict:
    key = jax.random.key(seed)
    ks = jax.random.split(key, 24)
    f32 = jnp.float32
    nrm = lambda k, shape, scale: jax.random.normal(k, shape, f32) * scale
    dt = jnp.exp(jax.random.uniform(ks[6], (DEPTH, SSD_HEADS), f32, np.log(1e-3), np.log(1e-1)))
    dt_bias = dt + jnp.log(-jnp.expm1(-dt))
    return {
        "x": nrm(ks[0], (BATCH, SEQ, D_MODEL), 1.0),
        "meta_tokens": nrm(ks[1], (N_META, D_MODEL), 1.0),
        "norm_mix_g": 1.0 + nrm(ks[2], (DEPTH, D_MODEL), 0.02),
        "w_in": nrm(ks[3], (DEPTH, D_MODEL, IN_PROJ_DIM), D_MODEL ** -0.5),
        "ssd_conv_w": nrm(ks[4], (DEPTH, SSD_CONV, SSD_CONV_DIM), SSD_CONV ** -0.5),
        "ssd_conv_b": nrm(ks[5], (DEPTH, SSD_CONV_DIM), 0.02),
        "dt_bias": dt_bias,
        "a_log": jnp.log(jax.random.uniform(ks[7], (DEPTH, SSD_HEADS), f32, 1.0, 16.0)),
        "d_skip": 1.0 + nrm(ks[8], (DEPTH, SSD_HEADS), 0.02),
        "ssd_norm_g": 1.0 + nrm(ks[9], (DEPTH, SSD_D_INNER), 0.02),
        "w_ssd_out": nrm(ks[10], (DEPTH, SSD_D_INNER, D_MODEL), SSD_D_INNER ** -0.5),
        "sc_conv_w": nrm(ks[11], (DEPTH, SC_CONV, SC_WIDTH), SC_CONV ** -0.5),
        "w_sc_out": nrm(ks[12], (DEPTH, SC_WIDTH, D_MODEL), SC_WIDTH ** -0.5),
        "w_o": nrm(ks[13], (DEPTH, D_MODEL, D_MODEL), D_MODEL ** -0.5),
        "norm_ffn_g": 1.0 + nrm(ks[14], (DEPTH, D_MODEL), 0.02),
        "w_router": nrm(ks[15], (DEPTH, D_MODEL, N_EXPERTS), D_MODEL ** -0.5),
        "b_router": nrm(ks[16], (DEPTH, N_EXPERTS), 0.01),
        "w_gate_up": nrm(ks[17], (DEPTH, N_EXPERTS, D_MODEL, 2 * D_EXPERT), D_MODEL ** -0.5),
        "b_gate_up": nrm(ks[18], (DEPTH, N_EXPERTS, 2 * D_EXPERT), 0.01),
        "w_down": nrm(ks[19], (DEPTH, N_EXPERTS, D_EXPERT, D_MODEL), D_EXPERT ** -0.5),
        "b_down": nrm(ks[20], (DEPTH, N_EXPERTS, D_MODEL), 0.01),
        "final_g": 1.0 + nrm(ks[21], (D_MODEL,), 0.02),
    }


def reference(x, meta_tokens, norm_mix_g, w_in, ssd_conv_w, ssd_conv_b, dt_bias, a_log, d_skip,
              ssd_norm_g, w_ssd_out, sc_conv_w, w_sc_out, w_o, norm_ffn_g, w_router, b_router,
              w_gate_up, b_gate_up, w_down, b_down, final_g):
    Bsz = x.shape[0]
    meta = jnp.broadcast_to(meta_tokens.astype(x.dtype)[None], (Bsz, N_META, D_MODEL))
    h_res = jnp.concatenate([meta, x], axis=1)
    for l in range(DEPTH):
        h = rms_norm(h_res, norm_mix_g[l])
        h_res = h_res + hybrid_mixer(h, w_in[l], ssd_conv_w[l], ssd_conv_b[l], dt_bias[l], a_log[l],
                                     d_skip[l], ssd_norm_g[l], w_ssd_out[l], sc_conv_w[l],
                                     w_sc_out[l], w_o[l])
        h = rms_norm(h_res, norm_ffn_g[l])
        h_res = h_res + moe_ffn(h, w_router[l], b_router[l], w_gate_up[l], b_gate_up[l],
                                w_down[l], b_down[l])
    out = rms_norm(h_res, final_g)
    return out[:, N_META:]
```

```python
import functools

import jax
import jax.numpy as jnp
from jax import lax
from jax.experimental import pallas as pl
from jax.experimental.pallas import tpu as pltpu

F32 = jnp.float32
BF16 = jnp.bfloat16
I32 = jnp.int32
U32 = jnp.uint32

N_META = 16
SSD_HEAD_DIM = 64
SSD_GROUPS = 8
SSD_STATE = 128
SSD_CONV = 4
SSD_CHUNK = 128
SC_CONV = 3
TOP_K = 4
SWIGLU_LIMIT = 7.0
SWIGLU_ALPHA = 1.702
EPS = 1e-5

LANE = 128
SUBLANE = 8
MOE_BLK = 256
MOE_SB_BLKS = 5
VMEM_MB = 56


def _cparams(n_axes, vmem_mb=VMEM_MB):
    return pltpu.CompilerParams(
        dimension_semantics=("arbitrary",) * n_axes,
        vmem_limit_bytes=vmem_mb * 1024 * 1024,
    )


def _sigmoid(v):
    return 1.0 / (1.0 + jnp.exp(-v))


def _rms_body(x_ref, g_ref, o_ref):
    x = x_ref[...]
    ms = jnp.mean(x * x, axis=-1, keepdims=True)
    o_ref[...] = (x * lax.rsqrt(ms + EPS) * g_ref[...]).astype(o_ref.dtype)


def _rmsnorm_bf16(x2, g, tm):
    m, d = x2.shape
    return pl.pallas_call(
        _rms_body,
        grid=(m // tm,),
        in_specs=[pl.BlockSpec((tm, d), lambda i: (i, 0)),
                  pl.BlockSpec((1, d), lambda i: (0, 0))],
        out_specs=pl.BlockSpec((tm, d), lambda i: (i, 0)),
        out_shape=jax.ShapeDtypeStruct((m, d), BF16),
        compiler_params=_cparams(1),
        name="rmsnorm",
    )(x2, g.reshape(1, d))


def _mm_body(*refs, n_extra, has_side, epilogue):
    it = iter(refs)
    a_ref = next(it)
    w_ref = next(it)
    side_ref = next(it) if has_side else None
    extras = [next(it) for _ in range(n_extra)]
    o_ref = next(it)
    oside_ref = next(it) if has_side else None
    wbf_ref = next(it)

    @pl.when(pl.program_id(1) == 0)
    def _():
        wbf_ref[...] = w_ref[...].astype(BF16)
        if has_side:
            oside_ref[...] = jnp.dot(side_ref[...], wbf_ref[...],
                                     preferred_element_type=F32)

    acc = jnp.dot(a_ref[...], wbf_ref[...], preferred_element_type=F32)
    if epilogue is not None:
        acc = epilogue(acc, *[e[...] for e in extras])
    o_ref[...] = acc.astype(o_ref.dtype)


def _matmul(a, w, col_off, n_cols, *, tm, tn, out_dtype, side=None, extras=(),
            epilogue=None, name="matmul"):
    m, k = a.shape
    assert m % tm == 0 and n_cols % tn == 0 and col_off % tn == 0
    cb = col_off // tn
    in_specs = [pl.BlockSpec((tm, k), lambda n, i: (i, 0)),
                pl.BlockSpec((k, tn), lambda n, i: (0, n + cb))]
    args = [a, w]
    if side is not None:
        in_specs.append(pl.BlockSpec((side.shape[0], k), lambda n, i: (0, 0)))
        args.append(side)
    for arr, off in extras:
        assert off % tn == 0
        in_specs.append(pl.BlockSpec((tm, tn), lambda n, i, o=off // tn: (i, n + o)))
        args.append(arr)
    out_shape = [jax.ShapeDtypeStruct((m, n_cols), out_dtype)]
    out_specs = [pl.BlockSpec((tm, tn), lambda n, i: (i, n))]
    if side is not None:
        out_shape.append(jax.ShapeDtypeStruct((side.shape[0], n_cols), F32))
        out_specs.append(pl.BlockSpec((side.shape[0], tn), lambda n, i: (0, n)))
    res = pl.pallas_call(
        functools.partial(_mm_body, n_extra=len(extras), has_side=side is not None,
                          epilogue=epilogue),
        grid=(n_cols // tn, m // tm),
        in_specs=in_specs,
        out_specs=out_specs,
        out_shape=out_shape,
        scratch_shapes=[pltpu.VMEM((k, tn), BF16)],
        compiler_params=_cparams(2),
        name=name,
    )(*args)
    return res if side is not None else res[0]


def _causal_conv(raw, prev, w, n_tap):
    rows = raw.shape[0]
    acc = raw * w[n_tap - 1:n_tap, :]
    row8 = lax.broadcasted_iota(I32, (SUBLANE, raw.shape[1]), 0)
    for s in range(1, n_tap):
        sh = pltpu.roll(raw, s, 0)
        shp = pltpu.roll(prev, s, 0)
        first = jnp.where(row8 < s, shp, sh[:SUBLANE])
        sh = jnp.concatenate([first, sh[SUBLANE:]], axis=0)
        acc = acc + sh * w[n_tap - 1 - s:n_tap - s, :]
    del rows
    return acc


def _ssd_chunk(raw, prev, dt_raw, st, cw, cb, dtb, alog, e_mat, *, heads, meta):
    L = raw.shape[0]
    di = heads * SSD_HEAD_DIM
    gn = SSD_GROUPS * SSD_STATE
    acc = _causal_conv(raw, prev, cw, SSD_CONV) + cb
    xbc = acc * _sigmoid(acc)
    dt_in = dt_raw[:, :heads] + dtb
    dtv = jnp.maximum(dt_in, 0.0) + jnp.log1p(jnp.exp(-jnp.abs(dt_in)))
    if meta:
        valid = (lax.broadcasted_iota(I32, (L, 1), 0) >= L - N_META).astype(F32)
        xbc = xbc * valid
        dtv = dtv * valid
    xs = xbc[:, :di]
    bm = xbc[:, di:di + gn]
    cm = xbc[:, di + gn:]
    a_neg = -jnp.exp(alog)
    ad = dtv * a_neg
    r = lax.broadcasted_iota(I32, (L, L), 0)
    c = lax.broadcasted_iota(I32, (L, L), 1)
    tril = r >= c
    acum = jnp.dot(tril.astype(F32), ad, precision=lax.Precision.HIGHEST,
                   preferred_element_type=F32)
    acum_x = jnp.dot(acum, e_mat, precision=lax.Precision.HIGHEST,
                     preferred_element_type=F32)
    dt_x = jnp.dot(dtv, e_mat, precision=lax.Precision.HIGHEST,
                   preferred_element_type=F32)
    alast_x = acum_x[L - 1:L, :]
    x_dt = xs * dt_x
    x_dec = (x_dt * jnp.exp(alast_x - acum_x)).astype(BF16)
    bt = bm.T.astype(BF16)
    return dict(xs=xs, cm=cm, bt=bt, acum=acum, acum_x=acum_x, alast_x=alast_x,
                x_dt=x_dt, x_dec=x_dec, tril=tril, st=st)


def _ssd_state_update(q, st_ref, heads):
    gw = heads * SSD_HEAD_DIM // SSD_GROUPS
    for g in range(SSD_GROUPS):
        sl = slice(g * gw, (g + 1) * gw)
        btg = q["bt"][g * SSD_STATE:(g + 1) * SSD_STATE, :]
        new = jnp.dot(btg, q["x_dec"][:, sl], preferred_element_type=F32)
        st_ref[:, sl] = st_ref[:, sl] * jnp.exp(q["alast_x"][:, sl]) + new


def _ssd_body(xbc_ref, dt_ref, z_ref, s0_ref, tail_ref, cw_ref, cb_ref, dtb_ref,
              alog_ref, dskip_ref, ng_ref, e_ref, o_ref, st_ref, prev_ref, *, heads):
    @pl.when(pl.program_id(1) == 0)
    def _():
        st_ref[...] = s0_ref[...]
        prev_ref[...] = tail_ref[...]

    raw = xbc_ref[...]
    L = raw.shape[0]
    q = _ssd_chunk(raw, prev_ref[...], dt_ref[...], None, cw_ref[...], cb_ref[...],
                   dtb_ref[...], alog_ref[...], e_ref[...], heads=heads, meta=False)
    prev_ref[...] = raw[L - SUBLANE:, :]

    gw = heads * SSD_HEAD_DIM // SSD_GROUPS
    j_heads = heads // SSD_GROUPS
    acum = q["acum"]
    acum_t = acum.T
    cmb = q["cm"].astype(BF16)
    x_dt_b = q["x_dt"].astype(BF16)
    lane = lax.broadcasted_iota(I32, (L, 2 * SSD_HEAD_DIM), 1)
    lo = lane < SSD_HEAD_DIM
    neg_inf = jnp.float32(-jnp.inf)

    def decay_mat(h):
        col = jnp.broadcast_to(acum[:, h:h + 1], (L, L))
        row = jnp.broadcast_to(acum_t[h:h + 1, :], (L, L))
        return jnp.exp(jnp.where(q["tril"], col - row, neg_inf))

    for g in range(SSD_GROUPS):
        sl = slice(g * gw, (g + 1) * gw)
        cg = cmb[:, g * SSD_STATE:(g + 1) * SSD_STATE]
        btg = q["bt"][g * SSD_STATE:(g + 1) * SSD_STATE, :]
        cb_mat = jnp.dot(cg, btg, preferred_element_type=F32)
        parts = []
        for p in range(j_heads // 2):
            h1 = g * j_heads + 2 * p
            slab = x_dt_b[:, h1 * SSD_HEAD_DIM:(h1 + 2) * SSD_HEAD_DIM]
            lhs = jnp.concatenate([(cb_mat * decay_mat(h1)).astype(BF16),
                                   (cb_mat * decay_mat(h1 + 1)).astype(BF16)], axis=1)
            zero = jnp.zeros_like(slab)
            rhs = jnp.concatenate([jnp.where(lo, slab, zero),
                                   jnp.where(lo, zero, slab)], axis=0)
            parts.append(jnp.dot(lhs, rhs, preferred_element_type=F32))
        y_diag = parts[0] if len(parts) == 1 else jnp.concatenate(parts, axis=1)
        st_g = st_ref[:, sl]
        y_off = jnp.dot(cg, st_g.astype(BF16), preferred_element_type=F32)
        y = y_diag + y_off * jnp.exp(q["acum_x"][:, sl]) + dskip_ref[:, sl] * q["xs"][:, sl]
        zg = z_ref[:, sl]
        gated = y * (zg * _sigmoid(zg))
        ms = jnp.mean(gated * gated, axis=-1, keepdims=True)
        o_ref[:, sl] = (gated * lax.rsqrt(ms + EPS) * ng_ref[:, sl]).astype(o_ref.dtype)
        new = jnp.dot(btg, q["x_dec"][:, sl], preferred_element_type=F32)
        st_ref[:, sl] = st_g * jnp.exp(q["alast_x"][:, sl]) + new


def _ssd_meta_body(xbc_ref, dt_ref, cw_ref, cb_ref, dtb_ref, alog_ref, e_ref,
                   st_ref, tail_ref, *, heads):
    raw = xbc_ref[...]
    L = raw.shape[0]
    st_ref[...] = jnp.zeros_like(st_ref)
    prev = jnp.zeros((SUBLANE, raw.shape[1]), F32)
    q = _ssd_chunk(raw, prev, dt_ref[...], None, cw_ref[...], cb_ref[...],
                   dtb_ref[...], alog_ref[...], e_ref[...], heads=heads, meta=True)
    _ssd_state_update(q, st_ref, heads)
    tail_ref[...] = raw[L - SUBLANE:, :]


def _full_spec(shape):
    nd = len(shape)
    return pl.BlockSpec(shape, lambda *_: (0,) * nd)


def _ssd_meta(xbc_m, dt_m, cw, cb, dtb, alog, e_mat, heads):
    L, cd = xbc_m.shape
    di = heads * SSD_HEAD_DIM
    args = [xbc_m, dt_m, cw, cb, dtb, alog, e_mat]
    return pl.pallas_call(
        functools.partial(_ssd_meta_body, heads=heads),
        grid=(1,),
        in_specs=[_full_spec(a.shape) for a in args],
        out_specs=[_full_spec((SSD_STATE, di)), _full_spec((SUBLANE, cd))],
        out_shape=[jax.ShapeDtypeStruct((SSD_STATE, di), F32),
                   jax.ShapeDtypeStruct((SUBLANE, cd), F32)],
        compiler_params=_cparams(1),
        name="ssd_meta",
    )(*args)


def _ssd_main(xbc, dt, z, s0, tail, cw, cb, dtb, alog, dskip_x, ng, e_mat, *, heads,
              batch):
    t, cd = xbc.shape
    di = heads * SSD_HEAD_DIM
    L = SSD_CHUNK
    nc = t // batch // L
    row = lambda b, c: (b * nc + c, 0)
    in_specs = [pl.BlockSpec((L, cd), row),
                pl.BlockSpec((L, LANE), row),
                pl.BlockSpec((L, di), row)]
    consts = [s0, tail, cw, cb, dtb, alog, dskip_x, ng, e_mat]
    in_specs += [_full_spec(a.shape) for a in consts]
    return pl.pallas_call(
        functools.partial(_ssd_body, heads=heads),
        grid=(batch, nc),
        in_specs=in_specs,
        out_specs=pl.BlockSpec((L, di), row),
        out_shape=jax.ShapeDtypeStruct((t, di), BF16),
        scratch_shapes=[pltpu.VMEM((SSD_STATE, di), F32),
                        pltpu.VMEM((SUBLANE, cd), F32)],
        compiler_params=_cparams(2),
        name="ssd_main",
    )(xbc, dt, z, *consts)


def _sc_body(b_ref, c_ref, v_ref, cm_ref, vm_ref, w_ref, o_ref, prev_ref, *,
             tiles_per_batch):
    @pl.when(pl.program_id(1) % tiles_per_batch == 0)
    def _():
        cvm = cm_ref[...] * vm_ref[...]
        prev_ref[...] = cvm[N_META - SUBLANE:, :]

    cv = c_ref[...] * v_ref[...]
    u = _causal_conv(cv, prev_ref[...], w_ref[...], SC_CONV)
    prev_ref[...] = cv[cv.shape[0] - SUBLANE:, :]
    o_ref[...] = (b_ref[...] * u).astype(o_ref.dtype)


def _short_conv(sc, sc_meta, w, *, width, tm, tn, seq):
    t = sc.shape[0]
    nb = width // tn
    in_specs = [pl.BlockSpec((tm, tn), lambda n, i: (i, n)),
                pl.BlockSpec((tm, tn), lambda n, i: (i, n + nb)),
                pl.BlockSpec((tm, tn), lambda n, i: (i, n + 2 * nb)),
                pl.BlockSpec((N_META, tn), lambda n, i: (0, n + nb)),
                pl.BlockSpec((N_META, tn), lambda n, i: (0, n + 2 * nb)),
                pl.BlockSpec((SC_CONV, tn), lambda n, i: (0, n))]
    return pl.pallas_call(
        functools.partial(_sc_body, tiles_per_batch=seq // tm),
        grid=(nb, t // tm),
        in_specs=in_specs,
        out_specs=pl.BlockSpec((tm, tn), lambda n, i: (i, n)),
        out_shape=jax.ShapeDtypeStruct((t, width), BF16),
        scratch_shapes=[pltpu.VMEM((SUBLANE, tn), F32)],
        compiler_params=_cparams(2),
        name="short_conv",
    )(sc, sc, sc, sc_meta, sc_meta, w)


def _router_body(x_ref, g_ref, wr_ref, br_ref, hp_ref, idx_ref, prob_ref, rank_ref,
                 cnt_ref, base_ref):
    @pl.when(pl.program_id(0) == 0)
    def _():
        base_ref[...] = jnp.zeros_like(base_ref)

    x = x_ref[...]
    tm, d = x.shape
    ne = wr_ref.shape[0]
    ms = jnp.mean(x * x, axis=-1, keepdims=True)
    h = x * lax.rsqrt(ms + EPS) * g_ref[...]
    half = d // 2
    lo_bits = lax.bitcast_convert_type(h[:, :half].astype(BF16).astype(F32), U32)
    hi_bits = lax.bitcast_convert_type(h[:, half:].astype(BF16).astype(F32), U32)
    hp_ref[...] = (lo_bits >> 16) | (hi_bits & jnp.uint32(0xFFFF0000))

    logits = lax.dot_general(wr_ref[...], h, (((1,), (1,)), ((), ())),
                             precision=lax.Precision.HIGHEST,
                             preferred_element_type=F32) + br_ref[:, 0:1]
    eidx = lax.broadcasted_iota(I32, (ne, tm), 0)
    work = logits
    vals, onehots = [], []
    for k in range(TOP_K):
        mx = jnp.max(work, axis=0, keepdims=True)
        sel = jnp.min(jnp.where(work == mx, eidx, ne), axis=0, keepdims=True)
        oh = eidx == sel
        idx_ref[k:k + 1, :] = sel
        vals.append(mx)
        onehots.append(oh)
        work = jnp.where(oh, -jnp.inf, work)
    exps = [jnp.exp(v - vals[0]) for v in vals]
    denom = exps[0]
    for e in exps[1:]:
        denom = denom + e
    for k in range(TOP_K):
        prob_ref[k:k + 1, :] = exps[k] / denom

    cnt = onehots[0].astype(F32)
    for oh in onehots[1:]:
        cnt = cnt + oh.astype(F32)
    r = lax.broadcasted_iota(I32, (tm, tm), 0)
    c = lax.broadcasted_iota(I32, (tm, tm), 1)
    before = (r < c).astype(BF16)
    prefix = jnp.dot(cnt.astype(BF16), before, preferred_element_type=F32)
    prefix = prefix + base_ref[:, 0:1]
    for k in range(TOP_K):
        rk = jnp.sum(jnp.where(onehots[k], prefix, 0.0), axis=0, keepdims=True)
        rank_ref[k:k + 1, :] = rk.astype(I32)
    base_ref[...] = base_ref[...] + jnp.sum(cnt, axis=1, keepdims=True)
    cnt_ref[...] = base_ref[...]


def _router(hres, g, wr_t, br, tm):
    t, d = hres.shape
    ne = wr_t.shape[0]
    tok = lambda i: (0, i)
    return pl.pallas_call(
        _router_body,
        grid=(t // tm,),
        in_specs=[pl.BlockSpec((tm, d), lambda i: (i, 0)),
                  _full_spec((1, d)), _full_spec((ne, d)), _full_spec((ne, LANE))],
        out_specs=[pl.BlockSpec((tm, d // 2), lambda i: (i, 0)),
                   pl.BlockSpec((TOP_K, tm), tok),
                   pl.BlockSpec((TOP_K, tm), tok),
                   pl.BlockSpec((TOP_K, tm), tok),
                   _full_spec((ne, LANE))],
        out_shape=[jax.ShapeDtypeStruct((t, d // 2), U32),
                   jax.ShapeDtypeStruct((TOP_K, t), I32),
                   jax.ShapeDtypeStruct((TOP_K, t), F32),
                   jax.ShapeDtypeStruct((TOP_K, t), I32),
                   jax.ShapeDtypeStruct((ne, LANE), F32)],
        scratch_shapes=[pltpu.VMEM((ne, LANE), F32)],
        compiler_params=_cparams(1),
        name="router",
    )(hres, g.reshape(1, d), wr_t, jnp.broadcast_to(br.reshape(ne, 1), (ne, LANE)))


def _gather_body(tok_ref, used_ref, src_ref, o_ref, sem):
    i = pl.program_id(0)
    blk = o_ref.shape[0]

    @pl.when(used_ref[i] != 0)
    def _():
        def issue(r, carry):
            t = tok_ref[i * blk + r]
            pltpu.make_async_copy(src_ref.at[pl.ds(t, 1)], o_ref.at[pl.ds(r, 1)], sem).start()
            return carry
        lax.fori_loop(0, blk, issue, 0)
        pltpu.make_async_copy(src_ref.at[pl.ds(0, blk)], o_ref, sem).wait()

    @pl.when(used_ref[i] == 0)
    def _():
        o_ref[...] = jnp.zeros_like(o_ref)


def _moe_gather(tok_slot, blk_used, hp, n_rows):
    w = hp.shape[1]
    return pl.pallas_call(
        _gather_body,
        grid_spec=pltpu.PrefetchScalarGridSpec(
            num_scalar_prefetch=2,
            grid=(n_rows // MOE_BLK,),
            in_specs=[pl.BlockSpec(memory_space=pl.ANY)],
            out_specs=pl.BlockSpec((MOE_BLK, w), lambda i, *_: (i, 0)),
            scratch_shapes=[pltpu.SemaphoreType.DMA],
        ),
        out_shape=jax.ShapeDtypeStruct((n_rows, w), U32),
        compiler_params=_cparams(1),
        name="moe_gather",
    )(tok_slot, blk_used, hp)


def _sb_specs(rows, width_in, tile, n_tiles):
    def x_map(s, j, e, xblk, nblk):
        return (xblk[s], 0)

    def wcol(s, j, nblk):
        return jnp.where(nblk[s] > 0, j, n_tiles - 1)

    x_spec = pl.BlockSpec((rows, width_in), x_map)
    o_spec = pl.BlockSpec((rows, tile), lambda s, j, e, xblk, nblk: (s, j))
    return x_spec, wcol, o_spec


def _sb_loop(nb, compute_block, o_ref):
    def block(b, carry):
        compute_block(pl.multiple_of(b * MOE_BLK, MOE_BLK))
        return carry
    lax.fori_loop(0, nb, block, 0)

    def zero(b, carry):
        r0 = pl.multiple_of(b * MOE_BLK, MOE_BLK)
        o_ref[pl.ds(r0, MOE_BLK), :] = jnp.zeros((MOE_BLK, o_ref.shape[1]), o_ref.dtype)
        return carry
    lax.fori_loop(nb, MOE_SB_BLKS, zero, 0)


def _gate_up_body(e_ref, xblk_ref, nblk_ref, x_ref, wg_ref, wu_ref, bg_ref, bu_ref,
                  o_ref, wbf_ref):
    s = pl.program_id(0)
    half = x_ref.shape[1]
    tj = wg_ref.shape[1]
    nb = nblk_ref[s]

    @pl.when(nb > 0)
    def _():
        wbf_ref[:, :tj] = wg_ref[...].astype(BF16)
        wbf_ref[:, tj:] = wu_ref[...].astype(BF16)

    bias = jnp.concatenate([bg_ref[...], bu_ref[...]], axis=1)

    def compute_block(r0):
        xp = x_ref[pl.ds(r0, MOE_BLK), :]
        x_lo = lax.bitcast_convert_type(xp << 16, F32).astype(BF16)
        x_hi = lax.bitcast_convert_type(xp & jnp.uint32(0xFFFF0000), F32).astype(BF16)
        gu = (jnp.dot(x_lo, wbf_ref[:half, :], preferred_element_type=F32)
              + jnp.dot(x_hi, wbf_ref[half:, :], preferred_element_type=F32) + bias)
        gate = jnp.minimum(gu[:, :tj], SWIGLU_LIMIT)
        up = jnp.clip(gu[:, tj:], -SWIGLU_LIMIT, SWIGLU_LIMIT)
        act = (up + 1.0) * (gate * _sigmoid(gate * SWIGLU_ALPHA))
        o_ref[pl.ds(r0, MOE_BLK), :] = act.astype(o_ref.dtype)

    _sb_loop(nb, compute_block, o_ref)


def _moe_gate_up(sb, xs, w_gu, b_gu, *, tj):
    ne, d, de2 = w_gu.shape
    de = de2 // 2
    nj = de // tj
    rows = MOE_SB_BLKS * MOE_BLK
    n_rows, half = xs.shape
    x_spec, wcol, o_spec = _sb_specs(rows, half, tj, nj)
    in_specs = [
        x_spec,
        pl.BlockSpec((None, d, tj), lambda s, j, e, xb, nb: (e[s], 0, wcol(s, j, nb))),
        pl.BlockSpec((None, d, tj), lambda s, j, e, xb, nb: (e[s], 0, nj + wcol(s, j, nb))),
        pl.BlockSpec((None, 1, tj), lambda s, j, e, xb, nb: (e[s], 0, wcol(s, j, nb))),
        pl.BlockSpec((None, 1, tj), lambda s, j, e, xb, nb: (e[s], 0, nj + wcol(s, j, nb))),
    ]
    return pl.pallas_call(
        _gate_up_body,
        grid_spec=pltpu.PrefetchScalarGridSpec(
            num_scalar_prefetch=3,
            grid=(sb[0].shape[0], nj),
            in_specs=in_specs,
            out_specs=o_spec,
            scratch_shapes=[pltpu.VMEM((d, 2 * tj), BF16)],
        ),
        out_shape=jax.ShapeDtypeStruct((n_rows, de), BF16),
        compiler_params=_cparams(2),
        name="moe_gate_up",
    )(*sb, xs, w_gu, w_gu, b_gu.reshape(ne, 1, de2), b_gu.reshape(ne, 1, de2))


def _down_body(e_ref, xblk_ref, nblk_ref, a_ref, w_ref, b_ref, o_ref, wbf_ref):
    s = pl.program_id(0)
    nb = nblk_ref[s]

    @pl.when(nb > 0)
    def _():
        wbf_ref[...] = w_ref[...].astype(BF16)

    bias = b_ref[...]

    def compute_block(r0):
        o_ref[pl.ds(r0, MOE_BLK), :] = jnp.dot(
            a_ref[pl.ds(r0, MOE_BLK), :], wbf_ref[...], preferred_element_type=F32) + bias

    _sb_loop(nb, compute_block, o_ref)


def _moe_down(sb, act, w_down, b_down, *, tn):
    ne, de, d = w_down.shape
    nn = d // tn
    rows = MOE_SB_BLKS * MOE_BLK
    n_rows = act.shape[0]
    a_spec, wcol, o_spec = _sb_specs(rows, de, tn, nn)
    in_specs = [
        a_spec,
        pl.BlockSpec((None, de, tn), lambda s, j, e, xb, nb: (e[s], 0, wcol(s, j, nb))),
        pl.BlockSpec((None, 1, tn), lambda s, j, e, xb, nb: (e[s], 0, wcol(s, j, nb))),
    ]
    return pl.pallas_call(
        _down_body,
        grid_spec=pltpu.PrefetchScalarGridSpec(
            num_scalar_prefetch=3,
            grid=(sb[0].shape[0], nn),
            in_specs=in_specs,
            out_specs=o_spec,
            scratch_shapes=[pltpu.VMEM((de, tn), BF16)],
        ),
        out_shape=jax.ShapeDtypeStruct((n_rows, d), F32),
        compiler_params=_cparams(2),
        name="moe_down",
    )(*sb, act, w_down, b_down.reshape(ne, 1, d))


def _moe_schedule(cnt, n_pairs, ne):
    sb_rows = MOE_SB_BLKS * MOE_BLK
    s_max = n_pairs // sb_rows + ne
    n_rows = s_max * sb_rows
    pblk = (cnt + MOE_BLK - 1) // MOE_BLK
    nsb_e = (pblk + MOE_SB_BLKS - 1) // MOE_SB_BLKS
    sb_end = jnp.cumsum(nsb_e)
    sb_first = sb_end - nsb_e
    n_sb = sb_end[-1]
    seg_start = sb_first * sb_rows
    s_ids = jnp.arange(s_max, dtype=I32)
    used = s_ids < n_sb
    e_of_s = jnp.minimum(jnp.searchsorted(sb_end, s_ids, side="right"), ne - 1).astype(I32)
    i_in_e = s_ids - sb_first[e_of_s]
    nblk = jnp.where(used, jnp.clip(pblk[e_of_s] - MOE_SB_BLKS * i_in_e, 0, MOE_SB_BLKS), 0)
    last = jnp.maximum(n_sb - 1, 0)
    sb = (jnp.where(used, e_of_s, e_of_s[last]).astype(I32),
          jnp.minimum(s_ids, last).astype(I32), nblk.astype(I32))
    blk_ids = jnp.arange(n_rows // MOE_BLK, dtype=I32)
    blk_used = (blk_ids % MOE_SB_BLKS < nblk[blk_ids // MOE_SB_BLKS]).astype(I32)
    return sb, seg_start, n_rows, blk_used


def _combine_body(dest_ref, y_ref, h_ref, p_ref, g_ref, o_ref, ybuf, sems, *, t_total):
    i = pl.program_id(0)
    n = pl.num_programs(0)
    tc = h_ref.shape[0]

    def issue(tile, slot):
        for k in range(TOP_K):
            def one(r, carry, k=k):
                row = dest_ref[k * t_total + tile * tc + r]
                pltpu.make_async_copy(y_ref.at[pl.ds(row, 1)],
                                      ybuf.at[slot, k, pl.ds(r, 1)],
                                      sems.at[slot]).start()
                return carry
            lax.fori_loop(0, tc, one, 0)

    @pl.when(i == 0)
    def _():
        issue(0, 0)

    slot = i % 2
    for k in range(TOP_K):
        pltpu.make_async_copy(y_ref.at[pl.ds(0, tc)], ybuf.at[slot, k], sems.at[slot]).wait()

    @pl.when(i + 1 < n)
    def _():
        issue(i + 1, 1 - slot)

    acc = h_ref[...]
    p = p_ref[...]
    for k in range(TOP_K):
        acc = acc + p[:, k:k + 1] * ybuf[slot, k]
    ms = jnp.mean(acc * acc, axis=-1, keepdims=True)
    o_ref[...] = acc * lax.rsqrt(ms + EPS) * g_ref[...]


def _combine(dest_flat, y_sorted, hres, p_t, final_g, *, tc):
    t, d = hres.shape
    return pl.pallas_call(
        functools.partial(_combine_body, t_total=t),
        grid_spec=pltpu.PrefetchScalarGridSpec(
            num_scalar_prefetch=1,
            grid=(t // tc,),
            in_specs=[pl.BlockSpec(memory_space=pl.ANY),
                      pl.BlockSpec((tc, d), lambda i, *_: (i, 0)),
                      pl.BlockSpec((tc, TOP_K), lambda i, *_: (i, 0)),
                      pl.BlockSpec((1, d), lambda i, *_: (0, 0))],
            out_specs=pl.BlockSpec((tc, d), lambda i, *_: (i, 0)),
            scratch_shapes=[pltpu.VMEM((2, TOP_K, tc, d), F32),
                            pltpu.SemaphoreType.DMA((2,))],
        ),
        out_shape=jax.ShapeDtypeStruct((t, d), F32),
        compiler_params=_cparams(1),
        name="moe_combine",
    )(dest_flat, y_sorted, hres, p_t, final_g.reshape(1, d))


def _pick(n, prefs):
    for p in prefs:
        if n % p == 0:
            return p
    raise ValueError(f"no tile in {prefs} divides {n}")


def kernel(x, meta_tokens, norm_mix_g, w_in, ssd_conv_w, ssd_conv_b, dt_bias, a_log, d_skip,
           ssd_norm_g, w_ssd_out, sc_conv_w, w_sc_out, w_o, norm_ffn_g, w_router, b_router,
           w_gate_up, b_gate_up, w_down, b_down, final_g):
    batch, seq, d = x.shape
    assert norm_mix_g.shape[0] == 1, "meta-token handling assumes a single layer"
    assert seq % SSD_CHUNK == 0 and N_META <= SSD_CHUNK
    t = batch * seq
    heads = d // SSD_HEAD_DIM
    gn = SSD_GROUPS * SSD_STATE
    conv_dim = d + 2 * gn
    xbc_off = d
    dt_off = xbc_off + conv_dim
    sc_off = dt_off + heads
    gate_off = sc_off + 3 * d
    ne = w_router.shape[-1]
    de = w_down.shape[2]

    x2 = x.reshape(t, d)
    w_in0 = w_in[0]
    w_tail = w_in0[:, sc_off:]

    tm = _pick(t, (1024, 512, 256, 128))
    tn = _pick(d, (512, 256, 128))

    h = _rmsnorm_bf16(x2, norm_mix_g[0], _pick(t, (256, 128)))
    h_meta = _rmsnorm_bf16(meta_tokens, norm_mix_g[0], N_META)

    z = _matmul(h, w_in0, 0, d, tm=tm, tn=tn, out_dtype=F32, name="proj_z")
    xbc, xbc_meta = _matmul(h, w_in0, xbc_off, conv_dim, tm=tm, tn=tn, out_dtype=F32,
                            side=h_meta, name="proj_xbc")
    dtr, dt_meta = _matmul(h, w_in0, dt_off, LANE, tm=tm, tn=LANE, out_dtype=F32,
                           side=h_meta, name="proj_dt")
    sc, sc_meta = _matmul(h, w_tail, 0, 3 * d, tm=tm, tn=tn, out_dtype=F32,
                          side=h_meta, name="proj_sc")
    gates = _matmul(h, w_tail, 3 * d, 2 * d, tm=tm, tn=tn, out_dtype=F32, name="proj_gate")

    e_mat = (jnp.arange(d, dtype=I32)[None, :] // SSD_HEAD_DIM
             == jnp.arange(heads, dtype=I32)[:, None]).astype(F32)
    pad = SSD_CHUNK - N_META
    cw = ssd_conv_w[0]
    cb = ssd_conv_b[0].reshape(1, conv_dim)
    dtb = dt_bias[0].reshape(1, heads)
    alog = a_log[0].reshape(1, heads)
    s0, tail = _ssd_meta(jnp.pad(xbc_meta, ((pad, 0), (0, 0))),
                         jnp.pad(dt_meta, ((pad, 0), (0, 0))),
                         cw, cb, dtb, alog, e_mat, heads)
    dskip_x = jnp.repeat(d_skip[0], SSD_HEAD_DIM).reshape(1, d)
    g_ssd = _ssd_main(xbc, dtr, z, s0, tail, cw, cb, dtb, alog, dskip_x,
                      ssd_norm_g[0].reshape(1, d), e_mat, heads=heads, batch=batch)

    sc_tm = _pick(seq, (512, 256, 128))
    u_sc = _short_conv(sc, sc_meta, sc_conv_w[0], width=d, tm=sc_tm, tn=tn, seq=seq)

    y_a = _matmul(g_ssd, w_ssd_out[0], 0, d, tm=tm, tn=tn, out_dtype=F32,
                  extras=[(gates, 0)],
                  epilogue=lambda acc, ga: _sigmoid(ga) * acc, name="ssd_out")
    mixed = _matmul(u_sc, w_sc_out[0], 0, d, tm=tm, tn=tn, out_dtype=BF16,
                    extras=[(gates, d), (y_a, 0)],
                    epilogue=lambda acc, gb, ya: ya + _sigmoid(gb) * acc, name="sc_out")
    hres = _matmul(mixed, w_o[0], 0, d, tm=tm, tn=tn, out_dtype=F32,
                   extras=[(x2, 0)], epilogue=lambda acc, xr: xr + acc, name="w_o")

    rt = _pick(t, (512, 256, 128))
    hp, top_idx, probs, rank, counts = _router(hres, norm_ffn_g[0], w_router[0].T,
                                               b_router[0], rt)
    cnt = counts[:, 0].astype(I32)
    sb, seg_start, n_slots, blk_used = _moe_schedule(cnt, t * TOP_K, ne)
    dest = seg_start[top_idx] + rank
    tok_ids = jnp.broadcast_to(jnp.arange(t, dtype=I32)[None, :], (TOP_K, t))
    tok_slot = jnp.zeros((n_slots,), I32).at[dest.reshape(-1)].set(tok_ids.reshape(-1))

    xs = _moe_gather(tok_slot, blk_used, hp, n_slots)
    act = _moe_gate_up(sb, xs, w_gate_up[0], b_gate_up[0], tj=_pick(de, (256, 128)))
    y_sorted = _moe_down(sb, act, w_down[0], b_down[0],
                         tn=_pick(d, (1024, 512, 256, 128)))
    out = _combine(dest.reshape(-1), y_sorted, hres, probs.T, final_g,
                   tc=_pick(t, (128,)))
    return out.reshape(batch, seq, d)
```

```python
import functools

import jax
import jax.numpy as jnp
from jax import lax
from jax.experimental import pallas as pl
from jax.experimental.pallas import tpu as pltpu

F32 = jnp.float32
BF16 = jnp.bfloat16
I32 = jnp.int32
U32 = jnp.uint32

N_META = 16
SSD_HEAD_DIM = 64
SSD_GROUPS = 8
SSD_STATE = 128
SSD_CONV = 4
SSD_CHUNK = 128
SC_CONV = 3
TOP_K = 4
SWIGLU_LIMIT = 7.0
SWIGLU_ALPHA = 1.702
EPS = 1e-5

LANE = 128
SUBLANE = 8
MOE_BLK = 256
MOE_SB_BLKS = 5
VMEM_MB = 56


def _cparams(n_axes, vmem_mb=VMEM_MB):
    return pltpu.CompilerParams(
        dimension_semantics=("arbitrary",) * n_axes,
        vmem_limit_bytes=vmem_mb * 1024 * 1024,
    )


def _sigmoid(v):
    return 1.0 / (1.0 + jnp.exp(-v))


def _rms_body(x_ref, g_ref, o_ref):
    x = x_ref[...]
    ms = jnp.mean(x * x, axis=-1, keepdims=True)
    o_ref[...] = (x * lax.rsqrt(ms + EPS) * g_ref[...]).astype(o_ref.dtype)


def _rmsnorm_bf16(x2, g, tm):
    m, d = x2.shape
    return pl.pallas_call(
        _rms_body,
        grid=(m // tm,),
        in_specs=[pl.BlockSpec((tm, d), lambda i: (i, 0)),
                  pl.BlockSpec((1, d), lambda i: (0, 0))],
        out_specs=pl.BlockSpec((tm, d), lambda i: (i, 0)),
        out_shape=jax.ShapeDtypeStruct((m, d), BF16),
        compiler_params=_cparams(1),
        name="rmsnorm",
    )(x2, g.reshape(1, d))


def _mm_body(*refs, n_extra, has_side, epilogue, w_is_nk):
    it = iter(refs)
    a_ref = next(it)
    w_ref = next(it)
    side_ref = next(it) if has_side else None
    extras = [next(it) for _ in range(n_extra)]
    o_ref = next(it)
    oside_ref = next(it) if has_side else None
    wbf_ref = next(it)

    @pl.when(pl.program_id(1) == 0)
    def _():
        w = w_ref[...]
        wbf_ref[...] = (w.T if w_is_nk else w).astype(BF16)
        if has_side:
            oside_ref[...] = jnp.dot(side_ref[...], wbf_ref[...],
                                     preferred_element_type=F32)

    acc = jnp.dot(a_ref[...], wbf_ref[...], preferred_element_type=F32)
    if epilogue is not None:
        acc = epilogue(acc, *[e[...] for e in extras])
    o_ref[...] = acc.astype(o_ref.dtype)


def _matmul(a, w, col_off, n_cols, *, tm, tn, out_dtype, side=None, extras=(),
            epilogue=None, w_is_nk=False, name="matmul"):
    m, k = a.shape
    assert m % tm == 0 and n_cols % tn == 0
    if w_is_nk:
        assert col_off % SUBLANE == 0
        w_spec = pl.BlockSpec((pl.Element(tn), pl.Element(k)),
                              lambda n, i: (pl.multiple_of(col_off + n * tn, SUBLANE), 0))
    else:
        assert col_off % tn == 0
        w_spec = pl.BlockSpec((k, tn), lambda n, i, cb=col_off // tn: (0, n + cb))
    in_specs = [pl.BlockSpec((tm, k), lambda n, i: (i, 0)), w_spec]
    args = [a, w]
    if side is not None:
        in_specs.append(pl.BlockSpec((side.shape[0], k), lambda n, i: (0, 0)))
        args.append(side)
    for arr, off in extras:
        assert off % tn == 0
        in_specs.append(pl.BlockSpec((tm, tn), lambda n, i, o=off // tn: (i, n + o)))
        args.append(arr)
    out_shape = [jax.ShapeDtypeStruct((m, n_cols), out_dtype)]
    out_specs = [pl.BlockSpec((tm, tn), lambda n, i: (i, n))]
    if side is not None:
        out_shape.append(jax.ShapeDtypeStruct((side.shape[0], n_cols), F32))
        out_specs.append(pl.BlockSpec((side.shape[0], tn), lambda n, i: (0, n)))
    res = pl.pallas_call(
        functools.partial(_mm_body, n_extra=len(extras), has_side=side is not None,
                          epilogue=epilogue, w_is_nk=w_is_nk),
        grid=(n_cols // tn, m // tm),
        in_specs=in_specs,
        out_specs=out_specs,
        out_shape=out_shape,
        scratch_shapes=[pltpu.VMEM((k, tn), BF16)],
        compiler_params=_cparams(2),
        name=name,
    )(*args)
    return res if side is not None else res[0]


def _causal_conv(raw, prev, w, n_tap):
    rows = raw.shape[0]
    acc = raw * w[n_tap - 1:n_tap, :]
    row8 = lax.broadcasted_iota(I32, (SUBLANE, raw.shape[1]), 0)
    for s in range(1, n_tap):
        sh = pltpu.roll(raw, s, 0)
        shp = pltpu.roll(prev, s, 0)
        first = jnp.where(row8 < s, shp, sh[:SUBLANE])
        sh = jnp.concatenate([first, sh[SUBLANE:]], axis=0)
        acc = acc + sh * w[n_tap - 1 - s:n_tap - s, :]
    del rows
    return acc


def _ssd_chunk(raw, prev, dt_raw, st, cw, cb, dtb, alog, e_mat, *, heads, meta):
    L = raw.shape[0]
    di = heads * SSD_HEAD_DIM
    gn = SSD_GROUPS * SSD_STATE
    acc = _causal_conv(raw, prev, cw, SSD_CONV) + cb
    xbc = acc * _sigmoid(acc)
    dt_in = dt_raw[:, :heads] + dtb
    dtv = jnp.maximum(dt_in, 0.0) + jnp.log1p(jnp.exp(-jnp.abs(dt_in)))
    if meta:
        valid = (lax.broadcasted_iota(I32, (L, 1), 0) >= L - N_META).astype(F32)
        xbc = xbc * valid
        dtv = dtv * valid
    xs = xbc[:, :di]
    bm = xbc[:, di:di + gn]
    cm = xbc[:, di + gn:]
    a_neg = -jnp.exp(alog)
    ad = dtv * a_neg
    r = lax.broadcasted_iota(I32, (L, L), 0)
    c = lax.broadcasted_iota(I32, (L, L), 1)
    tril = r >= c
    acum = jnp.dot(tril.astype(F32), ad, precision=lax.Precision.HIGHEST,
                   preferred_element_type=F32)
    acum_x = jnp.dot(acum, e_mat, precision=lax.Precision.HIGHEST,
                     preferred_element_type=F32)
    dt_x = jnp.dot(dtv, e_mat, precision=lax.Precision.HIGHEST,
                   preferred_element_type=F32)
    alast_x = acum_x[L - 1:L, :]
    x_dt = xs * dt_x
    x_dec = (x_dt * jnp.exp(alast_x - acum_x)).astype(BF16)
    bt = bm.T.astype(BF16)
    return dict(xs=xs, cm=cm, bt=bt, acum=acum, acum_x=acum_x, alast_x=alast_x,
                x_dt=x_dt, x_dec=x_dec, tril=tril, st=st)


def _ssd_state_update(q, st_ref, heads):
    gw = heads * SSD_HEAD_DIM // SSD_GROUPS
    for g in range(SSD_GROUPS):
        sl = slice(g * gw, (g + 1) * gw)
        btg = q["bt"][g * SSD_STATE:(g + 1) * SSD_STATE, :]
        new = jnp.dot(btg, q["x_dec"][:, sl], preferred_element_type=F32)
        st_ref[:, sl] = st_ref[:, sl] * jnp.exp(q["alast_x"][:, sl]) + new


def _ssd_body(xbc_ref, dt_ref, z_ref, s0_ref, tail_ref, cw_ref, cb_ref, dtb_ref,
              alog_ref, dskip_ref, ng_ref, e_ref, o_ref, st_ref, prev_ref, *, heads):
    @pl.when(pl.program_id(1) == 0)
    def _():
        st_ref[...] = s0_ref[...]
        prev_ref[...] = tail_ref[...]

    raw = xbc_ref[...]
    L = raw.shape[0]
    q = _ssd_chunk(raw, prev_ref[...], dt_ref[...], None, cw_ref[...], cb_ref[...],
                   dtb_ref[...], alog_ref[...], e_ref[...], heads=heads, meta=False)
    prev_ref[...] = raw[L - SUBLANE:, :]

    gw = heads * SSD_HEAD_DIM // SSD_GROUPS
    j_heads = heads // SSD_GROUPS
    acum = q["acum"]
    acum_t = acum.T
    cmb = q["cm"].astype(BF16)
    x_dt_b = q["x_dt"].astype(BF16)
    lane = lax.broadcasted_iota(I32, (L, 2 * SSD_HEAD_DIM), 1)
    lo = lane < SSD_HEAD_DIM
    neg_inf = jnp.float32(-jnp.inf)

    def decay_mat(h):
        col = jnp.broadcast_to(acum[:, h:h + 1], (L, L))
        row = jnp.broadcast_to(acum_t[h:h + 1, :], (L, L))
        return jnp.exp(jnp.where(q["tril"], col - row, neg_inf))

    for g in range(SSD_GROUPS):
        sl = slice(g * gw, (g + 1) * gw)
        cg = cmb[:, g * SSD_STATE:(g + 1) * SSD_STATE]
        btg = q["bt"][g * SSD_STATE:(g + 1) * SSD_STATE, :]
        cb_mat = jnp.dot(cg, btg, preferred_element_type=F32)
        parts = []
        for p in range(j_heads // 2):
            h1 = g * j_heads + 2 * p
            slab = x_dt_b[:, h1 * SSD_HEAD_DIM:(h1 + 2) * SSD_HEAD_DIM]
            lhs = jnp.concatenate([(cb_mat * decay_mat(h1)).astype(BF16),
                                   (cb_mat * decay_mat(h1 + 1)).astype(BF16)], axis=1)
            zero = jnp.zeros_like(slab)
            rhs = jnp.concatenate([jnp.where(lo, slab, zero),
                                   jnp.where(lo, zero, slab)], axis=0)
            parts.append(jnp.dot(lhs, rhs, preferred_element_type=F32))
        y_diag = parts[0] if len(parts) == 1 else jnp.concatenate(parts, axis=1)
        st_g = st_ref[:, sl]
        y_off = jnp.dot(cg, st_g.astype(BF16), preferred_element_type=F32)
        y = y_diag + y_off * jnp.exp(q["acum_x"][:, sl]) + dskip_ref[:, sl] * q["xs"][:, sl]
        zg = z_ref[:, sl]
        gated = y * (zg * _sigmoid(zg))
        ms = jnp.mean(gated * gated, axis=-1, keepdims=True)
        o_ref[:, sl] = (gated * lax.rsqrt(ms + EPS) * ng_ref[:, sl]).astype(o_ref.dtype)
        new = jnp.dot(btg, q["x_dec"][:, sl], preferred_element_type=F32)
        st_ref[:, sl] = st_g * jnp.exp(q["alast_x"][:, sl]) + new


def _ssd_meta_body(xbc_ref, dt_ref, cw_ref, cb_ref, dtb_ref, alog_ref, e_ref,
                   st_ref, tail_ref, *, heads):
    raw = xbc_ref[...]
    L = raw.shape[0]
    st_ref[...] = jnp.zeros_like(st_ref)
    prev = jnp.zeros((SUBLANE, raw.shape[1]), F32)
    q = _ssd_chunk(raw, prev, dt_ref[...], None, cw_ref[...], cb_ref[...],
                   dtb_ref[...], alog_ref[...], e_ref[...], heads=heads, meta=True)
    _ssd_state_update(q, st_ref, heads)
    tail_ref[...] = raw[L - SUBLANE:, :]


def _full_spec(shape):
    nd = len(shape)
    return pl.BlockSpec(shape, lambda *_: (0,) * nd)


def _ssd_meta(xbc_m, dt_m, cw, cb, dtb, alog, e_mat, heads):
    L, cd = xbc_m.shape
    di = heads * SSD_HEAD_DIM
    args = [xbc_m, dt_m, cw, cb, dtb, alog, e_mat]
    return pl.pallas_call(
        functools.partial(_ssd_meta_body, heads=heads),
        grid=(1,),
        in_specs=[_full_spec(a.shape) for a in args],
        out_specs=[_full_spec((SSD_STATE, di)), _full_spec((SUBLANE, cd))],
        out_shape=[jax.ShapeDtypeStruct((SSD_STATE, di), F32),
                   jax.ShapeDtypeStruct((SUBLANE, cd), F32)],
        compiler_params=_cparams(1),
        name="ssd_meta",
    )(*args)


def _ssd_main(xbc, dt, z, s0, tail, cw, cb, dtb, alog, dskip_x, ng, e_mat, *, heads,
              batch):
    t, cd = xbc.shape
    di = heads * SSD_HEAD_DIM
    L = SSD_CHUNK
    nc = t // batch // L
    row = lambda b, c: (b * nc + c, 0)
    in_specs = [pl.BlockSpec((L, cd), row),
                pl.BlockSpec((L, LANE), row),
                pl.BlockSpec((L, di), row)]
    consts = [s0, tail, cw, cb, dtb, alog, dskip_x, ng, e_mat]
    in_specs += [_full_spec(a.shape) for a in consts]
    return pl.pallas_call(
        functools.partial(_ssd_body, heads=heads),
        grid=(batch, nc),
        in_specs=in_specs,
        out_specs=pl.BlockSpec((L, di), row),
        out_shape=jax.ShapeDtypeStruct((t, di), BF16),
        scratch_shapes=[pltpu.VMEM((SSD_STATE, di), F32),
                        pltpu.VMEM((SUBLANE, cd), F32)],
        compiler_params=_cparams(2),
        name="ssd_main",
    )(xbc, dt, z, *consts)


def _sc_body(b_ref, c_ref, v_ref, cm_ref, vm_ref, w_ref, o_ref, prev_ref, *,
             tiles_per_batch):
    @pl.when(pl.program_id(1) % tiles_per_batch == 0)
    def _():
        cvm = cm_ref[...] * vm_ref[...]
        prev_ref[...] = cvm[N_META - SUBLANE:, :]

    cv = c_ref[...] * v_ref[...]
    u = _causal_conv(cv, prev_ref[...], w_ref[...], SC_CONV)
    prev_ref[...] = cv[cv.shape[0] - SUBLANE:, :]
    o_ref[...] = (b_ref[...] * u).astype(o_ref.dtype)


def _short_conv(sc, sc_meta, w, *, width, tm, tn, seq):
    t = sc.shape[0]
    nb = width // tn
    in_specs = [pl.BlockSpec((tm, tn), lambda n, i: (i, n)),
                pl.BlockSpec((tm, tn), lambda n, i: (i, n + nb)),
                pl.BlockSpec((tm, tn), lambda n, i: (i, n + 2 * nb)),
                pl.BlockSpec((N_META, tn), lambda n, i: (0, n + nb)),
                pl.BlockSpec((N_META, tn), lambda n, i: (0, n + 2 * nb)),
                pl.BlockSpec((SC_CONV, tn), lambda n, i: (0, n))]
    return pl.pallas_call(
        functools.partial(_sc_body, tiles_per_batch=seq // tm),
        grid=(nb, t // tm),
        in_specs=in_specs,
        out_specs=pl.BlockSpec((tm, tn), lambda n, i: (i, n)),
        out_shape=jax.ShapeDtypeStruct((t, width), BF16),
        scratch_shapes=[pltpu.VMEM((SUBLANE, tn), F32)],
        compiler_params=_cparams(2),
        name="short_conv",
    )(sc, sc, sc, sc_meta, sc_meta, w)


def _router_body(x_ref, g_ref, wr_ref, br_ref, hp_ref, idx_ref, prob_ref, rank_ref,
                 cnt_ref, base_ref):
    @pl.when(pl.program_id(0) == 0)
    def _():
        base_ref[...] = jnp.zeros_like(base_ref)

    x = x_ref[...]
    tm, d = x.shape
    ne = wr_ref.shape[0]
    ms = jnp.mean(x * x, axis=-1, keepdims=True)
    h = x * lax.rsqrt(ms + EPS) * g_ref[...]
    half = d // 2
    lo_bits = lax.bitcast_convert_type(h[:, :half].astype(BF16).astype(F32), U32)
    hi_bits = lax.bitcast_convert_type(h[:, half:].astype(BF16).astype(F32), U32)
    hp_ref[...] = (lo_bits >> 16) | (hi_bits & jnp.uint32(0xFFFF0000))

    logits = lax.dot_general(wr_ref[...], h, (((1,), (1,)), ((), ())),
                             precision=lax.Precision.HIGHEST,
                             preferred_element_type=F32) + br_ref[:, 0:1]
    eidx = lax.broadcasted_iota(I32, (ne, tm), 0)
    work = logits
    vals, onehots = [], []
    for k in range(TOP_K):
        mx = jnp.max(work, axis=0, keepdims=True)
        sel = jnp.min(jnp.where(work == mx, eidx, ne), axis=0, keepdims=True)
        oh = eidx == sel
        idx_ref[k:k + 1, :] = sel
        vals.append(mx)
        onehots.append(oh)
        work = jnp.where(oh, -jnp.inf, work)
    exps = [jnp.exp(v - vals[0]) for v in vals]
    denom = exps[0]
    for e in exps[1:]:
        denom = denom + e
    for k in range(TOP_K):
        prob_ref[k:k + 1, :] = exps[k] / denom

    cnt = onehots[0].astype(F32)
    for oh in onehots[1:]:
        cnt = cnt + oh.astype(F32)
    r = lax.broadcasted_iota(I32, (tm, tm), 0)
    c = lax.broadcasted_iota(I32, (tm, tm), 1)
    before = (r < c).astype(BF16)
    prefix = jnp.dot(cnt.astype(BF16), before, preferred_element_type=F32)
    prefix = prefix + base_ref[:, 0:1]
    for k in range(TOP_K):
        rk = jnp.sum(jnp.where(onehots[k], prefix, 0.0), axis=0, keepdims=True)
        rank_ref[k:k + 1, :] = rk.astype(I32)
    base_ref[...] = base_ref[...] + jnp.sum(cnt, axis=1, keepdims=True)
    cnt_ref[...] = base_ref[...]


def _router(hres, g, wr_t, br, tm):
    t, d = hres.shape
    ne = wr_t.shape[0]
    tok = lambda i: (0, i)
    return pl.pallas_call(
        _router_body,
        grid=(t // tm,),
        in_specs=[pl.BlockSpec((tm, d), lambda i: (i, 0)),
                  _full_spec((1, d)), _full_spec((ne, d)), _full_spec((ne, LANE))],
        out_specs=[pl.BlockSpec((tm, d // 2), lambda i: (i, 0)),
                   pl.BlockSpec((TOP_K, tm), tok),
                   pl.BlockSpec((TOP_K, tm), tok),
                   pl.BlockSpec((TOP_K, tm), tok),
                   _full_spec((ne, LANE))],
        out_shape=[jax.ShapeDtypeStruct((t, d // 2), U32),
                   jax.ShapeDtypeStruct((TOP_K, t), I32),
                   jax.ShapeDtypeStruct((TOP_K, t), F32),
                   jax.ShapeDtypeStruct((TOP_K, t), I32),
                   jax.ShapeDtypeStruct((ne, LANE), F32)],
        scratch_shapes=[pltpu.VMEM((ne, LANE), F32)],
        compiler_params=_cparams(1),
        name="router",
    )(hres, g.reshape(1, d), wr_t, jnp.broadcast_to(br.reshape(ne, 1), (ne, LANE)))


def _gather_body(tok_ref, used_ref, src_ref, o_ref, sem):
    i = pl.program_id(0)
    blk = o_ref.shape[0]

    @pl.when(used_ref[i] != 0)
    def _():
        def issue(r, carry):
            t = tok_ref[i * blk + r]
            pltpu.make_async_copy(src_ref.at[pl.ds(t, 1)], o_ref.at[pl.ds(r, 1)], sem).start()
            return carry
        lax.fori_loop(0, blk, issue, 0)
        pltpu.make_async_copy(src_ref.at[pl.ds(0, blk)], o_ref, sem).wait()

    @pl.when(used_ref[i] == 0)
    def _():
        o_ref[...] = jnp.zeros_like(o_ref)


def _moe_gather(tok_slot, blk_used, hp, n_rows):
    w = hp.shape[1]
    return pl.pallas_call(
        _gather_body,
        grid_spec=pltpu.PrefetchScalarGridSpec(
            num_scalar_prefetch=2,
            grid=(n_rows // MOE_BLK,),
            in_specs=[pl.BlockSpec(memory_space=pl.ANY)],
            out_specs=pl.BlockSpec((MOE_BLK, w), lambda i, *_: (i, 0)),
            scratch_shapes=[pltpu.SemaphoreType.DMA],
        ),
        out_shape=jax.ShapeDtypeStruct((n_rows, w), U32),
        compiler_params=_cparams(1),
        name="moe_gather",
    )(tok_slot, blk_used, hp)


def _sb_specs(rows, width_in, tile, n_tiles):
    def x_map(s, j, e, xblk, nblk):
        return (xblk[s], 0)

    def wcol(s, j, nblk):
        return jnp.where(nblk[s] > 0, j, n_tiles - 1)

    x_spec = pl.BlockSpec((rows, width_in), x_map)
    o_spec = pl.BlockSpec((rows, tile), lambda s, j, e, xblk, nblk: (s, j))
    return x_spec, wcol, o_spec


def _sb_loop(nb, compute_block, o_ref):
    def block(b, carry):
        compute_block(pl.multiple_of(b * MOE_BLK, MOE_BLK))
        return carry
    lax.fori_loop(0, nb, block, 0)

    def zero(b, carry):
        r0 = pl.multiple_of(b * MOE_BLK, MOE_BLK)
        o_ref[pl.ds(r0, MOE_BLK), :] = jnp.zeros((MOE_BLK, o_ref.shape[1]), o_ref.dtype)
        return carry
    lax.fori_loop(nb, MOE_SB_BLKS, zero, 0)


def _gate_up_body(e_ref, xblk_ref, nblk_ref, x_ref, wg_ref, wu_ref, bg_ref, bu_ref,
                  o_ref, wbf_ref):
    s = pl.program_id(0)
    half = x_ref.shape[1]
    tj = wg_ref.shape[1]
    nb = nblk_ref[s]

    @pl.when(nb > 0)
    def _():
        wbf_ref[:, :tj] = wg_ref[...].astype(BF16)
        wbf_ref[:, tj:] = wu_ref[...].astype(BF16)

    bias = jnp.concatenate([bg_ref[...], bu_ref[...]], axis=1)

    def compute_block(r0):
        xp = x_ref[pl.ds(r0, MOE_BLK), :]
        x_lo = lax.bitcast_convert_type(xp << 16, F32).astype(BF16)
        x_hi = lax.bitcast_convert_type(xp & jnp.uint32(0xFFFF0000), F32).astype(BF16)
        gu = (jnp.dot(x_lo, wbf_ref[:half, :], preferred_element_type=F32)
              + jnp.dot(x_hi, wbf_ref[half:, :], preferred_element_type=F32) + bias)
        gate = jnp.minimum(gu[:, :tj], SWIGLU_LIMIT)
        up = jnp.clip(gu[:, tj:], -SWIGLU_LIMIT, SWIGLU_LIMIT)
        act = (up + 1.0) * (gate * _sigmoid(gate * SWIGLU_ALPHA))
        o_ref[pl.ds(r0, MOE_BLK), :] = act.astype(o_ref.dtype)

    _sb_loop(nb, compute_block, o_ref)


def _moe_gate_up(sb, xs, w_gu, b_gu, *, tj):
    ne, d, de2 = w_gu.shape
    de = de2 // 2
    nj = de // tj
    rows = MOE_SB_BLKS * MOE_BLK
    n_rows, half = xs.shape
    x_spec, wcol, o_spec = _sb_specs(rows, half, tj, nj)
    in_specs = [
        x_spec,
        pl.BlockSpec((None, d, tj), lambda s, j, e, xb, nb: (e[s], 0, wcol(s, j, nb))),
        pl.BlockSpec((None, d, tj), lambda s, j, e, xb, nb: (e[s], 0, nj + wcol(s, j, nb))),
        pl.BlockSpec((None, 1, tj), lambda s, j, e, xb, nb: (e[s], 0, wcol(s, j, nb))),
        pl.BlockSpec((None, 1, tj), lambda s, j, e, xb, nb: (e[s], 0, nj + wcol(s, j, nb))),
    ]
    return pl.pallas_call(
        _gate_up_body,
        grid_spec=pltpu.PrefetchScalarGridSpec(
            num_scalar_prefetch=3,
            grid=(sb[0].shape[0], nj),
            in_specs=in_specs,
            out_specs=o_spec,
            scratch_shapes=[pltpu.VMEM((d, 2 * tj), BF16)],
        ),
        out_shape=jax.ShapeDtypeStruct((n_rows, de), BF16),
        compiler_params=_cparams(2),
        name="moe_gate_up",
    )(*sb, xs, w_gu, w_gu, b_gu.reshape(ne, 1, de2), b_gu.reshape(ne, 1, de2))


def _down_body(e_ref, xblk_ref, nblk_ref, a_ref, w_ref, b_ref, o_ref, wbf_ref):
    s = pl.program_id(0)
    nb = nblk_ref[s]

    @pl.when(nb > 0)
    def _():
        wbf_ref[...] = w_ref[...].astype(BF16)

    bias = b_ref[...]

    def compute_block(r0):
        o_ref[pl.ds(r0, MOE_BLK), :] = jnp.dot(
            a_ref[pl.ds(r0, MOE_BLK), :], wbf_ref[...], preferred_element_type=F32) + bias

    _sb_loop(nb, compute_block, o_ref)


def _moe_down(sb, act, w_down, b_down, *, tn):
    ne, de, d = w_down.shape
    nn = d // tn
    rows = MOE_SB_BLKS * MOE_BLK
    n_rows = act.shape[0]
    a_spec, wcol, o_spec = _sb_specs(rows, de, tn, nn)
    in_specs = [
        a_spec,
        pl.BlockSpec((None, de, tn), lambda s, j, e, xb, nb: (e[s], 0, wcol(s, j, nb))),
        pl.BlockSpec((None, 1, tn), lambda s, j, e, xb, nb: (e[s], 0, wcol(s, j, nb))),
    ]
    return pl.pallas_call(
        _down_body,
        grid_spec=pltpu.PrefetchScalarGridSpec(
            num_scalar_prefetch=3,
            grid=(sb[0].shape[0], nn),
            in_specs=in_specs,
            out_specs=o_spec,
            scratch_shapes=[pltpu.VMEM((de, tn), BF16)],
        ),
        out_shape=jax.ShapeDtypeStruct((n_rows, d), F32),
        compiler_params=_cparams(2),
        name="moe_down",
    )(*sb, act, w_down, b_down.reshape(ne, 1, d))


def _moe_schedule(cnt, n_pairs, ne):
    sb_rows = MOE_SB_BLKS * MOE_BLK
    s_max = n_pairs // sb_rows + ne
    n_rows = s_max * sb_rows
    pblk = (cnt + MOE_BLK - 1) // MOE_BLK
    nsb_e = (pblk + MOE_SB_BLKS - 1) // MOE_SB_BLKS
    sb_end = jnp.cumsum(nsb_e)
    sb_first = sb_end - nsb_e
    n_sb = sb_end[-1]
    seg_start = sb_first * sb_rows
    s_ids = jnp.arange(s_max, dtype=I32)
    used = s_ids < n_sb
    e_of_s = jnp.minimum(jnp.searchsorted(sb_end, s_ids, side="right"), ne - 1).astype(I32)
    i_in_e = s_ids - sb_first[e_of_s]
    nblk = jnp.where(used, jnp.clip(pblk[e_of_s] - MOE_SB_BLKS * i_in_e, 0, MOE_SB_BLKS), 0)
    last = jnp.maximum(n_sb - 1, 0)
    sb = (jnp.where(used, e_of_s, e_of_s[last]).astype(I32),
          jnp.minimum(s_ids, last).astype(I32), nblk.astype(I32))
    blk_ids = jnp.arange(n_rows // MOE_BLK, dtype=I32)
    blk_used = (blk_ids % MOE_SB_BLKS < nblk[blk_ids // MOE_SB_BLKS]).astype(I32)
    return sb, seg_start, n_rows, blk_used


def _combine_body(dest_ref, y_ref, h_ref, p_ref, g_ref, o_ref, ybuf, sems, *, t_total):
    i = pl.program_id(0)
    n = pl.num_programs(0)
    tc = h_ref.shape[0]

    def issue(tile, slot):
        for k in range(TOP_K):
            def one(r, carry, k=k):
                row = dest_ref[k * t_total + tile * tc + r]
                pltpu.make_async_copy(y_ref.at[pl.ds(row, 1)],
                                      ybuf.at[slot, k, pl.ds(r, 1)],
                                      sems.at[slot]).start()
                return carry
            lax.fori_loop(0, tc, one, 0)

    @pl.when(i == 0)
    def _():
        issue(0, 0)

    slot = i % 2
    for k in range(TOP_K):
        pltpu.make_async_copy(y_ref.at[pl.ds(0, tc)], ybuf.at[slot, k], sems.at[slot]).wait()

    @pl.when(i + 1 < n)
    def _():
        issue(i + 1, 1 - slot)

    acc = h_ref[...]
    p = p_ref[...]
    for k in range(TOP_K):
        acc = acc + p[:, k:k + 1] * ybuf[slot, k]
    ms = jnp.mean(acc * acc, axis=-1, keepdims=True)
    o_ref[...] = acc * lax.rsqrt(ms + EPS) * g_ref[...]


def _combine(dest_flat, y_sorted, hres, p_t, final_g, *, tc):
    t, d = hres.shape
    return pl.pallas_call(
        functools.partial(_combine_body, t_total=t),
        grid_spec=pltpu.PrefetchScalarGridSpec(
            num_scalar_prefetch=1,
            grid=(t // tc,),
            in_specs=[pl.BlockSpec(memory_space=pl.ANY),
                      pl.BlockSpec((tc, d), lambda i, *_: (i, 0)),
                      pl.BlockSpec((tc, TOP_K), lambda i, *_: (i, 0)),
                      pl.BlockSpec((1, d), lambda i, *_: (0, 0))],
            out_specs=pl.BlockSpec((tc, d), lambda i, *_: (i, 0)),
            scratch_shapes=[pltpu.VMEM((2, TOP_K, tc, d), F32),
                            pltpu.SemaphoreType.DMA((2,))],
        ),
        out_shape=jax.ShapeDtypeStruct((t, d), F32),
        compiler_params=_cparams(1),
        name="moe_combine",
    )(dest_flat, y_sorted, hres, p_t, final_g.reshape(1, d))


def _pick(n, prefs):
    for p in prefs:
        if n % p == 0:
            return p
    raise ValueError(f"no tile in {prefs} divides {n}")


def kernel(x, meta_tokens, norm_mix_g, w_in, ssd_conv_w, ssd_conv_b, dt_bias, a_log, d_skip,
           ssd_norm_g, w_ssd_out, sc_conv_w, w_sc_out, w_o, norm_ffn_g, w_router, b_router,
           w_gate_up, b_gate_up, w_down, b_down, final_g):
    batch, seq, d = x.shape
    assert norm_mix_g.shape[0] == 1, "meta-token handling assumes a single layer"
    assert seq % SSD_CHUNK == 0 and N_META <= SSD_CHUNK
    t = batch * seq
    heads = d // SSD_HEAD_DIM
    gn = SSD_GROUPS * SSD_STATE
    conv_dim = d + 2 * gn
    xbc_off = d
    dt_off = xbc_off + conv_dim
    sc_off = dt_off + heads
    gate_off = sc_off + 3 * d
    ne = w_router.shape[-1]
    de = w_down.shape[2]

    x2 = x.reshape(t, d)
    w_in_nk = jnp.swapaxes(w_in[0], 0, 1)

    tm = _pick(t, (1024, 512, 256, 128))
    tn = _pick(d, (512, 256, 128))

    h = _rmsnorm_bf16(x2, norm_mix_g[0], _pick(t, (256, 128)))
    h_meta = _rmsnorm_bf16(meta_tokens, norm_mix_g[0], N_META)

    proj = functools.partial(_matmul, h, w_in_nk, tm=tm, out_dtype=F32, w_is_nk=True)
    z = proj(0, d, tn=tn, name="proj_z")
    xbc, xbc_meta = proj(xbc_off, conv_dim, tn=tn, side=h_meta, name="proj_xbc")
    dtr, dt_meta = proj(dt_off, LANE, tn=LANE, side=h_meta, name="proj_dt")
    sc, sc_meta = proj(sc_off, 3 * d, tn=tn, side=h_meta, name="proj_sc")
    gates = proj(gate_off, 2 * d, tn=tn, name="proj_gate")

    e_mat = (jnp.arange(d, dtype=I32)[None, :] // SSD_HEAD_DIM
             == jnp.arange(heads, dtype=I32)[:, None]).astype(F32)
    pad = SSD_CHUNK - N_META
    cw = ssd_conv_w[0]
    cb = ssd_conv_b[0].reshape(1, conv_dim)
    dtb = dt_bias[0].reshape(1, heads)
    alog = a_log[0].reshape(1, heads)
    s0, tail = _ssd_meta(jnp.pad(xbc_meta, ((pad, 0), (0, 0))),
                         jnp.pad(dt_meta, ((pad, 0), (0, 0))),
                         cw, cb, dtb, alog, e_mat, heads)
    dskip_x = jnp.repeat(d_skip[0], SSD_HEAD_DIM).reshape(1, d)
    g_ssd = _ssd_main(xbc, dtr, z, s0, tail, cw, cb, dtb, alog, dskip_x,
                      ssd_norm_g[0].reshape(1, d), e_mat, heads=heads, batch=batch)

    sc_tm = _pick(seq, (512, 256, 128))
    u_sc = _short_conv(sc, sc_meta, sc_conv_w[0], width=d, tm=sc_tm, tn=tn, seq=seq)

    y_a = _matmul(g_ssd, w_ssd_out[0], 0, d, tm=tm, tn=tn, out_dtype=F32,
                  extras=[(gates, 0)],
                  epilogue=lambda acc, ga: _sigmoid(ga) * acc, name="ssd_out")
    mixed = _matmul(u_sc, w_sc_out[0], 0, d, tm=tm, tn=tn, out_dtype=BF16,
                    extras=[(gates, d), (y_a, 0)],
                    epilogue=lambda acc, gb, ya: ya + _sigmoid(gb) * acc, name="sc_out")
    hres = _matmul(mixed, w_o[0], 0, d, tm=tm, tn=tn, out_dtype=F32,
                   extras=[(x2, 0)], epilogue=lambda acc, xr: xr + acc, name="w_o")

    rt = _pick(t, (512, 256, 128))
    hp, top_idx, probs, rank, counts = _router(hres, norm_ffn_g[0], w_router[0].T,
                                               b_router[0], rt)
    cnt = counts[:, 0].astype(I32)
    sb, seg_start, n_slots, blk_used = _moe_schedule(cnt, t * TOP_K, ne)
    e_ids = jnp.arange(ne, dtype=I32)[:, None, None]
    dest = rank + jnp.sum(jnp.where(top_idx[None] == e_ids, seg_start[:, None, None], 0),
                          axis=0)
    tok_ids = jnp.broadcast_to(jnp.arange(t, dtype=I32)[None, :], (TOP_K, t))
    tok_slot = jnp.zeros((n_slots,), I32).at[dest.reshape(-1)].set(tok_ids.reshape(-1))

    xs = _moe_gather(tok_slot, blk_used, hp, n_slots)
    act = _moe_gate_up(sb, xs, w_gate_up[0], b_gate_up[0], tj=_pick(de, (256, 128)))
    y_sorted = _moe_down(sb, act, w_down[0], b_down[0],
                         tn=_pick(d, (1024, 512, 256, 128)))
    out = _combine(dest.reshape(-1), y_sorted, hres, probs.T, final_g,
                   tc=_pick(t, (128,)))
    return out.reshape(batch, seq, d)
```

```python
import functools

import jax
import jax.numpy as jnp
from jax import lax
from jax.experimental import pallas as pl
from jax.experimental.pallas import tpu as pltpu

F32 = jnp.float32
BF16 = jnp.bfloat16
I32 = jnp.int32
U32 = jnp.uint32

N_META = 16
SSD_HEAD_DIM = 64
SSD_GROUPS = 8
SSD_STATE = 128
SSD_CONV = 4
SSD_CHUNK = 128
SC_CONV = 3
TOP_K = 4
SWIGLU_LIMIT = 7.0
SWIGLU_ALPHA = 1.702
EPS = 1e-5

LANE = 128
SUBLANE = 8
MOE_BLK = 256
MOE_SB_BLKS = 5
VMEM_MB = 56


def _cparams(n_axes, vmem_mb=VMEM_MB):
    return pltpu.CompilerParams(
        dimension_semantics=("arbitrary",) * n_axes,
        vmem_limit_bytes=vmem_mb * 1024 * 1024,
    )


def _sigmoid(v):
    return 1.0 / (1.0 + jnp.exp(-v))


def _rms_body(x_ref, g_ref, o_ref):
    x = x_ref[...]
    ms = jnp.mean(x * x, axis=-1, keepdims=True)
    o_ref[...] = (x * lax.rsqrt(ms + EPS) * g_ref[...]).astype(o_ref.dtype)


def _rmsnorm_bf16(x2, g, tm):
    m, d = x2.shape
    return pl.pallas_call(
        _rms_body,
        grid=(m // tm,),
        in_specs=[pl.BlockSpec((tm, d), lambda i: (i, 0)),
                  pl.BlockSpec((1, d), lambda i: (0, 0))],
        out_specs=pl.BlockSpec((tm, d), lambda i: (i, 0)),
        out_shape=jax.ShapeDtypeStruct((m, d), BF16),
        compiler_params=_cparams(1),
        name="rmsnorm",
    )(x2, g.reshape(1, d))


def _mm_body(*refs, n_extra, has_side, epilogue, w_is_nk):
    it = iter(refs)
    a_ref = next(it)
    w_ref = next(it)
    side_ref = next(it) if has_side else None
    extras = [next(it) for _ in range(n_extra)]
    o_ref = next(it)
    oside_ref = next(it) if has_side else None
    wbf_ref = next(it)

    @pl.when(pl.program_id(1) == 0)
    def _():
        w = w_ref[...]
        wbf_ref[...] = (w.T if w_is_nk else w).astype(BF16)
        if has_side:
            oside_ref[...] = jnp.dot(side_ref[...], wbf_ref[...],
                                     preferred_element_type=F32)

    acc = jnp.dot(a_ref[...], wbf_ref[...], preferred_element_type=F32)
    if epilogue is not None:
        acc = epilogue(acc, *[e[...] for e in extras])
    o_ref[...] = acc.astype(o_ref.dtype)


def _matmul(a, w, col_off, n_cols, *, tm, tn, out_dtype, side=None, extras=(),
            epilogue=None, w_is_nk=False, name="matmul"):
    m, k = a.shape
    assert m % tm == 0 and n_cols % tn == 0
    if w_is_nk:
        assert col_off % SUBLANE == 0
        w_spec = pl.BlockSpec((pl.Element(tn), pl.Element(k)),
                              lambda n, i: (pl.multiple_of(col_off + n * tn, SUBLANE), 0))
    else:
        assert col_off % tn == 0
        w_spec = pl.BlockSpec((k, tn), lambda n, i, cb=col_off // tn: (0, n + cb))
    in_specs = [pl.BlockSpec((tm, k), lambda n, i: (i, 0)), w_spec]
    args = [a, w]
    if side is not None:
        in_specs.append(pl.BlockSpec((side.shape[0], k), lambda n, i: (0, 0)))
        args.append(side)
    for arr, off in extras:
        assert off % tn == 0
        in_specs.append(pl.BlockSpec((tm, tn), lambda n, i, o=off // tn: (i, n + o)))
        args.append(arr)
    out_shape = [jax.ShapeDtypeStruct((m, n_cols), out_dtype)]
    out_specs = [pl.BlockSpec((tm, tn), lambda n, i: (i, n))]
    if side is not None:
        out_shape.append(jax.ShapeDtypeStruct((side.shape[0], n_cols), F32))
        out_specs.append(pl.BlockSpec((side.shape[0], tn), lambda n, i: (0, n)))
    res = pl.pallas_call(
        functools.partial(_mm_body, n_extra=len(extras), has_side=side is not None,
                          epilogue=epilogue, w_is_nk=w_is_nk),
        grid=(n_cols // tn, m // tm),
        in_specs=in_specs,
        out_specs=out_specs,
        out_shape=out_shape,
        scratch_shapes=[pltpu.VMEM((k, tn), BF16)],
        compiler_params=_cparams(2),
        name=name,
    )(*args)
    return res if side is not None else res[0]


def _causal_conv(raw, prev, w, n_tap):
    rows = raw.shape[0]
    acc = raw * w[n_tap - 1:n_tap, :]
    row8 = lax.broadcasted_iota(I32, (SUBLANE, raw.shape[1]), 0)
    for s in range(1, n_tap):
        sh = pltpu.roll(raw, s, 0)
        shp = pltpu.roll(prev, s, 0)
        first = jnp.where(row8 < s, shp, sh[:SUBLANE])
        sh = jnp.concatenate([first, sh[SUBLANE:]], axis=0)
        acc = acc + sh * w[n_tap - 1 - s:n_tap - s, :]
    del rows
    return acc


def _ssd_chunk(raw, prev, dt_raw, st, cw, cb, dtb, alog, e_mat, *, heads, meta):
    L = raw.shape[0]
    di = heads * SSD_HEAD_DIM
    gn = SSD_GROUPS * SSD_STATE
    acc = _causal_conv(raw, prev, cw, SSD_CONV) + cb
    xbc = acc * _sigmoid(acc)
    dt_in = dt_raw[:, :heads] + dtb
    dtv = jnp.maximum(dt_in, 0.0) + jnp.log1p(jnp.exp(-jnp.abs(dt_in)))
    if meta:
        valid = (lax.broadcasted_iota(I32, (L, 1), 0) >= L - N_META).astype(F32)
        xbc = xbc * valid
        dtv = dtv * valid
    xs = xbc[:, :di]
    bm = xbc[:, di:di + gn]
    cm = xbc[:, di + gn:]
    a_neg = -jnp.exp(alog)
    ad = dtv * a_neg
    r = lax.broadcasted_iota(I32, (L, L), 0)
    c = lax.broadcasted_iota(I32, (L, L), 1)
    tril = r >= c
    acum = jnp.dot(tril.astype(F32), ad, precision=lax.Precision.HIGHEST,
                   preferred_element_type=F32)
    acum_x = jnp.dot(acum, e_mat, precision=lax.Precision.HIGHEST,
                     preferred_element_type=F32)
    dt_x = jnp.dot(dtv, e_mat, precision=lax.Precision.HIGHEST,
                   preferred_element_type=F32)
    alast_x = acum_x[L - 1:L, :]
    x_dt = xs * dt_x
    x_dec = (x_dt * jnp.exp(alast_x - acum_x)).astype(BF16)
    bt = bm.T.astype(BF16)
    return dict(xs=xs, cm=cm, bt=bt, acum=acum, acum_x=acum_x, alast_x=alast_x,
                x_dt=x_dt, x_dec=x_dec, tril=tril, st=st)


def _ssd_state_update(q, st_ref, heads):
    gw = heads * SSD_HEAD_DIM // SSD_GROUPS
    for g in range(SSD_GROUPS):
        sl = slice(g * gw, (g + 1) * gw)
        btg = q["bt"][g * SSD_STATE:(g + 1) * SSD_STATE, :]
        new = jnp.dot(btg, q["x_dec"][:, sl], preferred_element_type=F32)
        st_ref[:, sl] = st_ref[:, sl] * jnp.exp(q["alast_x"][:, sl]) + new


def _ssd_body(xbc_ref, dt_ref, z_ref, s0_ref, tail_ref, cw_ref, cb_ref, dtb_ref,
              alog_ref, dskip_ref, ng_ref, e_ref, o_ref, st_ref, prev_ref, *, heads):
    @pl.when(pl.program_id(1) == 0)
    def _():
        st_ref[...] = s0_ref[...]
        prev_ref[...] = tail_ref[...]

    raw = xbc_ref[...]
    L = raw.shape[0]
    q = _ssd_chunk(raw, prev_ref[...], dt_ref[...], None, cw_ref[...], cb_ref[...],
                   dtb_ref[...], alog_ref[...], e_ref[...], heads=heads, meta=False)
    prev_ref[...] = raw[L - SUBLANE:, :]

    gw = heads * SSD_HEAD_DIM // SSD_GROUPS
    j_heads = heads // SSD_GROUPS
    acum = q["acum"]
    acum_t = acum.T
    cmb = q["cm"].astype(BF16)
    x_dt_b = q["x_dt"].astype(BF16)
    lane = lax.broadcasted_iota(I32, (L, 2 * SSD_HEAD_DIM), 1)
    lo = lane < SSD_HEAD_DIM
    neg_inf = jnp.float32(-jnp.inf)

    def decay_mat(h):
        col = jnp.broadcast_to(acum[:, h:h + 1], (L, L))
        row = jnp.broadcast_to(acum_t[h:h + 1, :], (L, L))
        return jnp.exp(jnp.where(q["tril"], col - row, neg_inf))

    for g in range(SSD_GROUPS):
        sl = slice(g * gw, (g + 1) * gw)
        cg = cmb[:, g * SSD_STATE:(g + 1) * SSD_STATE]
        btg = q["bt"][g * SSD_STATE:(g + 1) * SSD_STATE, :]
        cb_mat = jnp.dot(cg, btg, preferred_element_type=F32)
        parts = []
        for p in range(j_heads // 2):
            h1 = g * j_heads + 2 * p
            slab = x_dt_b[:, h1 * SSD_HEAD_DIM:(h1 + 2) * SSD_HEAD_DIM]
            lhs = jnp.concatenate([(cb_mat * decay_mat(h1)).astype(BF16),
                                   (cb_mat * decay_mat(h1 + 1)).astype(BF16)], axis=1)
            zero = jnp.zeros_like(slab)
            rhs = jnp.concatenate([jnp.where(lo, slab, zero),
                                   jnp.where(lo, zero, slab)], axis=0)
            parts.append(jnp.dot(lhs, rhs, preferred_element_type=F32))
        y_diag = parts[0] if len(parts) == 1 else jnp.concatenate(parts, axis=1)
        st_g = st_ref[:, sl]
        y_off = jnp.dot(cg, st_g.astype(BF16), preferred_element_type=F32)
        y = y_diag + y_off * jnp.exp(q["acum_x"][:, sl]) + dskip_ref[:, sl] * q["xs"][:, sl]
        zg = z_ref[:, sl]
        gated = y * (zg * _sigmoid(zg))
        ms = jnp.mean(gated * gated, axis=-1, keepdims=True)
        o_ref[:, sl] = (gated * lax.rsqrt(ms + EPS) * ng_ref[:, sl]).astype(o_ref.dtype)
        new = jnp.dot(btg, q["x_dec"][:, sl], preferred_element_type=F32)
        st_ref[:, sl] = st_g * jnp.exp(q["alast_x"][:, sl]) + new


def _ssd_meta_body(xbc_ref, dt_ref, cw_ref, cb_ref, dtb_ref, alog_ref, e_ref,
                   st_ref, tail_ref, *, heads):
    raw = xbc_ref[...]
    L = raw.shape[0]
    st_ref[...] = jnp.zeros_like(st_ref)
    prev = jnp.zeros((SUBLANE, raw.shape[1]), F32)
    q = _ssd_chunk(raw, prev, dt_ref[...], None, cw_ref[...], cb_ref[...],
                   dtb_ref[...], alog_ref[...], e_ref[...], heads=heads, meta=True)
    _ssd_state_update(q, st_ref, heads)
    tail_ref[...] = raw[L - SUBLANE:, :]


def _full_spec(shape):
    nd = len(shape)
    return pl.BlockSpec(shape, lambda *_: (0,) * nd)


def _ssd_meta(xbc_m, dt_m, cw, cb, dtb, alog, e_mat, heads):
    L, cd = xbc_m.shape
    di = heads * SSD_HEAD_DIM
    args = [xbc_m, dt_m, cw, cb, dtb, alog, e_mat]
    return pl.pallas_call(
        functools.partial(_ssd_meta_body, heads=heads),
        grid=(1,),
        in_specs=[_full_spec(a.shape) for a in args],
        out_specs=[_full_spec((SSD_STATE, di)), _full_spec((SUBLANE, cd))],
        out_shape=[jax.ShapeDtypeStruct((SSD_STATE, di), F32),
                   jax.ShapeDtypeStruct((SUBLANE, cd), F32)],
        compiler_params=_cparams(1),
        name="ssd_meta",
    )(*args)


def _ssd_main(xbc, dt, z, s0, tail, cw, cb, dtb, alog, dskip_x, ng, e_mat, *, heads,
              batch):
    t, cd = xbc.shape
    di = heads * SSD_HEAD_DIM
    L = SSD_CHUNK
    nc = t // batch // L
    row = lambda b, c: (b * nc + c, 0)
    in_specs = [pl.BlockSpec((L, cd), row),
                pl.BlockSpec((L, LANE), row),
                pl.BlockSpec((L, di), row)]
    consts = [s0, tail, cw, cb, dtb, alog, dskip_x, ng, e_mat]
    in_specs += [_full_spec(a.shape) for a in consts]
    return pl.pallas_call(
        functools.partial(_ssd_body, heads=heads),
        grid=(batch, nc),
        in_specs=in_specs,
        out_specs=pl.BlockSpec((L, di), row),
        out_shape=jax.ShapeDtypeStruct((t, di), BF16),
        scratch_shapes=[pltpu.VMEM((SSD_STATE, di), F32),
                        pltpu.VMEM((SUBLANE, cd), F32)],
        compiler_params=_cparams(2),
        name="ssd_main",
    )(xbc, dt, z, *consts)


def _sc_body(b_ref, c_ref, v_ref, cm_ref, vm_ref, w_ref, o_ref, prev_ref, *,
             tiles_per_batch):
    @pl.when(pl.program_id(1) % tiles_per_batch == 0)
    def _():
        cvm = cm_ref[...] * vm_ref[...]
        prev_ref[...] = cvm[N_META - SUBLANE:, :]

    cv = c_ref[...] * v_ref[...]
    u = _causal_conv(cv, prev_ref[...], w_ref[...], SC_CONV)
    prev_ref[...] = cv[cv.shape[0] - SUBLANE:, :]
    o_ref[...] = (b_ref[...] * u).astype(o_ref.dtype)


def _short_conv(sc, sc_meta, w, *, width, tm, tn, seq):
    t = sc.shape[0]
    nb = width // tn
    in_specs = [pl.BlockSpec((tm, tn), lambda n, i: (i, n)),
                pl.BlockSpec((tm, tn), lambda n, i: (i, n + nb)),
                pl.BlockSpec((tm, tn), lambda n, i: (i, n + 2 * nb)),
                pl.BlockSpec((N_META, tn), lambda n, i: (0, n + nb)),
                pl.BlockSpec((N_META, tn), lambda n, i: (0, n + 2 * nb)),
                pl.BlockSpec((SC_CONV, tn), lambda n, i: (0, n))]
    return pl.pallas_call(
        functools.partial(_sc_body, tiles_per_batch=seq // tm),
        grid=(nb, t // tm),
        in_specs=in_specs,
        out_specs=pl.BlockSpec((tm, tn), lambda n, i: (i, n)),
        out_shape=jax.ShapeDtypeStruct((t, width), BF16),
        scratch_shapes=[pltpu.VMEM((SUBLANE, tn), F32)],
        compiler_params=_cparams(2),
        name="short_conv",
    )(sc, sc, sc, sc_meta, sc_meta, w)


def _router_body(x_ref, g_ref, wr_ref, br_ref, hp_ref, idx_ref, prob_ref, rank_ref,
                 cnt_ref, base_ref):
    @pl.when(pl.program_id(0) == 0)
    def _():
        base_ref[...] = jnp.zeros_like(base_ref)

    x = x_ref[...]
    tm, d = x.shape
    ne = wr_ref.shape[0]
    ms = jnp.mean(x * x, axis=-1, keepdims=True)
    h = x * lax.rsqrt(ms + EPS) * g_ref[...]
    half = d // 2
    lo_bits = lax.bitcast_convert_type(h[:, :half].astype(BF16).astype(F32), U32)
    hi_bits = lax.bitcast_convert_type(h[:, half:].astype(BF16).astype(F32), U32)
    hp_ref[...] = (lo_bits >> 16) | (hi_bits & jnp.uint32(0xFFFF0000))

    logits = lax.dot_general(wr_ref[...], h, (((1,), (1,)), ((), ())),
                             precision=lax.Precision.HIGHEST,
                             preferred_element_type=F32) + br_ref[:, 0:1]
    eidx = lax.broadcasted_iota(I32, (ne, tm), 0)
    work = logits
    vals, onehots = [], []
    for k in range(TOP_K):
        mx = jnp.max(work, axis=0, keepdims=True)
        sel = jnp.min(jnp.where(work == mx, eidx, ne), axis=0, keepdims=True)
        oh = eidx == sel
        idx_ref[k:k + 1, :] = sel
        vals.append(mx)
        onehots.append(oh)
        work = jnp.where(oh, -jnp.inf, work)
    exps = [jnp.exp(v - vals[0]) for v in vals]
    denom = exps[0]
    for e in exps[1:]:
        denom = denom + e
    for k in range(TOP_K):
        prob_ref[k:k + 1, :] = exps[k] / denom

    cnt = onehots[0].astype(F32)
    for oh in onehots[1:]:
        cnt = cnt + oh.astype(F32)
    r = lax.broadcasted_iota(I32, (tm, tm), 0)
    c = lax.broadcasted_iota(I32, (tm, tm), 1)
    before = (r < c).astype(BF16)
    prefix = jnp.dot(cnt.astype(BF16), before, preferred_element_type=F32)
    prefix = prefix + base_ref[:, 0:1]
    for k in range(TOP_K):
        rk = jnp.sum(jnp.where(onehots[k], prefix, 0.0), axis=0, keepdims=True)
        rank_ref[k:k + 1, :] = rk.astype(I32)
    base_ref[...] = base_ref[...] + jnp.sum(cnt, axis=1, keepdims=True)
    cnt_ref[...] = base_ref[...]


def _router(hres, g, wr_t, br, tm):
    t, d = hres.shape
    ne = wr_t.shape[0]
    tok = lambda i: (0, i)
    return pl.pallas_call(
        _router_body,
        grid=(t // tm,),
        in_specs=[pl.BlockSpec((tm, d), lambda i: (i, 0)),
                  _full_spec((1, d)), _full_spec((ne, d)), _full_spec((ne, LANE))],
        out_specs=[pl.BlockSpec((tm, d // 2), lambda i: (i, 0)),
                   pl.BlockSpec((TOP_K, tm), tok),
                   pl.BlockSpec((TOP_K, tm), tok),
                   pl.BlockSpec((TOP_K, tm), tok),
                   _full_spec((ne, LANE))],
        out_shape=[jax.ShapeDtypeStruct((t, d // 2), U32),
                   jax.ShapeDtypeStruct((TOP_K, t), I32),
                   jax.ShapeDtypeStruct((TOP_K, t), F32),
                   jax.ShapeDtypeStruct((TOP_K, t), I32),
                   jax.ShapeDtypeStruct((ne, LANE), F32)],
        scratch_shapes=[pltpu.VMEM((ne, LANE), F32)],
        compiler_params=_cparams(1),
        name="router",
    )(hres, g.reshape(1, d), wr_t, jnp.broadcast_to(br.reshape(ne, 1), (ne, LANE)))


def _dispatch_body(dest_ref, cnt_ref, seg_ref, tot_ref, hp_ref, xs_ref, zrow_ref, zblk_ref,
                   row_sem, pad_sem, *, t_total, ne, n_blocks):
    i = pl.program_id(0)
    tm = hp_ref.shape[0]

    def pad_copy(slot):
        return pltpu.make_async_copy(zrow_ref.at[pl.ds(0, 1)], xs_ref.at[pl.ds(slot, 1)], pad_sem)

    def tail_copy(b):
        return pltpu.make_async_copy(
            zblk_ref, xs_ref.at[pl.ds(pl.multiple_of(b * MOE_BLK, MOE_BLK), MOE_BLK)], pad_sem)

    @pl.when(i == 0)
    def _():
        zrow_ref[...] = jnp.zeros_like(zrow_ref)
        zblk_ref[...] = jnp.zeros_like(zblk_ref)

        def expert(e, carry):
            used = cnt_ref[e]
            end = (used + MOE_BLK - 1) // MOE_BLK * MOE_BLK

            def start(r, c):
                pad_copy(seg_ref[e] + r).start()
                return c
            lax.fori_loop(used, end, start, 0)

            def wait(r, c):
                pad_copy(seg_ref[e] + r).wait()
                return c
            lax.fori_loop(used, end, wait, 0)
            return carry
        lax.fori_loop(0, ne, expert, 0)

        def tail_start(b, c):
            tail_copy(b).start()
            return c
        lax.fori_loop(tot_ref[0], n_blocks, tail_start, 0)

        def tail_wait(b, c):
            tail_copy(b).wait()
            return c
        lax.fori_loop(tot_ref[0], n_blocks, tail_wait, 0)

    for k in range(TOP_K):
        def issue(r, c, k=k):
            slot = dest_ref[k * t_total + i * tm + r]
            pltpu.make_async_copy(hp_ref.at[pl.ds(r, 1)], xs_ref.at[pl.ds(slot, 1)],
                                  row_sem).start()
            return c
        lax.fori_loop(0, tm, issue, 0, unroll=8)
    for k in range(TOP_K):
        pltpu.make_async_copy(hp_ref, xs_ref.at[pl.ds(0, tm)], row_sem).wait()


def _moe_dispatch(dest_flat, cnt, seg_start, total_blk, hp, n_rows, *, tm):
    t, w = hp.shape
    ne = cnt.shape[0]
    return pl.pallas_call(
        functools.partial(_dispatch_body, t_total=t, ne=ne, n_blocks=n_rows // MOE_BLK),
        grid_spec=pltpu.PrefetchScalarGridSpec(
            num_scalar_prefetch=4,
            grid=(t // tm,),
            in_specs=[pl.BlockSpec((tm, w), lambda i, *_: (i, 0))],
            out_specs=pl.BlockSpec(memory_space=pl.ANY),
            scratch_shapes=[pltpu.VMEM((SUBLANE, w), U32),
                            pltpu.VMEM((MOE_BLK, w), U32),
                            pltpu.SemaphoreType.DMA,
                            pltpu.SemaphoreType.DMA],
        ),
        out_shape=jax.ShapeDtypeStruct((n_rows, w), U32),
        compiler_params=_cparams(1),
        name="moe_dispatch",
    )(dest_flat, cnt, seg_start, total_blk, hp)


def _sb_specs(rows, width_in, n_tiles):
    def x_map(s, j, e, start, nblk, tot):
        return (pl.multiple_of(start[s], MOE_BLK), 0)

    def wcol(s, j, nblk):
        return jnp.where(nblk[s] > 0, j, n_tiles - 1)

    return pl.BlockSpec((pl.Element(rows), pl.Element(width_in)), x_map), wcol


def _sb_run(start_ref, nblk_ref, tot_ref, o_ref, stage_ref, zblk_ref, out_sems, zero_sem,
            compute_block, *, n_blocks):
    s = pl.program_id(0)
    j = pl.program_id(1)
    nj = pl.num_programs(1)
    q = s * nj + j
    slot = q % 2
    tile = stage_ref.shape[2]
    col = pl.multiple_of(j * tile, LANE)
    nb = nblk_ref[s]

    def out_copy(sl, b, row0):
        r0 = pl.multiple_of(b * MOE_BLK, MOE_BLK)
        return pltpu.make_async_copy(
            stage_ref.at[sl, pl.ds(r0, MOE_BLK)],
            o_ref.at[pl.ds(pl.multiple_of(row0 + r0, MOE_BLK), MOE_BLK), pl.ds(col, tile)],
            out_sems.at[sl])

    def drain(sl, count):
        def one(b, c):
            out_copy(sl, 0, 0).wait()
            return c
        lax.fori_loop(0, count, one, 0)

    def zero_copy(b):
        return pltpu.make_async_copy(
            zblk_ref,
            o_ref.at[pl.ds(pl.multiple_of(b * MOE_BLK, MOE_BLK), MOE_BLK), pl.ds(col, tile)],
            zero_sem)

    @pl.when(q >= 2)
    def _():
        drain(slot, nblk_ref[(q - 2) // nj])

    @pl.when(s == 0)
    def _():
        zblk_ref[...] = jnp.zeros_like(zblk_ref)

        def zs(b, c):
            zero_copy(b).start()
            return c
        lax.fori_loop(tot_ref[0], n_blocks, zs, 0)

        def zw(b, c):
            zero_copy(b).wait()
            return c
        lax.fori_loop(tot_ref[0], n_blocks, zw, 0)

    def block(b, c):
        compute_block(slot, pl.multiple_of(b * MOE_BLK, MOE_BLK))
        out_copy(slot, b, start_ref[s]).start()
        return c
    lax.fori_loop(0, nb, block, 0)

    @pl.when(q == pl.num_programs(0) * nj - 1)
    def _():
        drain(slot, nb)

        @pl.when(q >= 1)
        def _():
            drain(1 - slot, nblk_ref[(q - 1) // nj])


def _gate_up_body(e_ref, start_ref, nblk_ref, tot_ref, x_ref, wg_ref, wu_ref, bg_ref, bu_ref,
                  o_ref, wbf_ref, stage_ref, zblk_ref, out_sems, zero_sem, *, n_blocks):
    half = x_ref.shape[1]
    tj = wg_ref.shape[1]

    @pl.when(nblk_ref[pl.program_id(0)] > 0)
    def _():
        wbf_ref[:, :tj] = wg_ref[...].astype(BF16)
        wbf_ref[:, tj:] = wu_ref[...].astype(BF16)

    bias = jnp.concatenate([bg_ref[...], bu_ref[...]], axis=1)

    def compute_block(slot, r0):
        xp = x_ref[pl.ds(r0, MOE_BLK), :]
        x_lo = lax.bitcast_convert_type(xp << 16, F32).astype(BF16)
        x_hi = lax.bitcast_convert_type(xp & jnp.uint32(0xFFFF0000), F32).astype(BF16)
        gu = (jnp.dot(x_lo, wbf_ref[:half, :], preferred_element_type=F32)
              + jnp.dot(x_hi, wbf_ref[half:, :], preferred_element_type=F32) + bias)
        gate = jnp.minimum(gu[:, :tj], SWIGLU_LIMIT)
        up = jnp.clip(gu[:, tj:], -SWIGLU_LIMIT, SWIGLU_LIMIT)
        act = (up + 1.0) * (gate * _sigmoid(gate * SWIGLU_ALPHA))
        stage_ref[slot, pl.ds(r0, MOE_BLK), :] = act.astype(stage_ref.dtype)

    _sb_run(start_ref, nblk_ref, tot_ref, o_ref, stage_ref, zblk_ref, out_sems, zero_sem,
            compute_block, n_blocks=n_blocks)


def _sb_scratch(rows, tile, dtype):
    return [pltpu.VMEM((2, rows, tile), dtype), pltpu.VMEM((MOE_BLK, tile), dtype),
            pltpu.SemaphoreType.DMA((2,)), pltpu.SemaphoreType.DMA]


def _moe_gate_up(sb, xs, w_gu, b_gu, *, tj):
    ne, d, de2 = w_gu.shape
    de = de2 // 2
    nj = de // tj
    rows = MOE_SB_BLKS * MOE_BLK
    n_rows, half = xs.shape
    x_spec, wcol = _sb_specs(rows, half, nj)
    in_specs = [
        x_spec,
        pl.BlockSpec((None, d, tj), lambda s, j, e, st, nb, tt: (e[s], 0, wcol(s, j, nb))),
        pl.BlockSpec((None, d, tj), lambda s, j, e, st, nb, tt: (e[s], 0, nj + wcol(s, j, nb))),
        pl.BlockSpec((None, 1, tj), lambda s, j, e, st, nb, tt: (e[s], 0, wcol(s, j, nb))),
        pl.BlockSpec((None, 1, tj), lambda s, j, e, st, nb, tt: (e[s], 0, nj + wcol(s, j, nb))),
    ]
    return pl.pallas_call(
        functools.partial(_gate_up_body, n_blocks=n_rows // MOE_BLK),
        grid_spec=pltpu.PrefetchScalarGridSpec(
            num_scalar_prefetch=4,
            grid=(sb[0].shape[0], nj),
            in_specs=in_specs,
            out_specs=pl.BlockSpec(memory_space=pl.ANY),
            scratch_shapes=[pltpu.VMEM((d, 2 * tj), BF16)] + _sb_scratch(rows, tj, BF16),
        ),
        out_shape=jax.ShapeDtypeStruct((n_rows, de), BF16),
        compiler_params=_cparams(2),
        name="moe_gate_up",
    )(*sb, xs, w_gu, w_gu, b_gu.reshape(ne, 1, de2), b_gu.reshape(ne, 1, de2))


def _down_body(e_ref, start_ref, nblk_ref, tot_ref, a_ref, w_ref, b_ref, o_ref, wbf_ref,
               stage_ref, zblk_ref, out_sems, zero_sem, *, n_blocks):
    @pl.when(nblk_ref[pl.program_id(0)] > 0)
    def _():
        wbf_ref[...] = w_ref[...].astype(BF16)

    bias = b_ref[...]

    def compute_block(slot, r0):
        stage_ref[slot, pl.ds(r0, MOE_BLK), :] = jnp.dot(
            a_ref[pl.ds(r0, MOE_BLK), :], wbf_ref[...], preferred_element_type=F32) + bias

    _sb_run(start_ref, nblk_ref, tot_ref, o_ref, stage_ref, zblk_ref, out_sems, zero_sem,
            compute_block, n_blocks=n_blocks)


def _moe_down(sb, act, w_down, b_down, *, tn):
    ne, de, d = w_down.shape
    nn = d // tn
    rows = MOE_SB_BLKS * MOE_BLK
    n_rows = act.shape[0]
    a_spec, wcol = _sb_specs(rows, de, nn)
    in_specs = [
        a_spec,
        pl.BlockSpec((None, de, tn), lambda s, j, e, st, nb, tt: (e[s], 0, wcol(s, j, nb))),
        pl.BlockSpec((None, 1, tn), lambda s, j, e, st, nb, tt: (e[s], 0, wcol(s, j, nb))),
    ]
    return pl.pallas_call(
        functools.partial(_down_body, n_blocks=n_rows // MOE_BLK),
        grid_spec=pltpu.PrefetchScalarGridSpec(
            num_scalar_prefetch=4,
            grid=(sb[0].shape[0], nn),
            in_specs=in_specs,
            out_specs=pl.BlockSpec(memory_space=pl.ANY),
            scratch_shapes=[pltpu.VMEM((de, tn), BF16)] + _sb_scratch(rows, tn, F32),
        ),
        out_shape=jax.ShapeDtypeStruct((n_rows, d), F32),
        compiler_params=_cparams(2),
        name="moe_down",
    )(*sb, act, w_down, b_down.reshape(ne, 1, d))


def _moe_schedule(cnt, n_pairs, ne):
    n_blocks = n_pairs // MOE_BLK + ne + MOE_SB_BLKS - 1
    s_max = -(-(n_pairs // MOE_BLK + ne) // MOE_SB_BLKS) + ne
    pblk = (cnt + MOE_BLK - 1) // MOE_BLK
    seg_end_blk = jnp.cumsum(pblk)
    seg_start = (seg_end_blk - pblk) * MOE_BLK
    total_blk = seg_end_blk[-1:]
    nsb_e = (pblk + MOE_SB_BLKS - 1) // MOE_SB_BLKS
    sb_end = jnp.cumsum(nsb_e)
    sb_first = sb_end - nsb_e
    n_sb = sb_end[-1]
    s_ids = jnp.arange(s_max, dtype=I32)
    used = s_ids < n_sb
    e_of_s = jnp.minimum(jnp.searchsorted(sb_end, s_ids, side="right"), ne - 1).astype(I32)
    i_in_e = s_ids - sb_first[e_of_s]
    start = seg_start[e_of_s] + i_in_e * (MOE_SB_BLKS * MOE_BLK)
    nblk = jnp.where(used, jnp.clip(pblk[e_of_s] - MOE_SB_BLKS * i_in_e, 0, MOE_SB_BLKS), 0)
    last = jnp.maximum(n_sb - 1, 0)
    sb = (jnp.where(used, e_of_s, e_of_s[last]).astype(I32),
          jnp.where(used, start, start[last]).astype(I32), nblk.astype(I32),
          total_blk.astype(I32))
    return sb, seg_start.astype(I32), n_blocks * MOE_BLK


def _combine_body(dest_ref, y_ref, h_ref, p_ref, g_ref, o_ref, ybuf, sems, *, t_total):
    i = pl.program_id(0)
    n = pl.num_programs(0)
    tc = h_ref.shape[0]

    def issue(tile, slot):
        for k in range(TOP_K):
            def one(r, carry, k=k):
                row = dest_ref[k * t_total + tile * tc + r]
                pltpu.make_async_copy(y_ref.at[pl.ds(row, 1)],
                                      ybuf.at[slot, k, pl.ds(r, 1)],
                                      sems.at[slot]).start()
                return carry
            lax.fori_loop(0, tc, one, 0)

    @pl.when(i == 0)
    def _():
        issue(0, 0)

    slot = i % 2
    for k in range(TOP_K):
        pltpu.make_async_copy(y_ref.at[pl.ds(0, tc)], ybuf.at[slot, k], sems.at[slot]).wait()

    @pl.when(i + 1 < n)
    def _():
        issue(i + 1, 1 - slot)

    acc = h_ref[...]
    p = p_ref[...]
    for k in range(TOP_K):
        acc = acc + p[:, k:k + 1] * ybuf[slot, k]
    ms = jnp.mean(acc * acc, axis=-1, keepdims=True)
    o_ref[...] = acc * lax.rsqrt(ms + EPS) * g_ref[...]


def _combine(dest_flat, y_sorted, hres, p_t, final_g, *, tc):
    t, d = hres.shape
    return pl.pallas_call(
        functools.partial(_combine_body, t_total=t),
        grid_spec=pltpu.PrefetchScalarGridSpec(
            num_scalar_prefetch=1,
            grid=(t // tc,),
            in_specs=[pl.BlockSpec(memory_space=pl.ANY),
                      pl.BlockSpec((tc, d), lambda i, *_: (i, 0)),
                      pl.BlockSpec((tc, TOP_K), lambda i, *_: (i, 0)),
                      pl.BlockSpec((1, d), lambda i, *_: (0, 0))],
            out_specs=pl.BlockSpec((tc, d), lambda i, *_: (i, 0)),
            scratch_shapes=[pltpu.VMEM((2, TOP_K, tc, d), F32),
                            pltpu.SemaphoreType.DMA((2,))],
        ),
        out_shape=jax.ShapeDtypeStruct((t, d), F32),
        compiler_params=_cparams(1),
        name="moe_combine",
    )(dest_flat, y_sorted, hres, p_t, final_g.reshape(1, d))


def _pick(n, prefs):
    for p in prefs:
        if n % p == 0:
            return p
    raise ValueError(f"no tile in {prefs} divides {n}")


def kernel(x, meta_tokens, norm_mix_g, w_in, ssd_conv_w, ssd_conv_b, dt_bias, a_log, d_skip,
           ssd_norm_g, w_ssd_out, sc_conv_w, w_sc_out, w_o, norm_ffn_g, w_router, b_router,
           w_gate_up, b_gate_up, w_down, b_down, final_g):
    batch, seq, d = x.shape
    assert norm_mix_g.shape[0] == 1, "meta-token handling assumes a single layer"
    assert seq % SSD_CHUNK == 0 and N_META <= SSD_CHUNK
    t = batch * seq
    heads = d // SSD_HEAD_DIM
    gn = SSD_GROUPS * SSD_STATE
    conv_dim = d + 2 * gn
    xbc_off = d
    dt_off = xbc_off + conv_dim
    sc_off = dt_off + heads
    gate_off = sc_off + 3 * d
    ne = w_router.shape[-1]
    de = w_down.shape[2]

    x2 = x.reshape(t, d)
    w_in_nk = jnp.swapaxes(w_in[0], 0, 1)

    tm = _pick(t, (1024, 512, 256, 128))
    tn = _pick(d, (512, 256, 128))

    h = _rmsnorm_bf16(x2, norm_mix_g[0], _pick(t, (256, 128)))
    h_meta = _rmsnorm_bf16(meta_tokens, norm_mix_g[0], N_META)

    proj = functools.partial(_matmul, h, w_in_nk, tm=tm, out_dtype=F32, w_is_nk=True)
    z = proj(0, d, tn=tn, name="proj_z")
    xbc, xbc_meta = proj(xbc_off, conv_dim, tn=tn, side=h_meta, name="proj_xbc")
    dtr, dt_meta = proj(dt_off, LANE, tn=LANE, side=h_meta, name="proj_dt")
    sc, sc_meta = proj(sc_off, 3 * d, tn=tn, side=h_meta, name="proj_sc")
    gates = proj(gate_off, 2 * d, tn=tn, name="proj_gate")

    e_mat = (jnp.arange(d, dtype=I32)[None, :] // SSD_HEAD_DIM
             == jnp.arange(heads, dtype=I32)[:, None]).astype(F32)
    pad = SSD_CHUNK - N_META
    cw = ssd_conv_w[0]
    cb = ssd_conv_b[0].reshape(1, conv_dim)
    dtb = dt_bias[0].reshape(1, heads)
    alog = a_log[0].reshape(1, heads)
    s0, tail = _ssd_meta(jnp.pad(xbc_meta, ((pad, 0), (0, 0))),
                         jnp.pad(dt_meta, ((pad, 0), (0, 0))),
                         cw, cb, dtb, alog, e_mat, heads)
    dskip_x = jnp.repeat(d_skip[0], SSD_HEAD_DIM).reshape(1, d)
    g_ssd = _ssd_main(xbc, dtr, z, s0, tail, cw, cb, dtb, alog, dskip_x,
                      ssd_norm_g[0].reshape(1, d), e_mat, heads=heads, batch=batch)

    sc_tm = _pick(seq, (512, 256, 128))
    u_sc = _short_conv(sc, sc_meta, sc_conv_w[0], width=d, tm=sc_tm, tn=tn, seq=seq)

    y_a = _matmul(g_ssd, w_ssd_out[0], 0, d, tm=tm, tn=tn, out_dtype=F32,
                  extras=[(gates, 0)],
                  epilogue=lambda acc, ga: _sigmoid(ga) * acc, name="ssd_out")
    mixed = _matmul(u_sc, w_sc_out[0], 0, d, tm=tm, tn=tn, out_dtype=BF16,
                    extras=[(gates, d), (y_a, 0)],
                    epilogue=lambda acc, gb, ya: ya + _sigmoid(gb) * acc, name="sc_out")
    hres = _matmul(mixed, w_o[0], 0, d, tm=tm, tn=tn, out_dtype=F32,
                   extras=[(x2, 0)], epilogue=lambda acc, xr: xr + acc, name="w_o")

    rt = _pick(t, (512, 256, 128))
    hp, top_idx, probs, rank, counts = _router(hres, norm_ffn_g[0], w_router[0].T,
                                               b_router[0], rt)
    cnt = counts[:, 0].astype(I32)
    sb, seg_start, n_slots = _moe_schedule(cnt, t * TOP_K, ne)
    e_ids = jnp.arange(ne, dtype=I32)[:, None, None]
    dest = rank + jnp.sum(jnp.where(top_idx[None] == e_ids, seg_start[:, None, None], 0),
                          axis=0)
    dest_flat = dest.reshape(-1)

    xs = _moe_dispatch(dest_flat, cnt, seg_start, sb[3], hp, n_slots,
                       tm=_pick(t, (512, 256, 128)))
    act = _moe_gate_up(sb, xs, w_gate_up[0], b_gate_up[0], tj=_pick(de, (256, 128)))
    y_sorted = _moe_down(sb, act, w_down[0], b_down[0],
                         tn=_pick(d, (1024, 512, 256, 128)))
    out = _combine(dest_flat, y_sorted, hres, probs.T, final_g, tc=_pick(t, (128,)))
    return out.reshape(batch, seq, d)
```

```python
import functools

import jax
import jax.numpy as jnp
from jax import lax
from jax.experimental import pallas as pl
from jax.experimental.pallas import tpu as pltpu

F32 = jnp.float32
BF16 = jnp.bfloat16
I32 = jnp.int32
U32 = jnp.uint32

N_META = 16
SSD_HEAD_DIM = 64
SSD_GROUPS = 8
SSD_STATE = 128
SSD_CONV = 4
SSD_CHUNK = 128
SC_CONV = 3
TOP_K = 4
SWIGLU_LIMIT = 7.0
SWIGLU_ALPHA = 1.702
EPS = 1e-5

LANE = 128
SUBLANE = 8
MOE_BLK = 256
MOE_SB_BLKS = 5
VMEM_MB = 56


def _cparams(n_axes, vmem_mb=VMEM_MB):
    return pltpu.CompilerParams(
        dimension_semantics=("arbitrary",) * n_axes,
        vmem_limit_bytes=vmem_mb * 1024 * 1024,
    )


def _sigmoid(v):
    return 1.0 / (1.0 + jnp.exp(-v))


def _rms_body(x_ref, g_ref, o_ref):
    x = x_ref[...]
    ms = jnp.mean(x * x, axis=-1, keepdims=True)
    o_ref[...] = (x * lax.rsqrt(ms + EPS) * g_ref[...]).astype(o_ref.dtype)


def _rmsnorm_bf16(x2, g, tm):
    m, d = x2.shape
    return pl.pallas_call(
        _rms_body,
        grid=(m // tm,),
        in_specs=[pl.BlockSpec((tm, d), lambda i: (i, 0)),
                  pl.BlockSpec((1, d), lambda i: (0, 0))],
        out_specs=pl.BlockSpec((tm, d), lambda i: (i, 0)),
        out_shape=jax.ShapeDtypeStruct((m, d), BF16),
        compiler_params=_cparams(1),
        name="rmsnorm",
    )(x2, g.reshape(1, d))


def _mm_body(*refs, n_extra, has_side, epilogue, w_is_nk):
    it = iter(refs)
    a_ref = next(it)
    w_ref = next(it)
    side_ref = next(it) if has_side else None
    extras = [next(it) for _ in range(n_extra)]
    o_ref = next(it)
    oside_ref = next(it) if has_side else None
    wbf_ref = next(it)

    @pl.when(pl.program_id(1) == 0)
    def _():
        w = w_ref[...]
        wbf_ref[...] = (w.T if w_is_nk else w).astype(BF16)
        if has_side:
            oside_ref[...] = jnp.dot(side_ref[...], wbf_ref[...],
                                     preferred_element_type=F32)

    acc = jnp.dot(a_ref[...], wbf_ref[...], preferred_element_type=F32)
    if epilogue is not None:
        acc = epilogue(acc, *[e[...] for e in extras])
    o_ref[...] = acc.astype(o_ref.dtype)


def _matmul(a, w, col_off, n_cols, *, tm, tn, out_dtype, side=None, extras=(),
            epilogue=None, w_is_nk=False, name="matmul"):
    m, k = a.shape
    assert m % tm == 0 and n_cols % tn == 0
    if w_is_nk:
        assert col_off % SUBLANE == 0
        w_spec = pl.BlockSpec((pl.Element(tn), pl.Element(k)),
                              lambda n, i: (pl.multiple_of(col_off + n * tn, SUBLANE), 0))
    else:
        assert col_off % tn == 0
        w_spec = pl.BlockSpec((k, tn), lambda n, i, cb=col_off // tn: (0, n + cb))
    in_specs = [pl.BlockSpec((tm, k), lambda n, i: (i, 0)), w_spec]
    args = [a, w]
    if side is not None:
        in_specs.append(pl.BlockSpec((side.shape[0], k), lambda n, i: (0, 0)))
        args.append(side)
    for arr, off in extras:
        assert off % tn == 0
        in_specs.append(pl.BlockSpec((tm, tn), lambda n, i, o=off // tn: (i, n + o)))
        args.append(arr)
    out_shape = [jax.ShapeDtypeStruct((m, n_cols), out_dtype)]
    out_specs = [pl.BlockSpec((tm, tn), lambda n, i: (i, n))]
    if side is not None:
        out_shape.append(jax.ShapeDtypeStruct((side.shape[0], n_cols), F32))
        out_specs.append(pl.BlockSpec((side.shape[0], tn), lambda n, i: (0, n)))
    res = pl.pallas_call(
        functools.partial(_mm_body, n_extra=len(extras), has_side=side is not None,
                          epilogue=epilogue, w_is_nk=w_is_nk),
        grid=(n_cols // tn, m // tm),
        in_specs=in_specs,
        out_specs=out_specs,
        out_shape=out_shape,
        scratch_shapes=[pltpu.VMEM((k, tn), BF16)],
        compiler_params=_cparams(2),
        name=name,
    )(*args)
    return res if side is not None else res[0]


def _causal_conv(raw, prev, w, n_tap):
    acc = raw * w[n_tap - 1:n_tap, :]
    row8 = lax.broadcasted_iota(I32, (SUBLANE, raw.shape[1]), 0)
    for s in range(1, n_tap):
        sh = pltpu.roll(raw, s, 0)
        shp = pltpu.roll(prev, s, 0)
        first = jnp.where(row8 < s, shp, sh[:SUBLANE])
        sh = jnp.concatenate([first, sh[SUBLANE:]], axis=0)
        acc = acc + sh * w[n_tap - 1 - s:n_tap - s, :]
    return acc


def _proj_conv_body(*refs, n_w, n_tap, tiles_per_batch, mode):
    it = iter(refs)
    a_ref = next(it)
    w_refs = [next(it) for _ in range(n_w)]
    side_ref = next(it)
    cw_ref = next(it)
    cb_ref = next(it) if mode == "xbc" else None
    o_ref = next(it)
    oside_ref = next(it) if mode == "xbc" else None
    wbf_ref = next(it)
    prev_ref = next(it)
    tail_ref = next(it)
    tn = o_ref.shape[1]
    cw = cw_ref[...]

    def pre(acc):
        return acc if mode == "xbc" else acc[:, tn:2 * tn] * acc[:, 2 * tn:]

    def post(y, acc):
        if mode == "xbc":
            v = y + cb_ref[...]
            return v * _sigmoid(v)
        return acc[:, :tn] * y

    m = pl.program_id(1)

    @pl.when(m == 0)
    def _():
        for i, w_ref in enumerate(w_refs):
            wbf_ref[:, i * tn:(i + 1) * tn] = w_ref[...].T.astype(BF16)
        acc_m = jnp.dot(side_ref[...], wbf_ref[...], preferred_element_type=F32)
        u_m = pre(acc_m)
        tail_ref[...] = u_m[N_META - SUBLANE:, :]
        if mode == "xbc":
            y_m = _causal_conv(u_m, jnp.zeros((SUBLANE, tn), F32), cw, n_tap)
            oside_ref[...] = post(y_m, acc_m)

    @pl.when(m % tiles_per_batch == 0)
    def _():
        prev_ref[...] = tail_ref[...]

    acc = jnp.dot(a_ref[...], wbf_ref[...], preferred_element_type=F32)
    u = pre(acc)
    y = _causal_conv(u, prev_ref[...], cw, n_tap)
    prev_ref[...] = u[u.shape[0] - SUBLANE:, :]
    o_ref[...] = post(y, acc).astype(o_ref.dtype)


def _proj_conv(a, w_nk, row_offs, side, cw, cb, *, width, tm, tn, seq, mode, name):
    m, k = a.shape
    n_w = len(row_offs)
    n_tap = cw.shape[0]
    ms = side.shape[0]

    def w_spec(off):
        return pl.BlockSpec((pl.Element(tn), pl.Element(k)),
                            lambda n, i: (pl.multiple_of(off + n * tn, SUBLANE), 0))

    in_specs = [pl.BlockSpec((tm, k), lambda n, i: (i, 0))]
    in_specs += [w_spec(off) for off in row_offs]
    in_specs += [pl.BlockSpec((ms, k), lambda n, i: (0, 0)),
                 pl.BlockSpec((n_tap, tn), lambda n, i: (0, n))]
    args = [a] + [w_nk] * n_w + [side, cw]
    out_shape = [jax.ShapeDtypeStruct((m, width), F32 if mode == "xbc" else BF16)]
    out_specs = [pl.BlockSpec((tm, tn), lambda n, i: (i, n))]
    if mode == "xbc":
        in_specs.append(pl.BlockSpec((1, tn), lambda n, i: (0, n)))
        args.append(cb)
        out_shape.append(jax.ShapeDtypeStruct((ms, width), F32))
        out_specs.append(pl.BlockSpec((ms, tn), lambda n, i: (0, n)))
    res = pl.pallas_call(
        functools.partial(_proj_conv_body, n_w=n_w, n_tap=n_tap,
                          tiles_per_batch=seq // tm, mode=mode),
        grid=(width // tn, m // tm),
        in_specs=in_specs,
        out_specs=out_specs,
        out_shape=out_shape,
        scratch_shapes=[pltpu.VMEM((k, n_w * tn), BF16),
                        pltpu.VMEM((SUBLANE, tn), F32),
                        pltpu.VMEM((SUBLANE, tn), F32)],
        compiler_params=_cparams(2, VMEM_MB + 4 * (n_w > 1)),
        name=name,
    )(*args)
    return res if mode == "xbc" else res[0]


def _ssd_chunk(xbc, dt_raw, dtb, alog, e2, *, heads, meta):
    L = xbc.shape[0]
    di = heads * SSD_HEAD_DIM
    gn = SSD_GROUPS * SSD_STATE
    dt_in = dt_raw[:, :heads] + dtb
    dtv = jnp.maximum(dt_in, 0.0) + jnp.log1p(jnp.exp(-jnp.abs(dt_in)))
    if meta:
        dtv = dtv * (lax.broadcasted_iota(I32, (L, 1), 0) >= L - N_META).astype(F32)
    xs = xbc[:, :di]
    bm = xbc[:, di:di + gn]
    cm = xbc[:, di + gn:]
    a_neg = -jnp.exp(alog)
    ad = dtv * a_neg
    r = lax.broadcasted_iota(I32, (L, L), 0)
    c = lax.broadcasted_iota(I32, (L, L), 1)
    tril = r >= c
    acum = jnp.dot(tril.astype(F32), ad, precision=lax.Precision.HIGHEST,
                   preferred_element_type=F32)
    stack = jnp.concatenate([dtv, dtv * jnp.exp(acum[L - 1:L, :] - acum), jnp.exp(acum)],
                            axis=0)
    hi = stack.astype(BF16)
    lo = (stack - hi.astype(F32)).astype(BF16)
    exp_x = jnp.dot(jnp.concatenate([hi, lo], axis=1), e2, preferred_element_type=F32)
    dt_x, dd_x, ea_x = exp_x[:L], exp_x[L:2 * L], exp_x[2 * L:]
    bt = bm.T.astype(BF16)
    return dict(xs=xs, cm=cm, bt=bt, acum=acum, ea_x=ea_x, x_dt=(xs * dt_x).astype(BF16),
                x_dec=(xs * dd_x).astype(BF16), tril=tril)


def _ssd_state_update(q, st_ref, heads):
    gw = heads * SSD_HEAD_DIM // SSD_GROUPS
    L = q["ea_x"].shape[0]
    for g in range(SSD_GROUPS):
        sl = slice(g * gw, (g + 1) * gw)
        btg = q["bt"][g * SSD_STATE:(g + 1) * SSD_STATE, :]
        new = jnp.dot(btg, q["x_dec"][:, sl], preferred_element_type=F32)
        st_ref[:, sl] = st_ref[:, sl] * q["ea_x"][L - 1:L, sl] + new


def _ssd_body(xbc_ref, dt_ref, z_ref, s0_ref, dtb_ref, alog_ref, dskip_ref, ng_ref, e_ref,
              o_ref, st_ref, *, heads):
    @pl.when(pl.program_id(1) == 0)
    def _():
        st_ref[...] = s0_ref[...]

    L = xbc_ref.shape[0]
    q = _ssd_chunk(xbc_ref[...], dt_ref[...], dtb_ref[...], alog_ref[...], e_ref[...],
                   heads=heads, meta=False)

    gw = heads * SSD_HEAD_DIM // SSD_GROUPS
    j_heads = heads // SSD_GROUPS
    acum = q["acum"]
    acum_t = acum.T
    cmb = q["cm"].astype(BF16)
    x_dt_b = q["x_dt"]
    lane = lax.broadcasted_iota(I32, (L, 2 * SSD_HEAD_DIM), 1)
    lo = lane < SSD_HEAD_DIM
    neg_inf = jnp.float32(-jnp.inf)

    def decay_mat(h):
        col = jnp.broadcast_to(acum[:, h:h + 1], (L, L))
        row = jnp.broadcast_to(acum_t[h:h + 1, :], (L, L))
        return jnp.exp(jnp.where(q["tril"], col - row, neg_inf))

    for g in range(SSD_GROUPS):
        sl = slice(g * gw, (g + 1) * gw)
        cg = cmb[:, g * SSD_STATE:(g + 1) * SSD_STATE]
        btg = q["bt"][g * SSD_STATE:(g + 1) * SSD_STATE, :]
        cb_mat = jnp.dot(cg, btg, preferred_element_type=F32)
        parts = []
        for p in range(j_heads // 2):
            h1 = g * j_heads + 2 * p
            slab = x_dt_b[:, h1 * SSD_HEAD_DIM:(h1 + 2) * SSD_HEAD_DIM]
            lhs = jnp.concatenate([(cb_mat * decay_mat(h1)).astype(BF16),
                                   (cb_mat * decay_mat(h1 + 1)).astype(BF16)], axis=1)
            zero = jnp.zeros_like(slab)
            rhs = jnp.concatenate([jnp.where(lo, slab, zero),
                                   jnp.where(lo, zero, slab)], axis=0)
            parts.append(jnp.dot(lhs, rhs, preferred_element_type=F32))
        y_diag = parts[0] if len(parts) == 1 else jnp.concatenate(parts, axis=1)
        st_g = st_ref[:, sl]
        y_off = jnp.dot(cg, st_g.astype(BF16), preferred_element_type=F32)
        y = y_diag + y_off * q["ea_x"][:, sl] + dskip_ref[:, sl] * q["xs"][:, sl]
        zg = z_ref[:, sl]
        gated = y * (zg * _sigmoid(zg))
        ms = jnp.mean(gated * gated, axis=-1, keepdims=True)
        o_ref[:, sl] = (gated * lax.rsqrt(ms + EPS) * ng_ref[:, sl]).astype(o_ref.dtype)
        new = jnp.dot(btg, q["x_dec"][:, sl], preferred_element_type=F32)
        st_ref[:, sl] = st_g * q["ea_x"][L - 1:L, sl] + new


def _ssd_meta_body(xbc_ref, dt_ref, dtb_ref, alog_ref, e_ref, st_ref, *, heads):
    st_ref[...] = jnp.zeros_like(st_ref)
    q = _ssd_chunk(xbc_ref[...], dt_ref[...], dtb_ref[...], alog_ref[...], e_ref[...],
                   heads=heads, meta=True)
    _ssd_state_update(q, st_ref, heads)


def _full_spec(shape):
    nd = len(shape)
    return pl.BlockSpec(shape, lambda *_: (0,) * nd)


def _ssd_meta(xbc_m, dt_m, dtb, alog, e2, heads):
    di = heads * SSD_HEAD_DIM
    args = [xbc_m, dt_m, dtb, alog, e2]
    return pl.pallas_call(
        functools.partial(_ssd_meta_body, heads=heads),
        grid=(1,),
        in_specs=[_full_spec(a.shape) for a in args],
        out_specs=_full_spec((SSD_STATE, di)),
        out_shape=jax.ShapeDtypeStruct((SSD_STATE, di), F32),
        compiler_params=_cparams(1),
        name="ssd_meta",
    )(*args)


def _ssd_main(xbc, dt, z, s0, dtb, alog, dskip_x, ng, e2, *, heads, batch):
    t, cd = xbc.shape
    di = heads * SSD_HEAD_DIM
    L = SSD_CHUNK
    nc = t // batch // L
    row = lambda b, c: (b * nc + c, 0)
    in_specs = [pl.BlockSpec((L, cd), row),
                pl.BlockSpec((L, LANE), row),
                pl.BlockSpec((L, di), row)]
    consts = [s0, dtb, alog, dskip_x, ng, e2]
    in_specs += [_full_spec(a.shape) for a in consts]
    return pl.pallas_call(
        functools.partial(_ssd_body, heads=heads),
        grid=(batch, nc),
        in_specs=in_specs,
        out_specs=pl.BlockSpec((L, di), row),
        out_shape=jax.ShapeDtypeStruct((t, di), BF16),
        scratch_shapes=[pltpu.VMEM((SSD_STATE, di), F32)],
        compiler_params=_cparams(2),
        name="ssd_main",
    )(xbc, dt, z, *consts)


def _router_body(x_ref, g_ref, wr_ref, br_ref, hp_ref, idx_ref, prob_ref, rank_ref,
                 cnt_ref, base_ref):
    @pl.when(pl.program_id(0) == 0)
    def _():
        base_ref[...] = jnp.zeros_like(base_ref)

    x = x_ref[...]
    tm, d = x.shape
    ne = wr_ref.shape[0]
    ms = jnp.mean(x * x, axis=-1, keepdims=True)
    h = x * lax.rsqrt(ms + EPS) * g_ref[...]
    half = d // 2
    lo_bits = lax.bitcast_convert_type(h[:, :half].astype(BF16).astype(F32), U32)
    hi_bits = lax.bitcast_convert_type(h[:, half:].astype(BF16).astype(F32), U32)
    hp_ref[...] = (lo_bits >> 16) | (hi_bits & jnp.uint32(0xFFFF0000))

    logits = lax.dot_general(wr_ref[...], h, (((1,), (1,)), ((), ())),
                             precision=lax.Precision.HIGHEST,
                             preferred_element_type=F32) + br_ref[:, 0:1]
    eidx = lax.broadcasted_iota(I32, (ne, tm), 0)
    work = logits
    vals, onehots = [], []
    for k in range(TOP_K):
        mx = jnp.max(work, axis=0, keepdims=True)
        sel = jnp.min(jnp.where(work == mx, eidx, ne), axis=0, keepdims=True)
        oh = eidx == sel
        idx_ref[k:k + 1, :] = sel
        vals.append(mx)
        onehots.append(oh)
        work = jnp.where(oh, -jnp.inf, work)
    exps = [jnp.exp(v - vals[0]) for v in vals]
    denom = exps[0]
    for e in exps[1:]:
        denom = denom + e
    for k in range(TOP_K):
        prob_ref[k:k + 1, :] = exps[k] / denom

    cnt = onehots[0].astype(F32)
    for oh in onehots[1:]:
        cnt = cnt + oh.astype(F32)
    r = lax.broadcasted_iota(I32, (tm, tm), 0)
    c = lax.broadcasted_iota(I32, (tm, tm), 1)
    before = (r < c).astype(BF16)
    prefix = jnp.dot(cnt.astype(BF16), before, preferred_element_type=F32)
    prefix = prefix + base_ref[:, 0:1]
    for k in range(TOP_K):
        rk = jnp.sum(jnp.where(onehots[k], prefix, 0.0), axis=0, keepdims=True)
        rank_ref[k:k + 1, :] = rk.astype(I32)
    base_ref[...] = base_ref[...] + jnp.sum(cnt, axis=1, keepdims=True)
    cnt_ref[...] = base_ref[...]


def _router(hres, g, wr_t, br, tm):
    t, d = hres.shape
    ne = wr_t.shape[0]
    tok = lambda i: (0, i)
    return pl.pallas_call(
        _router_body,
        grid=(t // tm,),
        in_specs=[pl.BlockSpec((tm, d), lambda i: (i, 0)),
                  _full_spec((1, d)), _full_spec((ne, d)), _full_spec((ne, LANE))],
        out_specs=[pl.BlockSpec((tm, d // 2), lambda i: (i, 0)),
                   pl.BlockSpec((TOP_K, tm), tok),
                   pl.BlockSpec((TOP_K, tm), tok),
                   pl.BlockSpec((TOP_K, tm), tok),
                   _full_spec((ne, LANE))],
        out_shape=[jax.ShapeDtypeStruct((t, d // 2), U32),
                   jax.ShapeDtypeStruct((TOP_K, t), I32),
                   jax.ShapeDtypeStruct((TOP_K, t), F32),
                   jax.ShapeDtypeStruct((TOP_K, t), I32),
                   jax.ShapeDtypeStruct((ne, LANE), F32)],
        scratch_shapes=[pltpu.VMEM((ne, LANE), F32)],
        compiler_params=_cparams(1),
        name="router",
    )(hres, g.reshape(1, d), wr_t, jnp.broadcast_to(br.reshape(ne, 1), (ne, LANE)))


def _dispatch_body(dest_ref, cnt_ref, seg_ref, tot_ref, hp_ref, xs_ref, zrow_ref, zblk_ref,
                   row_sem, pad_sem, *, t_total, ne, n_blocks):
    i = pl.program_id(0)
    tm = hp_ref.shape[0]

    def pad_copy(slot):
        return pltpu.make_async_copy(zrow_ref.at[pl.ds(0, 1)], xs_ref.at[pl.ds(slot, 1)], pad_sem)

    def tail_copy(b):
        return pltpu.make_async_copy(
            zblk_ref, xs_ref.at[pl.ds(pl.multiple_of(b * MOE_BLK, MOE_BLK), MOE_BLK)], pad_sem)

    @pl.when(i == 0)
    def _():
        zrow_ref[...] = jnp.zeros_like(zrow_ref)
        zblk_ref[...] = jnp.zeros_like(zblk_ref)

        def expert(e, carry):
            used = cnt_ref[e]
            end = (used + MOE_BLK - 1) // MOE_BLK * MOE_BLK

            def start(r, c):
                pad_copy(seg_ref[e] + r).start()
                return c
            lax.fori_loop(used, end, start, 0)

            def wait(r, c):
                pad_copy(seg_ref[e] + r).wait()
                return c
            lax.fori_loop(used, end, wait, 0)
            return carry
        lax.fori_loop(0, ne, expert, 0)

        def tail_start(b, c):
            tail_copy(b).start()
            return c
        lax.fori_loop(tot_ref[0], n_blocks, tail_start, 0)

        def tail_wait(b, c):
            tail_copy(b).wait()
            return c
        lax.fori_loop(tot_ref[0], n_blocks, tail_wait, 0)

    for k in range(TOP_K):
        def issue(r, c, k=k):
            slot = dest_ref[k * t_total + i * tm + r]
            pltpu.make_async_copy(hp_ref.at[pl.ds(r, 1)], xs_ref.at[pl.ds(slot, 1)],
                                  row_sem).start()
            return c
        lax.fori_loop(0, tm, issue, 0, unroll=8)
    for k in range(TOP_K):
        pltpu.make_async_copy(hp_ref, xs_ref.at[pl.ds(0, tm)], row_sem).wait()


def _moe_dispatch(dest_flat, cnt, seg_start, total_blk, hp, n_rows, *, tm):
    t, w = hp.shape
    ne = cnt.shape[0]
    return pl.pallas_call(
        functools.partial(_dispatch_body, t_total=t, ne=ne, n_blocks=n_rows // MOE_BLK),
        grid_spec=pltpu.PrefetchScalarGridSpec(
            num_scalar_prefetch=4,
            grid=(t // tm,),
            in_specs=[pl.BlockSpec((tm, w), lambda i, *_: (i, 0))],
            out_specs=pl.BlockSpec(memory_space=pl.ANY),
            scratch_shapes=[pltpu.VMEM((SUBLANE, w), U32),
                            pltpu.VMEM((MOE_BLK, w), U32),
                            pltpu.SemaphoreType.DMA,
                            pltpu.SemaphoreType.DMA],
        ),
        out_shape=jax.ShapeDtypeStruct((n_rows, w), U32),
        compiler_params=_cparams(1),
        name="moe_dispatch",
    )(dest_flat, cnt, seg_start, total_blk, hp)


def _sb_specs(rows, width_in, n_tiles):
    def x_map(s, j, e, start, nblk, tot):
        return (pl.multiple_of(start[s], MOE_BLK), 0)

    def wcol(s, j, nblk):
        return jnp.where(nblk[s] > 0, j, n_tiles - 1)

    return pl.BlockSpec((pl.Element(rows), pl.Element(width_in)), x_map), wcol


def _sb_run(start_ref, nblk_ref, tot_ref, o_ref, stage_ref, zblk_ref, out_sems, zero_sem,
            compute_block, *, n_blocks):
    s = pl.program_id(0)
    j = pl.program_id(1)
    nj = pl.num_programs(1)
    q = s * nj + j
    slot = q % 2
    tile = stage_ref.shape[2]
    col = pl.multiple_of(j * tile, LANE)
    nb = nblk_ref[s]

    def out_copy(sl, b, row0):
        r0 = pl.multiple_of(b * MOE_BLK, MOE_BLK)
        return pltpu.make_async_copy(
            stage_ref.at[sl, pl.ds(r0, MOE_BLK)],
            o_ref.at[pl.ds(pl.multiple_of(row0 + r0, MOE_BLK), MOE_BLK), pl.ds(col, tile)],
            out_sems.at[sl])

    def drain(sl, count):
        def one(b, c):
            out_copy(sl, 0, 0).wait()
            return c
        lax.fori_loop(0, count, one, 0)

    def zero_copy(b):
        return pltpu.make_async_copy(
            zblk_ref,
            o_ref.at[pl.ds(pl.multiple_of(b * MOE_BLK, MOE_BLK), MOE_BLK), pl.ds(col, tile)],
            zero_sem)

    @pl.when(q >= 2)
    def _():
        drain(slot, nblk_ref[(q - 2) // nj])

    @pl.when(s == 0)
    def _():
        zblk_ref[...] = jnp.zeros_like(zblk_ref)

        def zs(b, c):
            zero_copy(b).start()
            return c
        lax.fori_loop(tot_ref[0], n_blocks, zs, 0)

        def zw(b, c):
            zero_copy(b).wait()
            return c
        lax.fori_loop(tot_ref[0], n_blocks, zw, 0)

    def block(b, c):
        compute_block(slot, pl.multiple_of(b * MOE_BLK, MOE_BLK))
        out_copy(slot, b, start_ref[s]).start()
        return c
    lax.fori_loop(0, nb, block, 0)

    @pl.when(q == pl.num_programs(0) * nj - 1)
    def _():
        drain(slot, nb)

        @pl.when(q >= 1)
        def _():
            drain(1 - slot, nblk_ref[(q - 1) // nj])


def _gate_up_body(e_ref, start_ref, nblk_ref, tot_ref, x_ref, wg_ref, wu_ref, bg_ref, bu_ref,
                  o_ref, wbf_ref, stage_ref, zblk_ref, out_sems, zero_sem, *, n_blocks):
    half = x_ref.shape[1]
    tj = wg_ref.shape[1]

    @pl.when(nblk_ref[pl.program_id(0)] > 0)
    def _():
        wbf_ref[:, :tj] = wg_ref[...].astype(BF16)
        wbf_ref[:, tj:] = wu_ref[...].astype(BF16)

    bias = jnp.concatenate([bg_ref[...], bu_ref[...]], axis=1)

    def compute_block(slot, r0):
        xp = x_ref[pl.ds(r0, MOE_BLK), :]
        x_lo = lax.bitcast_convert_type(xp << 16, F32).astype(BF16)
        x_hi = lax.bitcast_convert_type(xp & jnp.uint32(0xFFFF0000), F32).astype(BF16)
        gu = (jnp.dot(x_lo, wbf_ref[:half, :], preferred_element_type=F32)
              + jnp.dot(x_hi, wbf_ref[half:, :], preferred_element_type=F32) + bias)
        gate = jnp.minimum(gu[:, :tj], SWIGLU_LIMIT)
        up = jnp.clip(gu[:, tj:], -SWIGLU_LIMIT, SWIGLU_LIMIT)
        act = (up + 1.0) * (gate * _sigmoid(gate * SWIGLU_ALPHA))
        stage_ref[slot, pl.ds(r0, MOE_BLK), :] = act.astype(stage_ref.dtype)

    _sb_run(start_ref, nblk_ref, tot_ref, o_ref, stage_ref, zblk_ref, out_sems, zero_sem,
            compute_block, n_blocks=n_blocks)


def _sb_scratch(rows, tile, dtype):
    return [pltpu.VMEM((2, rows, tile), dtype), pltpu.VMEM((MOE_BLK, tile), dtype),
            pltpu.SemaphoreType.DMA((2,)), pltpu.SemaphoreType.DMA]


def _moe_gate_up(sb, xs, w_gu, b_gu, *, tj):
    ne, d, de2 = w_gu.shape
    de = de2 // 2
    nj = de // tj
    rows = MOE_SB_BLKS * MOE_BLK
    n_rows, half = xs.shape
    x_spec, wcol = _sb_specs(rows, half, nj)
    in_specs = [
        x_spec,
        pl.BlockSpec((None, d, tj), lambda s, j, e, st, nb, tt: (e[s], 0, wcol(s, j, nb))),
        pl.BlockSpec((None, d, tj), lambda s, j, e, st, nb, tt: (e[s], 0, nj + wcol(s, j, nb))),
        pl.BlockSpec((None, 1, tj), lambda s, j, e, st, nb, tt: (e[s], 0, wcol(s, j, nb))),
        pl.BlockSpec((None, 1, tj), lambda s, j, e, st, nb, tt: (e[s], 0, nj + wcol(s, j, nb))),
    ]
    return pl.pallas_call(
        functools.partial(_gate_up_body, n_blocks=n_rows // MOE_BLK),
        grid_spec=pltpu.PrefetchScalarGridSpec(
            num_scalar_prefetch=4,
            grid=(sb[0].shape[0], nj),
            in_specs=in_specs,
            out_specs=pl.BlockSpec(memory_space=pl.ANY),
            scratch_shapes=[pltpu.VMEM((d, 2 * tj), BF16)] + _sb_scratch(rows, tj, BF16),
        ),
        out_shape=jax.ShapeDtypeStruct((n_rows, de), BF16),
        compiler_params=_cparams(2),
        name="moe_gate_up",
    )(*sb, xs, w_gu, w_gu, b_gu.reshape(ne, 1, de2), b_gu.reshape(ne, 1, de2))


def _down_body(e_ref, start_ref, nblk_ref, tot_ref, a_ref, w_ref, b_ref, o_ref, wbf_ref,
               stage_ref, zblk_ref, out_sems, zero_sem, *, n_blocks):
    @pl.when(nblk_ref[pl.program_id(0)] > 0)
    def _():
        wbf_ref[...] = w_ref[...].astype(BF16)

    bias = b_ref[...]

    def compute_block(slot, r0):
        stage_ref[slot, pl.ds(r0, MOE_BLK), :] = jnp.dot(
            a_ref[pl.ds(r0, MOE_BLK), :], wbf_ref[...], preferred_element_type=F32) + bias

    _sb_run(start_ref, nblk_ref, tot_ref, o_ref, stage_ref, zblk_ref, out_sems, zero_sem,
            compute_block, n_blocks=n_blocks)


def _moe_down(sb, act, w_down, b_down, *, tn):
    ne, de, d = w_down.shape
    nn = d // tn
    rows = MOE_SB_BLKS * MOE_BLK
    n_rows = act.shape[0]
    a_spec, wcol = _sb_specs(rows, de, nn)
    in_specs = [
        a_spec,
        pl.BlockSpec((None, de, tn), lambda s, j, e, st, nb, tt: (e[s], 0, wcol(s, j, nb))),
        pl.BlockSpec((None, 1, tn), lambda s, j, e, st, nb, tt: (e[s], 0, wcol(s, j, nb))),
    ]
    return pl.pallas_call(
        functools.partial(_down_body, n_blocks=n_rows // MOE_BLK),
        grid_spec=pltpu.PrefetchScalarGridSpec(
            num_scalar_prefetch=4,
            grid=(sb[0].shape[0], nn),
            in_specs=in_specs,
            out_specs=pl.BlockSpec(memory_space=pl.ANY),
            scratch_shapes=[pltpu.VMEM((de, tn), BF16)] + _sb_scratch(rows, tn, F32),
        ),
        out_shape=jax.ShapeDtypeStruct((n_rows, d), F32),
        compiler_params=_cparams(2),
        name="moe_down",
    )(*sb, act, w_down, b_down.reshape(ne, 1, d))


def _moe_schedule(cnt, n_pairs, ne):
    n_blocks = n_pairs // MOE_BLK + ne + MOE_SB_BLKS - 1
    s_max = -(-(n_pairs // MOE_BLK + ne) // MOE_SB_BLKS) + ne
    pblk = (cnt + MOE_BLK - 1) // MOE_BLK
    seg_end_blk = jnp.cumsum(pblk)
    seg_start = (seg_end_blk - pblk) * MOE_BLK
    total_blk = seg_end_blk[-1:]
    nsb_e = (pblk + MOE_SB_BLKS - 1) // MOE_SB_BLKS
    sb_end = jnp.cumsum(nsb_e)
    sb_first = sb_end - nsb_e
    n_sb = sb_end[-1]
    s_ids = jnp.arange(s_max, dtype=I32)
    used = s_ids < n_sb
    e_of_s = jnp.minimum(jnp.searchsorted(sb_end, s_ids, side="right"), ne - 1).astype(I32)
    i_in_e = s_ids - sb_first[e_of_s]
    start = seg_start[e_of_s] + i_in_e * (MOE_SB_BLKS * MOE_BLK)
    nblk = jnp.where(used, jnp.clip(pblk[e_of_s] - MOE_SB_BLKS * i_in_e, 0, MOE_SB_BLKS), 0)
    last = jnp.maximum(n_sb - 1, 0)
    sb = (jnp.where(used, e_of_s, e_of_s[last]).astype(I32),
          jnp.where(used, start, start[last]).astype(I32), nblk.astype(I32),
          total_blk.astype(I32))
    return sb, seg_start.astype(I32), n_blocks * MOE_BLK


def _combine_body(dest_ref, y_ref, h_ref, p_ref, g_ref, o_ref, ybuf, sems, *, t_total):
    i = pl.program_id(0)
    n = pl.num_programs(0)
    tc = h_ref.shape[0]

    def issue(tile, slot):
        for k in range(TOP_K):
            def one(r, carry, k=k):
                row = dest_ref[k * t_total + tile * tc + r]
                pltpu.make_async_copy(y_ref.at[pl.ds(row, 1)],
                                      ybuf.at[slot, k, pl.ds(r, 1)],
                                      sems.at[slot]).start()
                return carry
            lax.fori_loop(0, tc, one, 0, unroll=8)

    @pl.when(i == 0)
    def _():
        issue(0, 0)

    slot = i % 2
    for k in range(TOP_K):
        pltpu.make_async_copy(y_ref.at[pl.ds(0, tc)], ybuf.at[slot, k], sems.at[slot]).wait()

    @pl.when(i + 1 < n)
    def _():
        issue(i + 1, 1 - slot)

    acc = h_ref[...]
    p = p_ref[...]
    for k in range(TOP_K):
        acc = acc + p[:, k:k + 1] * ybuf[slot, k]
    ms = jnp.mean(acc * acc, axis=-1, keepdims=True)
    o_ref[...] = acc * lax.rsqrt(ms + EPS) * g_ref[...]


def _combine(dest_flat, y_sorted, hres, p_t, final_g, *, tc):
    t, d = hres.shape
    return pl.pallas_call(
        functools.partial(_combine_body, t_total=t),
        grid_spec=pltpu.PrefetchScalarGridSpec(
            num_scalar_prefetch=1,
            grid=(t // tc,),
            in_specs=[pl.BlockSpec(memory_space=pl.ANY),
                      pl.BlockSpec((tc, d), lambda i, *_: (i, 0)),
                      pl.BlockSpec((tc, TOP_K), lambda i, *_: (i, 0)),
                      pl.BlockSpec((1, d), lambda i, *_: (0, 0))],
            out_specs=pl.BlockSpec((tc, d), lambda i, *_: (i, 0)),
            scratch_shapes=[pltpu.VMEM((2, TOP_K, tc, d), F32),
                            pltpu.SemaphoreType.DMA((2,))],
        ),
        out_shape=jax.ShapeDtypeStruct((t, d), F32),
        compiler_params=_cparams(1),
        name="moe_combine",
    )(dest_flat, y_sorted, hres, p_t, final_g.reshape(1, d))


def _pick(n, prefs):
    for p in prefs:
        if n % p == 0:
            return p
    raise ValueError(f"no tile in {prefs} divides {n}")


def kernel(x, meta_tokens, norm_mix_g, w_in, ssd_conv_w, ssd_conv_b, dt_bias, a_log, d_skip,
           ssd_norm_g, w_ssd_out, sc_conv_w, w_sc_out, w_o, norm_ffn_g, w_router, b_router,
           w_gate_up, b_gate_up, w_down, b_down, final_g):
    batch, seq, d = x.shape
    assert norm_mix_g.shape[0] == 1, "meta-token handling assumes a single layer"
    assert seq % SSD_CHUNK == 0 and N_META <= SSD_CHUNK
    t = batch * seq
    heads = d // SSD_HEAD_DIM
    gn = SSD_GROUPS * SSD_STATE
    conv_dim = d + 2 * gn
    xbc_off = d
    dt_off = xbc_off + conv_dim
    sc_off = dt_off + heads
    gate_off = sc_off + 3 * d
    ne = w_router.shape[-1]
    de = w_down.shape[2]

    x2 = x.reshape(t, d)
    w_in_nk = jnp.swapaxes(w_in[0], 0, 1)

    tm = _pick(seq, (1024, 512, 256, 128))
    tn = _pick(d, (512, 256, 128))

    h = _rmsnorm_bf16(x2, norm_mix_g[0], _pick(t, (256, 128)))
    h_meta = _rmsnorm_bf16(meta_tokens, norm_mix_g[0], N_META)

    proj = functools.partial(_matmul, h, w_in_nk, tm=tm, out_dtype=F32, w_is_nk=True)
    z = proj(0, d, tn=tn, name="proj_z")
    xbc, xbc_meta = _proj_conv(h, w_in_nk, [xbc_off], h_meta, ssd_conv_w[0],
                               ssd_conv_b[0].reshape(1, conv_dim), width=conv_dim, tm=tm,
                               tn=tn, seq=seq, mode="xbc", name="proj_xbc")
    dtr, dt_meta = proj(dt_off, LANE, tn=LANE, side=h_meta, name="proj_dt")
    u_sc = _proj_conv(h, w_in_nk, [sc_off, sc_off + d, sc_off + 2 * d], h_meta,
                      sc_conv_w[0], None, width=d, tm=tm, tn=_pick(d, (256, 128)), seq=seq,
                      mode="sc", name="proj_sc")
    gates = proj(gate_off, 2 * d, tn=tn, name="proj_gate")

    e_mat = (jnp.arange(d, dtype=I32)[None, :] // SSD_HEAD_DIM
             == jnp.arange(heads, dtype=I32)[:, None]).astype(BF16)
    e2 = jnp.concatenate([e_mat, e_mat], axis=0)
    pad = SSD_CHUNK - N_META
    dtb = dt_bias[0].reshape(1, heads)
    alog = a_log[0].reshape(1, heads)
    s0 = _ssd_meta(jnp.pad(xbc_meta, ((pad, 0), (0, 0))),
                   jnp.pad(dt_meta, ((pad, 0), (0, 0))), dtb, alog, e2, heads)
    dskip_x = jnp.repeat(d_skip[0], SSD_HEAD_DIM).reshape(1, d)
    g_ssd = _ssd_main(xbc, dtr, z, s0, dtb, alog, dskip_x, ssd_norm_g[0].reshape(1, d), e2,
                      heads=heads, batch=batch)

    y_a = _matmul(g_ssd, w_ssd_out[0], 0, d, tm=tm, tn=tn, out_dtype=F32,
                  extras=[(gates, 0)],
                  epilogue=lambda acc, ga: _sigmoid(ga) * acc, name="ssd_out")
    mixed = _matmul(u_sc, w_sc_out[0], 0, d, tm=tm, tn=tn, out_dtype=BF16,
                    extras=[(gates, d), (y_a, 0)],
                    epilogue=lambda acc, gb, ya: ya + _sigmoid(gb) * acc, name="sc_out")
    hres = _matmul(mixed, w_o[0], 0, d, tm=tm, tn=tn, out_dtype=F32,
                   extras=[(x2, 0)], epilogue=lambda acc, xr: xr + acc, name="w_o")

    rt = _pick(t, (512, 256, 128))
    hp, top_idx, probs, rank, counts = _router(hres, norm_ffn_g[0], w_router[0].T,
                                               b_router[0], rt)
    cnt = counts[:, 0].astype(I32)
    sb, seg_start, n_slots = _moe_schedule(cnt, t * TOP_K, ne)
    e_ids = jnp.arange(ne, dtype=I32)[:, None, None]
    dest = rank + jnp.sum(jnp.where(top_idx[None] == e_ids, seg_start[:, None, None], 0),
                          axis=0)
    dest_flat = dest.reshape(-1)

    xs = _moe_dispatch(dest_flat, cnt, seg_start, sb[3], hp, n_slots,
                       tm=_pick(t, (512, 256, 128)))
    act = _moe_gate_up(sb, xs, w_gate_up[0], b_gate_up[0], tj=_pick(de, (256, 128)))
    y_sorted = _moe_down(sb, act, w_down[0], b_down[0],
                         tn=_pick(d, (1024, 512, 256, 128)))
    out = _combine(dest_flat, y_sorted, hres, probs.T, final_g, tc=_pick(t, (128,)))
    return out.reshape(batch, seq, d)
```

```python
import functools

import jax
import jax.numpy as jnp
from jax import lax
from jax.experimental import pallas as pl
from jax.experimental.pallas import tpu as pltpu

F32 = jnp.float32
BF16 = jnp.bfloat16
I32 = jnp.int32
U32 = jnp.uint32

N_META = 16
SSD_HEAD_DIM = 64
SSD_GROUPS = 8
SSD_STATE = 128
SSD_CONV = 4
SSD_CHUNK = 128
SC_CONV = 3
TOP_K = 4
SWIGLU_LIMIT = 7.0
SWIGLU_ALPHA = 1.702
EPS = 1e-5

LANE = 128
SUBLANE = 8
MOE_BLK = 256
MOE_SB_BLKS = 5
EPILOGUE_ROWS = 256
VMEM_MB = 56


def _cparams(n_axes, vmem_mb=VMEM_MB):
    return pltpu.CompilerParams(
        dimension_semantics=("arbitrary",) * n_axes,
        vmem_limit_bytes=vmem_mb * 1024 * 1024,
    )


def _sigmoid(v):
    return 1.0 / (1.0 + jnp.exp(-v))


def _rms_body(x_ref, g_ref, o_ref):
    x = x_ref[...]
    ms = jnp.mean(x * x, axis=-1, keepdims=True)
    o_ref[...] = (x * lax.rsqrt(ms + EPS) * g_ref[...]).astype(o_ref.dtype)


def _rmsnorm_bf16(x2, g, tm):
    m, d = x2.shape
    return pl.pallas_call(
        _rms_body,
        grid=(m // tm,),
        in_specs=[pl.BlockSpec((tm, d), lambda i: (i, 0)),
                  pl.BlockSpec((1, d), lambda i: (0, 0))],
        out_specs=pl.BlockSpec((tm, d), lambda i: (i, 0)),
        out_shape=jax.ShapeDtypeStruct((m, d), BF16),
        compiler_params=_cparams(1),
        name="rmsnorm",
    )(x2, g.reshape(1, d))


def _chunked_matmul(a_ref, wbf_ref, acc_ref, finish):
    tm = a_ref.shape[0]
    ch = acc_ref.shape[1]
    n = tm // ch

    def rows(c):
        return slice(c * ch, (c + 1) * ch)

    for c in range(n):
        if c > 0:
            finish(rows(c - 1), acc_ref[(c - 1) % 2])
        acc_ref[c % 2] = jnp.dot(a_ref[rows(c), :], wbf_ref[...], preferred_element_type=F32)
    finish(rows(n - 1), acc_ref[(n - 1) % 2])


def _mm_body(*refs, n_extra, has_side, epilogue, w_is_nk):
    it = iter(refs)
    a_ref = next(it)
    w_ref = next(it)
    side_ref = next(it) if has_side else None
    extras = [next(it) for _ in range(n_extra)]
    o_ref = next(it)
    oside_ref = next(it) if has_side else None
    wbf_ref = next(it)
    acc_ref = next(it) if epilogue is not None else None

    @pl.when(pl.program_id(1) == 0)
    def _():
        w = w_ref[...]
        wbf_ref[...] = (w.T if w_is_nk else w).astype(BF16)
        if has_side:
            oside_ref[...] = jnp.dot(side_ref[...], wbf_ref[...],
                                     preferred_element_type=F32)

    if epilogue is None:
        o_ref[...] = jnp.dot(a_ref[...], wbf_ref[...],
                             preferred_element_type=F32).astype(o_ref.dtype)
    else:
        def finish(rows, acc):
            o_ref[rows, :] = epilogue(acc, *[e[rows, :] for e in extras]).astype(o_ref.dtype)
        _chunked_matmul(a_ref, wbf_ref, acc_ref, finish)


def _matmul(a, w, col_off, n_cols, *, tm, tn, out_dtype, side=None, extras=(),
            epilogue=None, w_is_nk=False, name="matmul"):
    m, k = a.shape
    assert m % tm == 0 and n_cols % tn == 0
    if w_is_nk:
        assert col_off % SUBLANE == 0
        w_spec = pl.BlockSpec((pl.Element(tn), pl.Element(k)),
                              lambda n, i: (pl.multiple_of(col_off + n * tn, SUBLANE), 0))
    else:
        assert col_off % tn == 0
        w_spec = pl.BlockSpec((k, tn), lambda n, i, cb=col_off // tn: (0, n + cb))
    in_specs = [pl.BlockSpec((tm, k), lambda n, i: (i, 0)), w_spec]
    args = [a, w]
    if side is not None:
        in_specs.append(pl.BlockSpec((side.shape[0], k), lambda n, i: (0, 0)))
        args.append(side)
    for arr, off in extras:
        assert off % tn == 0
        in_specs.append(pl.BlockSpec((tm, tn), lambda n, i, o=off // tn: (i, n + o)))
        args.append(arr)
    out_shape = [jax.ShapeDtypeStruct((m, n_cols), out_dtype)]
    out_specs = [pl.BlockSpec((tm, tn), lambda n, i: (i, n))]
    if side is not None:
        out_shape.append(jax.ShapeDtypeStruct((side.shape[0], n_cols), F32))
        out_specs.append(pl.BlockSpec((side.shape[0], tn), lambda n, i: (0, n)))
    res = pl.pallas_call(
        functools.partial(_mm_body, n_extra=len(extras), has_side=side is not None,
                          epilogue=epilogue, w_is_nk=w_is_nk),
        grid=(n_cols // tn, m // tm),
        in_specs=in_specs,
        out_specs=out_specs,
        out_shape=out_shape,
        scratch_shapes=[pltpu.VMEM((k, tn), BF16)]
        + ([pltpu.VMEM((2, min(tm, EPILOGUE_ROWS), tn), F32)] if epilogue is not None else []),
        compiler_params=_cparams(2),
        name=name,
    )(*args)
    return res if side is not None else res[0]


def _causal_conv(raw, prev, w, n_tap):
    acc = raw * w[n_tap - 1:n_tap, :]
    row8 = lax.broadcasted_iota(I32, (SUBLANE, raw.shape[1]), 0)
    for s in range(1, n_tap):
        sh = pltpu.roll(raw, s, 0)
        shp = pltpu.roll(prev, s, 0)
        first = jnp.where(row8 < s, shp, sh[:SUBLANE])
        sh = jnp.concatenate([first, sh[SUBLANE:]], axis=0)
        acc = acc + sh * w[n_tap - 1 - s:n_tap - s, :]
    return acc


def _proj_conv_body(*refs, n_w, n_tap, tiles_per_batch, mode):
    it = iter(refs)
    a_ref = next(it)
    w_refs = [next(it) for _ in range(n_w)]
    side_ref = next(it)
    cw_ref = next(it)
    cb_ref = next(it) if mode == "xbc" else None
    o_ref = next(it)
    oside_ref = next(it) if mode == "xbc" else None
    wbf_ref = next(it)
    prev_ref = next(it)
    tail_ref = next(it)
    acc_ref = next(it)
    tn = o_ref.shape[1]
    cw = cw_ref[...]

    def pre(acc):
        return acc if mode == "xbc" else acc[:, tn:2 * tn] * acc[:, 2 * tn:]

    def post(y, acc):
        if mode == "xbc":
            v = y + cb_ref[...]
            return v * _sigmoid(v)
        return acc[:, :tn] * y

    m = pl.program_id(1)

    @pl.when(m == 0)
    def _():
        for i, w_ref in enumerate(w_refs):
            wbf_ref[:, i * tn:(i + 1) * tn] = w_ref[...].T.astype(BF16)
        acc_m = jnp.dot(side_ref[...], wbf_ref[...], preferred_element_type=F32)
        u_m = pre(acc_m)
        tail_ref[...] = u_m[N_META - SUBLANE:, :]
        if mode == "xbc":
            y_m = _causal_conv(u_m, jnp.zeros((SUBLANE, tn), F32), cw, n_tap)
            oside_ref[...] = post(y_m, acc_m)

    @pl.when(m % tiles_per_batch == 0)
    def _():
        prev_ref[...] = tail_ref[...]

    def finish(rows, acc):
        u = pre(acc)
        y = _causal_conv(u, prev_ref[...], cw, n_tap)
        prev_ref[...] = u[u.shape[0] - SUBLANE:, :]
        o_ref[rows, :] = post(y, acc).astype(o_ref.dtype)
    _chunked_matmul(a_ref, wbf_ref, acc_ref, finish)


def _proj_conv(a, w_nk, row_offs, side, cw, cb, *, width, tm, tn, seq, mode, name):
    m, k = a.shape
    n_w = len(row_offs)
    n_tap = cw.shape[0]
    ms = side.shape[0]

    def w_spec(off):
        return pl.BlockSpec((pl.Element(tn), pl.Element(k)),
                            lambda n, i: (pl.multiple_of(off + n * tn, SUBLANE), 0))

    in_specs = [pl.BlockSpec((tm, k), lambda n, i: (i, 0))]
    in_specs += [w_spec(off) for off in row_offs]
    in_specs += [pl.BlockSpec((ms, k), lambda n, i: (0, 0)),
                 pl.BlockSpec((n_tap, tn), lambda n, i: (0, n))]
    args = [a] + [w_nk] * n_w + [side, cw]
    out_shape = [jax.ShapeDtypeStruct((m, width), F32 if mode == "xbc" else BF16)]
    out_specs = [pl.BlockSpec((tm, tn), lambda n, i: (i, n))]
    if mode == "xbc":
        in_specs.append(pl.BlockSpec((1, tn), lambda n, i: (0, n)))
        args.append(cb)
        out_shape.append(jax.ShapeDtypeStruct((ms, width), F32))
        out_specs.append(pl.BlockSpec((ms, tn), lambda n, i: (0, n)))
    res = pl.pallas_call(
        functools.partial(_proj_conv_body, n_w=n_w, n_tap=n_tap,
                          tiles_per_batch=seq // tm, mode=mode),
        grid=(width // tn, m // tm),
        in_specs=in_specs,
        out_specs=out_specs,
        out_shape=out_shape,
        scratch_shapes=[pltpu.VMEM((k, n_w * tn), BF16),
                        pltpu.VMEM((SUBLANE, tn), F32),
                        pltpu.VMEM((SUBLANE, tn), F32),
                        pltpu.VMEM((2, min(tm, EPILOGUE_ROWS), n_w * tn), F32)],
        compiler_params=_cparams(2, VMEM_MB + 4 * (n_w > 1)),
        name=name,
    )(*args)
    return res if mode == "xbc" else res[0]


def _ssd_chunk(xbc, dt_raw, dtb, alog, e2, *, heads, meta):
    L = xbc.shape[0]
    di = heads * SSD_HEAD_DIM
    gn = SSD_GROUPS * SSD_STATE
    dt_in = dt_raw[:, :heads] + dtb
    dtv = jnp.maximum(dt_in, 0.0) + jnp.log1p(jnp.exp(-jnp.abs(dt_in)))
    if meta:
        dtv = dtv * (lax.broadcasted_iota(I32, (L, 1), 0) >= L - N_META).astype(F32)
    xs = xbc[:, :di]
    bm = xbc[:, di:di + gn]
    cm = xbc[:, di + gn:]
    a_neg = -jnp.exp(alog)
    ad = dtv * a_neg
    r = lax.broadcasted_iota(I32, (L, L), 0)
    c = lax.broadcasted_iota(I32, (L, L), 1)
    tril = r >= c
    acum = jnp.dot(tril.astype(F32), ad, precision=lax.Precision.HIGHEST,
                   preferred_element_type=F32)
    stack = jnp.concatenate([dtv, dtv * jnp.exp(acum[L - 1:L, :] - acum), jnp.exp(acum)],
                            axis=0)
    hi = stack.astype(BF16)
    lo = (stack - hi.astype(F32)).astype(BF16)
    exp_x = jnp.dot(jnp.concatenate([hi, lo], axis=1), e2, preferred_element_type=F32)
    dt_x, dd_x, ea_x = exp_x[:L], exp_x[L:2 * L], exp_x[2 * L:]
    bt = bm.T.astype(BF16)
    return dict(xs=xs, cm=cm, bt=bt, acum=acum, ea_x=ea_x, x_dt=(xs * dt_x).astype(BF16),
                x_dec=(xs * dd_x).astype(BF16), tril=tril)


def _ssd_state_update(q, st_ref, heads):
    gw = heads * SSD_HEAD_DIM // SSD_GROUPS
    L = q["ea_x"].shape[0]
    for g in range(SSD_GROUPS):
        sl = slice(g * gw, (g + 1) * gw)
        btg = q["bt"][g * SSD_STATE:(g + 1) * SSD_STATE, :]
        new = jnp.dot(btg, q["x_dec"][:, sl], preferred_element_type=F32)
        st_ref[:, sl] = st_ref[:, sl] * q["ea_x"][L - 1:L, sl] + new


def _ssd_body(xbc_ref, dt_ref, z_ref, s0_ref, dtb_ref, alog_ref, dskip_ref, ng_ref, e_ref,
              o_ref, st_ref, *, heads):
    @pl.when(pl.program_id(1) == 0)
    def _():
        st_ref[...] = s0_ref[...]

    L = xbc_ref.shape[0]
    q = _ssd_chunk(xbc_ref[...], dt_ref[...], dtb_ref[...], alog_ref[...], e_ref[...],
                   heads=heads, meta=False)

    gw = heads * SSD_HEAD_DIM // SSD_GROUPS
    j_heads = heads // SSD_GROUPS
    acum = q["acum"]
    acum_t = acum.T
    cmb = q["cm"].astype(BF16)
    x_dt_b = q["x_dt"]
    lane = lax.broadcasted_iota(I32, (L, 2 * SSD_HEAD_DIM), 1)
    lo = lane < SSD_HEAD_DIM
    neg_inf = jnp.float32(-jnp.inf)

    def decay_mat(h):
        col = jnp.broadcast_to(acum[:, h:h + 1], (L, L))
        row = jnp.broadcast_to(acum_t[h:h + 1, :], (L, L))
        return jnp.exp(jnp.where(q["tril"], col - row, neg_inf))

    for g in range(SSD_GROUPS):
        sl = slice(g * gw, (g + 1) * gw)
        cg = cmb[:, g * SSD_STATE:(g + 1) * SSD_STATE]
        btg = q["bt"][g * SSD_STATE:(g + 1) * SSD_STATE, :]
        cb_mat = jnp.dot(cg, btg, preferred_element_type=F32)
        parts = []
        for p in range(j_heads // 2):
            h1 = g * j_heads + 2 * p
            slab = x_dt_b[:, h1 * SSD_HEAD_DIM:(h1 + 2) * SSD_HEAD_DIM]
            lhs = jnp.concatenate([(cb_mat * decay_mat(h1)).astype(BF16),
                                   (cb_mat * decay_mat(h1 + 1)).astype(BF16)], axis=1)
            zero = jnp.zeros_like(slab)
            rhs = jnp.concatenate([jnp.where(lo, slab, zero),
                                   jnp.where(lo, zero, slab)], axis=0)
            parts.append(jnp.dot(lhs, rhs, preferred_element_type=F32))
        y_diag = parts[0] if len(parts) == 1 else jnp.concatenate(parts, axis=1)
        st_g = st_ref[:, sl]
        y_off = jnp.dot(cg, st_g.astype(BF16), preferred_element_type=F32)
        y = y_diag + y_off * q["ea_x"][:, sl] + dskip_ref[:, sl] * q["xs"][:, sl]
        zg = z_ref[:, sl]
        gated = y * (zg * _sigmoid(zg))
        ms = jnp.mean(gated * gated, axis=-1, keepdims=True)
        o_ref[:, sl] = (gated * lax.rsqrt(ms + EPS) * ng_ref[:, sl]).astype(o_ref.dtype)
        new = jnp.dot(btg, q["x_dec"][:, sl], preferred_element_type=F32)
        st_ref[:, sl] = st_g * q["ea_x"][L - 1:L, sl] + new


def _ssd_meta_body(xbc_ref, dt_ref, dtb_ref, alog_ref, e_ref, st_ref, *, heads):
    st_ref[...] = jnp.zeros_like(st_ref)
    q = _ssd_chunk(xbc_ref[...], dt_ref[...], dtb_ref[...], alog_ref[...], e_ref[...],
                   heads=heads, meta=True)
    _ssd_state_update(q, st_ref, heads)


def _full_spec(shape):
    nd = len(shape)
    return pl.BlockSpec(shape, lambda *_: (0,) * nd)


def _ssd_meta(xbc_m, dt_m, dtb, alog, e2, heads):
    di = heads * SSD_HEAD_DIM
    args = [xbc_m, dt_m, dtb, alog, e2]
    return pl.pallas_call(
        functools.partial(_ssd_meta_body, heads=heads),
        grid=(1,),
        in_specs=[_full_spec(a.shape) for a in args],
        out_specs=_full_spec((SSD_STATE, di)),
        out_shape=jax.ShapeDtypeStruct((SSD_STATE, di), F32),
        compiler_params=_cparams(1),
        name="ssd_meta",
    )(*args)


def _ssd_main(xbc, dt, z, s0, dtb, alog, dskip_x, ng, e2, *, heads, batch):
    t, cd = xbc.shape
    di = heads * SSD_HEAD_DIM
    L = SSD_CHUNK
    nc = t // batch // L
    row = lambda b, c: (b * nc + c, 0)
    in_specs = [pl.BlockSpec((L, cd), row),
                pl.BlockSpec((L, LANE), row),
                pl.BlockSpec((L, di), row)]
    consts = [s0, dtb, alog, dskip_x, ng, e2]
    in_specs += [_full_spec(a.shape) for a in consts]
    return pl.pallas_call(
        functools.partial(_ssd_body, heads=heads),
        grid=(batch, nc),
        in_specs=in_specs,
        out_specs=pl.BlockSpec((L, di), row),
        out_shape=jax.ShapeDtypeStruct((t, di), BF16),
        scratch_shapes=[pltpu.VMEM((SSD_STATE, di), F32)],
        compiler_params=_cparams(2),
        name="ssd_main",
    )(xbc, dt, z, *consts)


def _router_body(x_ref, g_ref, wr_ref, br_ref, hp_ref, idx_ref, prob_ref, rank_ref,
                 cnt_ref, base_ref):
    @pl.when(pl.program_id(0) == 0)
    def _():
        base_ref[...] = jnp.zeros_like(base_ref)

    x = x_ref[...]
    tm, d = x.shape
    ne = wr_ref.shape[0]
    ms = jnp.mean(x * x, axis=-1, keepdims=True)
    h = x * lax.rsqrt(ms + EPS) * g_ref[...]
    half = d // 2
    lo_bits = lax.bitcast_convert_type(h[:, :half].astype(BF16).astype(F32), U32)
    hi_bits = lax.bitcast_convert_type(h[:, half:].astype(BF16).astype(F32), U32)
    hp_ref[...] = (lo_bits >> 16) | (hi_bits & jnp.uint32(0xFFFF0000))

    logits = lax.dot_general(wr_ref[...], h, (((1,), (1,)), ((), ())),
                             precision=lax.Precision.HIGHEST,
                             preferred_element_type=F32) + br_ref[:, 0:1]
    eidx = lax.broadcasted_iota(I32, (ne, tm), 0)
    work = logits
    vals, onehots = [], []
    for k in range(TOP_K):
        mx = jnp.max(work, axis=0, keepdims=True)
        sel = jnp.min(jnp.where(work == mx, eidx, ne), axis=0, keepdims=True)
        oh = eidx == sel
        idx_ref[k:k + 1, :] = sel
        vals.append(mx)
        onehots.append(oh)
        work = jnp.where(oh, -jnp.inf, work)
    exps = [jnp.exp(v - vals[0]) for v in vals]
    denom = exps[0]
    for e in exps[1:]:
        denom = denom + e
    for k in range(TOP_K):
        prob_ref[k:k + 1, :] = exps[k] / denom

    cnt = onehots[0].astype(F32)
    for oh in onehots[1:]:
        cnt = cnt + oh.astype(F32)
    r = lax.broadcasted_iota(I32, (tm, tm), 0)
    c = lax.broadcasted_iota(I32, (tm, tm), 1)
    before = (r < c).astype(BF16)
    prefix = jnp.dot(cnt.astype(BF16), before, preferred_element_type=F32)
    prefix = prefix + base_ref[:, 0:1]
    for k in range(TOP_K):
        rk = jnp.sum(jnp.where(onehots[k], prefix, 0.0), axis=0, keepdims=True)
        rank_ref[k:k + 1, :] = rk.astype(I32)
    base_ref[...] = base_ref[...] + jnp.sum(cnt, axis=1, keepdims=True)
    cnt_ref[...] = base_ref[...]


def _router(hres, g, wr_t, br, tm):
    t, d = hres.shape
    ne = wr_t.shape[0]
    tok = lambda i: (0, i)
    return pl.pallas_call(
        _router_body,
        grid=(t // tm,),
        in_specs=[pl.BlockSpec((tm, d), lambda i: (i, 0)),
                  _full_spec((1, d)), _full_spec((ne, d)), _full_spec((ne, LANE))],
        out_specs=[pl.BlockSpec((tm, d // 2), lambda i: (i, 0)),
                   pl.BlockSpec((TOP_K, tm), tok),
                   pl.BlockSpec((TOP_K, tm), tok),
                   pl.BlockSpec((TOP_K, tm), tok),
                   _full_spec((ne, LANE))],
        out_shape=[jax.ShapeDtypeStruct((t, d // 2), U32),
                   jax.ShapeDtypeStruct((TOP_K, t), I32),
                   jax.ShapeDtypeStruct((TOP_K, t), F32),
                   jax.ShapeDtypeStruct((TOP_K, t), I32),
                   jax.ShapeDtypeStruct((ne, LANE), F32)],
        scratch_shapes=[pltpu.VMEM((ne, LANE), F32)],
        compiler_params=_cparams(1),
        name="router",
    )(hres, g.reshape(1, d), wr_t, jnp.broadcast_to(br.reshape(ne, 1), (ne, LANE)))


def _dispatch_body(dest_ref, cnt_ref, seg_ref, tot_ref, hp_ref, xs_ref, zrow_ref, zblk_ref,
                   row_sem, pad_sem, *, t_total, ne, n_blocks):
    i = pl.program_id(0)
    tm = hp_ref.shape[0]

    def pad_copy(slot):
        return pltpu.make_async_copy(zrow_ref.at[pl.ds(0, 1)], xs_ref.at[pl.ds(slot, 1)], pad_sem)

    def tail_copy(b):
        return pltpu.make_async_copy(
            zblk_ref, xs_ref.at[pl.ds(pl.multiple_of(b * MOE_BLK, MOE_BLK), MOE_BLK)], pad_sem)

    @pl.when(i == 0)
    def _():
        zrow_ref[...] = jnp.zeros_like(zrow_ref)
        zblk_ref[...] = jnp.zeros_like(zblk_ref)

        def expert(e, carry):
            used = cnt_ref[e]
            end = (used + MOE_BLK - 1) // MOE_BLK * MOE_BLK

            def start(r, c):
                pad_copy(seg_ref[e] + r).start()
                return c
            lax.fori_loop(used, end, start, 0)

            def wait(r, c):
                pad_copy(seg_ref[e] + r).wait()
                return c
            lax.fori_loop(used, end, wait, 0)
            return carry
        lax.fori_loop(0, ne, expert, 0)

        def tail_start(b, c):
            tail_copy(b).start()
            return c
        lax.fori_loop(tot_ref[0], n_blocks, tail_start, 0)

        def tail_wait(b, c):
            tail_copy(b).wait()
            return c
        lax.fori_loop(tot_ref[0], n_blocks, tail_wait, 0)

    for k in range(TOP_K):
        def issue(r, c, k=k):
            slot = dest_ref[k * t_total + i * tm + r]
            pltpu.make_async_copy(hp_ref.at[pl.ds(r, 1)], xs_ref.at[pl.ds(slot, 1)],
                                  row_sem).start()
            return c
        lax.fori_loop(0, tm, issue, 0, unroll=8)
    for k in range(TOP_K):
        pltpu.make_async_copy(hp_ref, xs_ref.at[pl.ds(0, tm)], row_sem).wait()


def _moe_dispatch(dest_flat, cnt, seg_start, total_blk, hp, n_rows, *, tm):
    t, w = hp.shape
    ne = cnt.shape[0]
    return pl.pallas_call(
        functools.partial(_dispatch_body, t_total=t, ne=ne, n_blocks=n_rows // MOE_BLK),
        grid_spec=pltpu.PrefetchScalarGridSpec(
            num_scalar_prefetch=4,
            grid=(t // tm,),
            in_specs=[pl.BlockSpec((tm, w), lambda i, *_: (i, 0))],
            out_specs=pl.BlockSpec(memory_space=pl.ANY),
            scratch_shapes=[pltpu.VMEM((SUBLANE, w), U32),
                            pltpu.VMEM((MOE_BLK, w), U32),
                            pltpu.SemaphoreType.DMA,
                            pltpu.SemaphoreType.DMA],
        ),
        out_shape=jax.ShapeDtypeStruct((n_rows, w), U32),
        compiler_params=_cparams(1),
        name="moe_dispatch",
    )(dest_flat, cnt, seg_start, total_blk, hp)


def _sb_specs(rows, width_in, n_tiles):
    def x_map(s, j, e, start, nblk, tot):
        return (pl.multiple_of(start[s], MOE_BLK), 0)

    def wcol(s, j, nblk):
        return jnp.where(nblk[s] > 0, j, n_tiles - 1)

    return pl.BlockSpec((pl.Element(rows), pl.Element(width_in)), x_map), wcol


def _sb_run(start_ref, nblk_ref, tot_ref, o_ref, stage_ref, zblk_ref, out_sems, zero_sem,
            compute_rows, *, n_blocks):
    s = pl.program_id(0)
    j = pl.program_id(1)
    nj = pl.num_programs(1)
    q = s * nj + j
    slot = q % 2
    tile = stage_ref.shape[2]
    col = pl.multiple_of(j * tile, LANE)
    nb = nblk_ref[s]

    def out_copy(sl, b, row0):
        r0 = pl.multiple_of(b * MOE_BLK, MOE_BLK)
        return pltpu.make_async_copy(
            stage_ref.at[sl, pl.ds(r0, MOE_BLK)],
            o_ref.at[pl.ds(pl.multiple_of(row0 + r0, MOE_BLK), MOE_BLK), pl.ds(col, tile)],
            out_sems.at[sl])

    def drain(sl, count):
        def one(b, c):
            out_copy(sl, 0, 0).wait()
            return c
        lax.fori_loop(0, count, one, 0)

    def zero_copy(b):
        return pltpu.make_async_copy(
            zblk_ref,
            o_ref.at[pl.ds(pl.multiple_of(b * MOE_BLK, MOE_BLK), MOE_BLK), pl.ds(col, tile)],
            zero_sem)

    @pl.when(q >= 2)
    def _():
        drain(slot, nblk_ref[(q - 2) // nj])

    @pl.when(s == 0)
    def _():
        zblk_ref[...] = jnp.zeros_like(zblk_ref)

        def zs(b, c):
            zero_copy(b).start()
            return c
        lax.fori_loop(tot_ref[0], n_blocks, zs, 0)

        def zw(b, c):
            zero_copy(b).wait()
            return c
        lax.fori_loop(tot_ref[0], n_blocks, zw, 0)

    def run(b0, k):
        compute_rows(slot, pl.multiple_of(b0 * MOE_BLK, MOE_BLK), k * MOE_BLK)
        for i in range(k):
            out_copy(slot, b0 + i, start_ref[s]).start()

    def quad(p, c):
        run(4 * p, 4)
        return c
    lax.fori_loop(0, nb // 4, quad, 0)
    done4 = nb // 4 * 4

    @pl.when(nb - done4 >= 2)
    def _():
        run(done4, 2)

    @pl.when(nb % 2 == 1)
    def _():
        run(nb - 1, 1)

    @pl.when(q == pl.num_programs(0) * nj - 1)
    def _():
        drain(slot, nb)

        @pl.when(q >= 1)
        def _():
            drain(1 - slot, nblk_ref[(q - 1) // nj])


def _gate_up_body(e_ref, start_ref, nblk_ref, tot_ref, x_ref, wg_ref, wu_ref, bg_ref, bu_ref,
                  o_ref, wbf_ref, stage_ref, zblk_ref, out_sems, zero_sem, *, n_blocks):
    half = x_ref.shape[1]
    tj = wg_ref.shape[1]

    @pl.when(nblk_ref[pl.program_id(0)] > 0)
    def _():
        wbf_ref[:, :tj] = wg_ref[...].astype(BF16)
        wbf_ref[:, tj:] = wu_ref[...].astype(BF16)

    bias = jnp.concatenate([bg_ref[...], bu_ref[...]], axis=1)

    def compute_rows(slot, r0, n_rows):
        xp = x_ref[pl.ds(r0, n_rows), :]
        x_lo = lax.bitcast_convert_type(xp << 16, F32).astype(BF16)
        x_hi = lax.bitcast_convert_type(xp & jnp.uint32(0xFFFF0000), F32).astype(BF16)
        gu = (jnp.dot(x_lo, wbf_ref[:half, :], preferred_element_type=F32)
              + jnp.dot(x_hi, wbf_ref[half:, :], preferred_element_type=F32) + bias)
        gate = jnp.minimum(gu[:, :tj], SWIGLU_LIMIT)
        up = jnp.clip(gu[:, tj:], -SWIGLU_LIMIT, SWIGLU_LIMIT)
        act = (up + 1.0) * (gate * _sigmoid(gate * SWIGLU_ALPHA))
        stage_ref[slot, pl.ds(r0, n_rows), :] = act.astype(stage_ref.dtype)

    _sb_run(start_ref, nblk_ref, tot_ref, o_ref, stage_ref, zblk_ref, out_sems, zero_sem,
            compute_rows, n_blocks=n_blocks)


def _sb_scratch(rows, tile, dtype):
    return [pltpu.VMEM((2, rows, tile), dtype), pltpu.VMEM((MOE_BLK, tile), dtype),
            pltpu.SemaphoreType.DMA((2,)), pltpu.SemaphoreType.DMA]


def _moe_gate_up(sb, n_sb, xs, w_gu, b_gu, *, tj):
    ne, d, de2 = w_gu.shape
    de = de2 // 2
    nj = de // tj
    rows = MOE_SB_BLKS * MOE_BLK
    n_rows, half = xs.shape
    x_spec, wcol = _sb_specs(rows, half, nj)
    in_specs = [
        x_spec,
        pl.BlockSpec((None, d, tj), lambda s, j, e, st, nb, tt: (e[s], 0, wcol(s, j, nb))),
        pl.BlockSpec((None, d, tj), lambda s, j, e, st, nb, tt: (e[s], 0, nj + wcol(s, j, nb))),
        pl.BlockSpec((None, 1, tj), lambda s, j, e, st, nb, tt: (e[s], 0, wcol(s, j, nb))),
        pl.BlockSpec((None, 1, tj), lambda s, j, e, st, nb, tt: (e[s], 0, nj + wcol(s, j, nb))),
    ]
    return pl.pallas_call(
        functools.partial(_gate_up_body, n_blocks=n_rows // MOE_BLK),
        grid_spec=pltpu.PrefetchScalarGridSpec(
            num_scalar_prefetch=4,
            grid=(n_sb, nj),
            in_specs=in_specs,
            out_specs=pl.BlockSpec(memory_space=pl.ANY),
            scratch_shapes=[pltpu.VMEM((d, 2 * tj), BF16)] + _sb_scratch(rows, tj, BF16),
        ),
        out_shape=jax.ShapeDtypeStruct((n_rows, de), BF16),
        compiler_params=_cparams(2),
        name="moe_gate_up",
    )(*sb, xs, w_gu, w_gu, b_gu.reshape(ne, 1, de2), b_gu.reshape(ne, 1, de2))


def _down_body(e_ref, start_ref, nblk_ref, tot_ref, a_ref, w_ref, b_ref, o_ref, wbf_ref,
               stage_ref, zblk_ref, out_sems, zero_sem, *, n_blocks):
    @pl.when(nblk_ref[pl.program_id(0)] > 0)
    def _():
        wbf_ref[...] = w_ref[...].astype(BF16)

    bias = b_ref[...]

    def compute_rows(slot, r0, n_rows):
        stage_ref[slot, pl.ds(r0, n_rows), :] = jnp.dot(
            a_ref[pl.ds(r0, n_rows), :], wbf_ref[...], preferred_element_type=F32) + bias

    _sb_run(start_ref, nblk_ref, tot_ref, o_ref, stage_ref, zblk_ref, out_sems, zero_sem,
            compute_rows, n_blocks=n_blocks)


def _moe_down(sb, n_sb, act, w_down, b_down, *, tn):
    ne, de, d = w_down.shape
    nn = d // tn
    rows = MOE_SB_BLKS * MOE_BLK
    n_rows = act.shape[0]
    a_spec, wcol = _sb_specs(rows, de, nn)
    in_specs = [
        a_spec,
        pl.BlockSpec((None, de, tn), lambda s, j, e, st, nb, tt: (e[s], 0, wcol(s, j, nb))),
        pl.BlockSpec((None, 1, tn), lambda s, j, e, st, nb, tt: (e[s], 0, wcol(s, j, nb))),
    ]
    return pl.pallas_call(
        functools.partial(_down_body, n_blocks=n_rows // MOE_BLK),
        grid_spec=pltpu.PrefetchScalarGridSpec(
            num_scalar_prefetch=4,
            grid=(n_sb, nn),
            in_specs=in_specs,
            out_specs=pl.BlockSpec(memory_space=pl.ANY),
            scratch_shapes=[pltpu.VMEM((de, tn), BF16)] + _sb_scratch(rows, tn, F32),
        ),
        out_shape=jax.ShapeDtypeStruct((n_rows, d), F32),
        compiler_params=_cparams(2),
        name="moe_down",
    )(*sb, act, w_down, b_down.reshape(ne, 1, d))


def _moe_schedule(cnt, n_pairs, ne):
    n_blocks = n_pairs // MOE_BLK + ne + MOE_SB_BLKS - 1
    s_max = -(-(n_pairs // MOE_BLK + ne) // MOE_SB_BLKS) + ne
    pblk = (cnt + MOE_BLK - 1) // MOE_BLK
    seg_end_blk = jnp.cumsum(pblk)
    seg_start = (seg_end_blk - pblk) * MOE_BLK
    total_blk = seg_end_blk[-1:]
    nsb_e = (pblk + MOE_SB_BLKS - 1) // MOE_SB_BLKS
    sb_end = jnp.cumsum(nsb_e)
    sb_first = sb_end - nsb_e
    n_sb = sb_end[-1]
    s_ids = jnp.arange(s_max, dtype=I32)
    used = s_ids < n_sb
    e_of_s = jnp.minimum(jnp.searchsorted(sb_end, s_ids, side="right"), ne - 1).astype(I32)
    i_in_e = s_ids - sb_first[e_of_s]
    start = seg_start[e_of_s] + i_in_e * (MOE_SB_BLKS * MOE_BLK)
    nblk = jnp.where(used, jnp.clip(pblk[e_of_s] - MOE_SB_BLKS * i_in_e, 0, MOE_SB_BLKS), 0)
    last = jnp.maximum(n_sb - 1, 0)
    sb = (jnp.where(used, e_of_s, e_of_s[last]).astype(I32),
          jnp.where(used, start, start[last]).astype(I32), nblk.astype(I32),
          total_blk.astype(I32))
    return sb, n_sb.astype(I32), seg_start.astype(I32), n_blocks * MOE_BLK


def _combine_body(dest_ref, y_ref, h_ref, p_ref, g_ref, o_ref, ybuf, sems, *, t_total):
    i = pl.program_id(0)
    n = pl.num_programs(0)
    tc = h_ref.shape[0]

    def issue(tile, slot):
        for k in range(TOP_K):
            def one(r, carry, k=k):
                row = dest_ref[k * t_total + tile * tc + r]
                pltpu.make_async_copy(y_ref.at[pl.ds(row, 1)],
                                      ybuf.at[slot, k, pl.ds(r, 1)],
                                      sems.at[slot]).start()
                return carry
            lax.fori_loop(0, tc, one, 0, unroll=8)

    @pl.when(i == 0)
    def _():
        issue(0, 0)

    slot = i % 2
    for k in range(TOP_K):
        pltpu.make_async_copy(y_ref.at[pl.ds(0, tc)], ybuf.at[slot, k], sems.at[slot]).wait()

    @pl.when(i + 1 < n)
    def _():
        issue(i + 1, 1 - slot)

    acc = h_ref[...]
    p = p_ref[...]
    for k in range(TOP_K):
        acc = acc + p[:, k:k + 1] * ybuf[slot, k]
    ms = jnp.mean(acc * acc, axis=-1, keepdims=True)
    o_ref[...] = acc * lax.rsqrt(ms + EPS) * g_ref[...]


def _combine(dest_flat, y_sorted, hres, p_t, final_g, *, tc):
    t, d = hres.shape
    return pl.pallas_call(
        functools.partial(_combine_body, t_total=t),
        grid_spec=pltpu.PrefetchScalarGridSpec(
            num_scalar_prefetch=1,
            grid=(t // tc,),
            in_specs=[pl.BlockSpec(memory_space=pl.ANY),
                      pl.BlockSpec((tc, d), lambda i, *_: (i, 0)),
                      pl.BlockSpec((tc, TOP_K), lambda i, *_: (i, 0)),
                      pl.BlockSpec((1, d), lambda i, *_: (0, 0))],
            out_specs=pl.BlockSpec((tc, d), lambda i, *_: (i, 0)),
            scratch_shapes=[pltpu.VMEM((2, TOP_K, tc, d), F32),
                            pltpu.SemaphoreType.DMA((2,))],
        ),
        out_shape=jax.ShapeDtypeStruct((t, d), F32),
        compiler_params=_cparams(1),
        name="moe_combine",
    )(dest_flat, y_sorted, hres, p_t, final_g.reshape(1, d))


def _pick(n, prefs):
    for p in prefs:
        if n % p == 0:
            return p
    raise ValueError(f"no tile in {prefs} divides {n}")


def kernel(x, meta_tokens, norm_mix_g, w_in, ssd_conv_w, ssd_conv_b, dt_bias, a_log, d_skip,
           ssd_norm_g, w_ssd_out, sc_conv_w, w_sc_out, w_o, norm_ffn_g, w_router, b_router,
           w_gate_up, b_gate_up, w_down, b_down, final_g):
    batch, seq, d = x.shape
    assert norm_mix_g.shape[0] == 1, "meta-token handling assumes a single layer"
    assert seq % SSD_CHUNK == 0 and N_META <= SSD_CHUNK
    t = batch * seq
    heads = d // SSD_HEAD_DIM
    gn = SSD_GROUPS * SSD_STATE
    conv_dim = d + 2 * gn
    xbc_off = d
    dt_off = xbc_off + conv_dim
    sc_off = dt_off + heads
    gate_off = sc_off + 3 * d
    ne = w_router.shape[-1]
    de = w_down.shape[2]

    x2 = x.reshape(t, d)
    w_in_nk = jnp.swapaxes(w_in[0], 0, 1)

    tm = _pick(seq, (1024, 512, 256, 128))
    tn = _pick(d, (512, 256, 128))

    h = _rmsnorm_bf16(x2, norm_mix_g[0], _pick(t, (256, 128)))
    h_meta = _rmsnorm_bf16(meta_tokens, norm_mix_g[0], N_META)

    proj = functools.partial(_matmul, h, w_in_nk, tm=tm, out_dtype=F32, w_is_nk=True)
    z = proj(0, d, tn=tn, name="proj_z")
    xbc, xbc_meta = _proj_conv(h, w_in_nk, [xbc_off], h_meta, ssd_conv_w[0],
                               ssd_conv_b[0].reshape(1, conv_dim), width=conv_dim, tm=tm,
                               tn=tn, seq=seq, mode="xbc", name="proj_xbc")
    dtr, dt_meta = proj(dt_off, LANE, tn=LANE, side=h_meta, name="proj_dt")
    u_sc = _proj_conv(h, w_in_nk, [sc_off, sc_off + d, sc_off + 2 * d], h_meta,
                      sc_conv_w[0], None, width=d, tm=tm, tn=_pick(d, (256, 128)), seq=seq,
                      mode="sc", name="proj_sc")
    gates = proj(gate_off, 2 * d, tn=tn, name="proj_gate")

    e_mat = (jnp.arange(d, dtype=I32)[None, :] // SSD_HEAD_DIM
             == jnp.arange(heads, dtype=I32)[:, None]).astype(BF16)
    e2 = jnp.concatenate([e_mat, e_mat], axis=0)
    pad = SSD_CHUNK - N_META
    dtb = dt_bias[0].reshape(1, heads)
    alog = a_log[0].reshape(1, heads)
    s0 = _ssd_meta(jnp.pad(xbc_meta, ((pad, 0), (0, 0))),
                   jnp.pad(dt_meta, ((pad, 0), (0, 0))), dtb, alog, e2, heads)
    dskip_x = jnp.repeat(d_skip[0], SSD_HEAD_DIM).reshape(1, d)
    g_ssd = _ssd_main(xbc, dtr, z, s0, dtb, alog, dskip_x, ssd_norm_g[0].reshape(1, d), e2,
                      heads=heads, batch=batch)

    y_a = _matmul(g_ssd, w_ssd_out[0], 0, d, tm=tm, tn=tn, out_dtype=F32,
                  extras=[(gates, 0)],
                  epilogue=lambda acc, ga: _sigmoid(ga) * acc, name="ssd_out")
    mixed = _matmul(u_sc, w_sc_out[0], 0, d, tm=tm, tn=tn, out_dtype=BF16,
                    extras=[(gates, d), (y_a, 0)],
                    epilogue=lambda acc, gb, ya: ya + _sigmoid(gb) * acc, name="sc_out")
    hres = _matmul(mixed, w_o[0], 0, d, tm=tm, tn=tn, out_dtype=F32,
                   extras=[(x2, 0)], epilogue=lambda acc, xr: xr + acc, name="w_o")

    rt = _pick(t, (512, 256, 128))
    hp, top_idx, probs, rank, counts = _router(hres, norm_ffn_g[0], w_router[0].T,
                                               b_router[0], rt)
    cnt = counts[:, 0].astype(I32)
    sb, n_sb, seg_start, n_slots = _moe_schedule(cnt, t * TOP_K, ne)
    e_ids = jnp.arange(ne, dtype=I32)[:, None, None]
    dest = rank + jnp.sum(jnp.where(top_idx[None] == e_ids, seg_start[:, None, None], 0),
                          axis=0)
    dest_flat = dest.reshape(-1)

    xs = _moe_dispatch(dest_flat, cnt, seg_start, sb[3], hp, n_slots,
                       tm=_pick(t, (512, 256, 128)))
    act = _moe_gate_up(sb, n_sb, xs, w_gate_up[0], b_gate_up[0], tj=_pick(de, (256, 128)))
    y_sorted = _moe_down(sb, n_sb, act, w_down[0], b_down[0],
                         tn=_pick(d, (1024, 512, 256, 128)))
    out = _combine(dest_flat, y_sorted, hres, probs.T, final_g, tc=_pick(t, (128,)))
    return out.reshape(batch, seq, d)
```

```python
import functools

import jax
import jax.numpy as jnp
from jax import lax
from jax.experimental import pallas as pl
from jax.experimental.pallas import tpu as pltpu

F32 = jnp.float32
BF16 = jnp.bfloat16
I32 = jnp.int32
U32 = jnp.uint32

N_META = 16
SSD_HEAD_DIM = 64
SSD_GROUPS = 8
SSD_STATE = 128
SSD_CONV = 4
SSD_CHUNK = 128
SC_CONV = 3
TOP_K = 4
SWIGLU_LIMIT = 7.0
SWIGLU_ALPHA = 1.702
EPS = 1e-5

LANE = 128
SUBLANE = 8
MOE_BLK = 128
MOE_SB_BLKS = 10
EPILOGUE_ROWS = 256
VMEM_MB = 56


def _cparams(n_axes, vmem_mb=VMEM_MB):
    return pltpu.CompilerParams(
        dimension_semantics=("arbitrary",) * n_axes,
        vmem_limit_bytes=vmem_mb * 1024 * 1024,
    )


def _sigmoid(v):
    return 1.0 / (1.0 + jnp.exp(-v))


def _rms_body(x_ref, g_ref, o_ref):
    x = x_ref[...]
    ms = jnp.mean(x * x, axis=-1, keepdims=True)
    o_ref[...] = (x * lax.rsqrt(ms + EPS) * g_ref[...]).astype(o_ref.dtype)


def _rmsnorm_bf16(x2, g, tm):
    m, d = x2.shape
    return pl.pallas_call(
        _rms_body,
        grid=(m // tm,),
        in_specs=[pl.BlockSpec((tm, d), lambda i: (i, 0)),
                  pl.BlockSpec((1, d), lambda i: (0, 0))],
        out_specs=pl.BlockSpec((tm, d), lambda i: (i, 0)),
        out_shape=jax.ShapeDtypeStruct((m, d), BF16),
        compiler_params=_cparams(1),
        name="rmsnorm",
    )(x2, g.reshape(1, d))


def _chunked_matmul(a_ref, wbf_ref, acc_ref, finish):
    tm = a_ref.shape[0]
    ch = acc_ref.shape[1]
    n = tm // ch

    def rows(c):
        return slice(c * ch, (c + 1) * ch)

    for c in range(n):
        if c > 0:
            finish(rows(c - 1), acc_ref[(c - 1) % 2])
        acc_ref[c % 2] = jnp.dot(a_ref[rows(c), :], wbf_ref[...], preferred_element_type=F32)
    finish(rows(n - 1), acc_ref[(n - 1) % 2])


def _mm_body(*refs, n_extra, has_side, epilogue, w_is_nk):
    it = iter(refs)
    a_ref = next(it)
    w_ref = next(it)
    side_ref = next(it) if has_side else None
    extras = [next(it) for _ in range(n_extra)]
    o_ref = next(it)
    oside_ref = next(it) if has_side else None
    wbf_ref = next(it)
    acc_ref = next(it) if epilogue is not None else None

    @pl.when(pl.program_id(1) == 0)
    def _():
        w = w_ref[...]
        wbf_ref[...] = (w.T if w_is_nk else w).astype(BF16)
        if has_side:
            oside_ref[...] = jnp.dot(side_ref[...], wbf_ref[...],
                                     preferred_element_type=F32)

    if epilogue is None:
        o_ref[...] = jnp.dot(a_ref[...], wbf_ref[...],
                             preferred_element_type=F32).astype(o_ref.dtype)
    else:
        def finish(rows, acc):
            o_ref[rows, :] = epilogue(acc, *[e[rows, :] for e in extras]).astype(o_ref.dtype)
        _chunked_matmul(a_ref, wbf_ref, acc_ref, finish)


def _matmul(a, w, col_off, n_cols, *, tm, tn, out_dtype, side=None, extras=(),
            epilogue=None, w_is_nk=False, name="matmul"):
    m, k = a.shape
    assert m % tm == 0 and n_cols % tn == 0
    if w_is_nk:
        assert col_off % SUBLANE == 0
        w_spec = pl.BlockSpec((pl.Element(tn), pl.Element(k)),
                              lambda n, i: (pl.multiple_of(col_off + n * tn, SUBLANE), 0))
    else:
        assert col_off % tn == 0
        w_spec = pl.BlockSpec((k, tn), lambda n, i, cb=col_off // tn: (0, n + cb))
    in_specs = [pl.BlockSpec((tm, k), lambda n, i: (i, 0)), w_spec]
    args = [a, w]
    if side is not None:
        in_specs.append(pl.BlockSpec((side.shape[0], k), lambda n, i: (0, 0)))
        args.append(side)
    for arr, off in extras:
        assert off % tn == 0
        in_specs.append(pl.BlockSpec((tm, tn), lambda n, i, o=off // tn: (i, n + o)))
        args.append(arr)
    out_shape = [jax.ShapeDtypeStruct((m, n_cols), out_dtype)]
    out_specs = [pl.BlockSpec((tm, tn), lambda n, i: (i, n))]
    if side is not None:
        out_shape.append(jax.ShapeDtypeStruct((side.shape[0], n_cols), F32))
        out_specs.append(pl.BlockSpec((side.shape[0], tn), lambda n, i: (0, n)))
    res = pl.pallas_call(
        functools.partial(_mm_body, n_extra=len(extras), has_side=side is not None,
                          epilogue=epilogue, w_is_nk=w_is_nk),
        grid=(n_cols // tn, m // tm),
        in_specs=in_specs,
        out_specs=out_specs,
        out_shape=out_shape,
        scratch_shapes=[pltpu.VMEM((k, tn), BF16)]
        + ([pltpu.VMEM((2, min(tm, EPILOGUE_ROWS), tn), F32)] if epilogue is not None else []),
        compiler_params=_cparams(2),
        name=name,
    )(*args)
    return res if side is not None else res[0]


def _causal_conv(raw, prev, w, n_tap):
    acc = raw * w[n_tap - 1:n_tap, :]
    row8 = lax.broadcasted_iota(I32, (SUBLANE, raw.shape[1]), 0)
    for s in range(1, n_tap):
        sh = pltpu.roll(raw, s, 0)
        shp = pltpu.roll(prev, s, 0)
        first = jnp.where(row8 < s, shp, sh[:SUBLANE])
        sh = jnp.concatenate([first, sh[SUBLANE:]], axis=0)
        acc = acc + sh * w[n_tap - 1 - s:n_tap - s, :]
    return acc


def _proj_conv_body(*refs, n_w, n_tap, tiles_per_batch, mode):
    it = iter(refs)
    a_ref = next(it)
    w_refs = [next(it) for _ in range(n_w)]
    side_ref = next(it)
    cw_ref = next(it)
    cb_ref = next(it) if mode == "xbc" else None
    o_ref = next(it)
    oside_ref = next(it) if mode == "xbc" else None
    wbf_ref = next(it)
    prev_ref = next(it)
    tail_ref = next(it)
    acc_ref = next(it)
    tn = o_ref.shape[1]
    cw = cw_ref[...]

    def pre(acc):
        return acc if mode == "xbc" else acc[:, tn:2 * tn] * acc[:, 2 * tn:]

    def post(y, acc):
        if mode == "xbc":
            v = y + cb_ref[...]
            return v * _sigmoid(v)
        return acc[:, :tn] * y

    m = pl.program_id(1)

    @pl.when(m == 0)
    def _():
        for i, w_ref in enumerate(w_refs):
            wbf_ref[:, i * tn:(i + 1) * tn] = w_ref[...].T.astype(BF16)
        acc_m = jnp.dot(side_ref[...], wbf_ref[...], preferred_element_type=F32)
        u_m = pre(acc_m)
        tail_ref[...] = u_m[N_META - SUBLANE:, :]
        if mode == "xbc":
            y_m = _causal_conv(u_m, jnp.zeros((SUBLANE, tn), F32), cw, n_tap)
            oside_ref[...] = post(y_m, acc_m)

    @pl.when(m % tiles_per_batch == 0)
    def _():
        prev_ref[...] = tail_ref[...]

    if mode == "xbc":
        ch = acc_ref.shape[1] - SUBLANE
        for c in range(a_ref.shape[0] // ch):
            sl = c % 2
            acc_ref[sl, :SUBLANE, :] = prev_ref[...]
            acc_ref[sl, SUBLANE:, :] = jnp.dot(a_ref[c * ch:(c + 1) * ch, :], wbf_ref[...],
                                               preferred_element_type=F32)
            y = None
            for k in range(n_tap):
                r0 = SUBLANE - (n_tap - 1) + k
                term = acc_ref[sl, r0:r0 + ch, :] * cw[k:k + 1, :]
                y = term if y is None else y + term
            prev_ref[...] = acc_ref[sl, ch:, :]
            o_ref[c * ch:(c + 1) * ch, :] = post(y, None).astype(o_ref.dtype)
    else:
        def finish(rows, acc):
            u = pre(acc)
            y = _causal_conv(u, prev_ref[...], cw, n_tap)
            prev_ref[...] = u[u.shape[0] - SUBLANE:, :]
            o_ref[rows, :] = post(y, acc).astype(o_ref.dtype)
        _chunked_matmul(a_ref, wbf_ref, acc_ref, finish)


def _proj_conv(a, w_nk, row_offs, side, cw, cb, *, width, tm, tn, seq, mode, name):
    m, k = a.shape
    n_w = len(row_offs)
    n_tap = cw.shape[0]
    ms = side.shape[0]

    def w_spec(off):
        return pl.BlockSpec((pl.Element(tn), pl.Element(k)),
                            lambda n, i: (pl.multiple_of(off + n * tn, SUBLANE), 0))

    in_specs = [pl.BlockSpec((tm, k), lambda n, i: (i, 0))]
    in_specs += [w_spec(off) for off in row_offs]
    in_specs += [pl.BlockSpec((ms, k), lambda n, i: (0, 0)),
                 pl.BlockSpec((n_tap, tn), lambda n, i: (0, n))]
    args = [a] + [w_nk] * n_w + [side, cw]
    out_shape = [jax.ShapeDtypeStruct((m, width), F32 if mode == "xbc" else BF16)]
    out_specs = [pl.BlockSpec((tm, tn), lambda n, i: (i, n))]
    if mode == "xbc":
        in_specs.append(pl.BlockSpec((1, tn), lambda n, i: (0, n)))
        args.append(cb)
        out_shape.append(jax.ShapeDtypeStruct((ms, width), F32))
        out_specs.append(pl.BlockSpec((ms, tn), lambda n, i: (0, n)))
    res = pl.pallas_call(
        functools.partial(_proj_conv_body, n_w=n_w, n_tap=n_tap,
                          tiles_per_batch=seq // tm, mode=mode),
        grid=(width // tn, m // tm),
        in_specs=in_specs,
        out_specs=out_specs,
        out_shape=out_shape,
        scratch_shapes=[pltpu.VMEM((k, n_w * tn), BF16),
                        pltpu.VMEM((SUBLANE, tn), F32),
                        pltpu.VMEM((SUBLANE, tn), F32),
                        pltpu.VMEM((2, min(tm, EPILOGUE_ROWS) + SUBLANE * (mode == "xbc"),
                                    n_w * tn), F32)],
        compiler_params=_cparams(2, VMEM_MB + 4 * (n_w > 1)),
        name=name,
    )(*args)
    return res if mode == "xbc" else res[0]


def _ssd_chunk(xbc, dt_raw, dtb, alog, e2, *, heads, meta):
    L = xbc.shape[0]
    di = heads * SSD_HEAD_DIM
    gn = SSD_GROUPS * SSD_STATE
    dt_in = dt_raw[:, :heads] + dtb
    dtv = jnp.maximum(dt_in, 0.0) + jnp.log1p(jnp.exp(-jnp.abs(dt_in)))
    if meta:
        dtv = dtv * (lax.broadcasted_iota(I32, (L, 1), 0) >= L - N_META).astype(F32)
    xs = xbc[:, :di]
    bm = xbc[:, di:di + gn]
    cm = xbc[:, di + gn:]
    a_neg = -jnp.exp(alog)
    ad = dtv * a_neg
    r = lax.broadcasted_iota(I32, (L, L), 0)
    c = lax.broadcasted_iota(I32, (L, L), 1)
    tril = r >= c
    acum = jnp.dot(tril.astype(F32), ad, precision=lax.Precision.HIGHEST,
                   preferred_element_type=F32)
    stack = jnp.concatenate([dtv, dtv * jnp.exp(acum[L - 1:L, :] - acum), jnp.exp(acum)],
                            axis=0)
    hi = stack.astype(BF16)
    lo = (stack - hi.astype(F32)).astype(BF16)
    exp_x = jnp.dot(jnp.concatenate([hi, lo], axis=1), e2, preferred_element_type=F32)
    dt_x, dd_x, ea_x = exp_x[:L], exp_x[L:2 * L], exp_x[2 * L:]
    bt = bm.T.astype(BF16)
    return dict(xs=xs, cm=cm, bt=bt, acum=acum, ea_x=ea_x, x_dt=(xs * dt_x).astype(BF16),
                x_dec=(xs * dd_x).astype(BF16), tril=tril)


def _ssd_state_update(q, st_ref, heads):
    gw = heads * SSD_HEAD_DIM // SSD_GROUPS
    L = q["ea_x"].shape[0]
    for g in range(SSD_GROUPS):
        sl = slice(g * gw, (g + 1) * gw)
        btg = q["bt"][g * SSD_STATE:(g + 1) * SSD_STATE, :]
        new = jnp.dot(btg, q["x_dec"][:, sl], preferred_element_type=F32)
        st_ref[:, sl] = st_ref[:, sl] * q["ea_x"][L - 1:L, sl] + new


def _ssd_body(xbc_ref, dt_ref, z_ref, s0_ref, dtb_ref, alog_ref, dskip_ref, ng_ref, e_ref,
              o_ref, st_ref, *, heads):
    @pl.when(pl.program_id(1) == 0)
    def _():
        st_ref[...] = s0_ref[...]

    L = xbc_ref.shape[0]
    q = _ssd_chunk(xbc_ref[...], dt_ref[...], dtb_ref[...], alog_ref[...], e_ref[...],
                   heads=heads, meta=False)

    gw = heads * SSD_HEAD_DIM // SSD_GROUPS
    j_heads = heads // SSD_GROUPS
    acum = q["acum"]
    acum_t = acum.T
    cmb = q["cm"].astype(BF16)
    x_dt_b = q["x_dt"]
    lane = lax.broadcasted_iota(I32, (L, 2 * SSD_HEAD_DIM), 1)
    lo = lane < SSD_HEAD_DIM
    neg_inf = jnp.float32(-jnp.inf)

    def decay_mat(h):
        col = jnp.broadcast_to(acum[:, h:h + 1], (L, L))
        row = jnp.broadcast_to(acum_t[h:h + 1, :], (L, L))
        return jnp.exp(jnp.where(q["tril"], col - row, neg_inf))

    for g in range(SSD_GROUPS):
        sl = slice(g * gw, (g + 1) * gw)
        cg = cmb[:, g * SSD_STATE:(g + 1) * SSD_STATE]
        btg = q["bt"][g * SSD_STATE:(g + 1) * SSD_STATE, :]
        cb_mat = jnp.dot(cg, btg, preferred_element_type=F32)
        parts = []
        for p in range(j_heads // 2):
            h1 = g * j_heads + 2 * p
            slab = x_dt_b[:, h1 * SSD_HEAD_DIM:(h1 + 2) * SSD_HEAD_DIM]
            lhs = jnp.concatenate([(cb_mat * decay_mat(h1)).astype(BF16),
                                   (cb_mat * decay_mat(h1 + 1)).astype(BF16)], axis=1)
            zero = jnp.zeros_like(slab)
            rhs = jnp.concatenate([jnp.where(lo, slab, zero),
                                   jnp.where(lo, zero, slab)], axis=0)
            parts.append(jnp.dot(lhs, rhs, preferred_element_type=F32))
        y_diag = parts[0] if len(parts) == 1 else jnp.concatenate(parts, axis=1)
        st_g = st_ref[:, sl]
        y_off = jnp.dot(cg, st_g.astype(BF16), preferred_element_type=F32)
        y = y_diag + y_off * q["ea_x"][:, sl] + dskip_ref[:, sl] * q["xs"][:, sl]
        zg = z_ref[:, sl]
        gated = y * (zg * _sigmoid(zg))
        ms = jnp.mean(gated * gated, axis=-1, keepdims=True)
        o_ref[:, sl] = (gated * lax.rsqrt(ms + EPS) * ng_ref[:, sl]).astype(o_ref.dtype)
        new = jnp.dot(btg, q["x_dec"][:, sl], preferred_element_type=F32)
        st_ref[:, sl] = st_g * q["ea_x"][L - 1:L, sl] + new


def _ssd_meta_body(xbc_ref, dt_ref, dtb_ref, alog_ref, e_ref, st_ref, *, heads):
    st_ref[...] = jnp.zeros_like(st_ref)
    q = _ssd_chunk(xbc_ref[...], dt_ref[...], dtb_ref[...], alog_ref[...], e_ref[...],
                   heads=heads, meta=True)
    _ssd_state_update(q, st_ref, heads)


def _full_spec(shape):
    nd = len(shape)
    return pl.BlockSpec(shape, lambda *_: (0,) * nd)


def _ssd_meta(xbc_m, dt_m, dtb, alog, e2, heads):
    di = heads * SSD_HEAD_DIM
    args = [xbc_m, dt_m, dtb, alog, e2]
    return pl.pallas_call(
        functools.partial(_ssd_meta_body, heads=heads),
        grid=(1,),
        in_specs=[_full_spec(a.shape) for a in args],
        out_specs=_full_spec((SSD_STATE, di)),
        out_shape=jax.ShapeDtypeStruct((SSD_STATE, di), F32),
        compiler_params=_cparams(1),
        name="ssd_meta",
    )(*args)


def _ssd_main(xbc, dt, z, s0, dtb, alog, dskip_x, ng, e2, *, heads, batch):
    t, cd = xbc.shape
    di = heads * SSD_HEAD_DIM
    L = SSD_CHUNK
    nc = t // batch // L
    row = lambda b, c: (b * nc + c, 0)
    in_specs = [pl.BlockSpec((L, cd), row),
                pl.BlockSpec((L, LANE), row),
                pl.BlockSpec((L, di), row)]
    consts = [s0, dtb, alog, dskip_x, ng, e2]
    in_specs += [_full_spec(a.shape) for a in consts]
    return pl.pallas_call(
        functools.partial(_ssd_body, heads=heads),
        grid=(batch, nc),
        in_specs=in_specs,
        out_specs=pl.BlockSpec((L, di), row),
        out_shape=jax.ShapeDtypeStruct((t, di), BF16),
        scratch_shapes=[pltpu.VMEM((SSD_STATE, di), F32)],
        compiler_params=_cparams(2),
        name="ssd_main",
    )(xbc, dt, z, *consts)


def _router_body(x_ref, g_ref, wr_ref, br_ref, hp_ref, idx_ref, prob_ref, rank_ref,
                 cnt_ref, base_ref):
    @pl.when(pl.program_id(0) == 0)
    def _():
        base_ref[...] = jnp.zeros_like(base_ref)

    x = x_ref[...]
    tm, d = x.shape
    ne = wr_ref.shape[0]
    ms = jnp.mean(x * x, axis=-1, keepdims=True)
    h = x * lax.rsqrt(ms + EPS) * g_ref[...]
    half = d // 2
    lo_bits = lax.bitcast_convert_type(h[:, :half].astype(BF16).astype(F32), U32)
    hi_bits = lax.bitcast_convert_type(h[:, half:].astype(BF16).astype(F32), U32)
    hp_ref[...] = (lo_bits >> 16) | (hi_bits & jnp.uint32(0xFFFF0000))

    logits = lax.dot_general(wr_ref[...], h, (((1,), (1,)), ((), ())),
                             precision=lax.Precision.HIGHEST,
                             preferred_element_type=F32) + br_ref[:, 0:1]
    eidx = lax.broadcasted_iota(I32, (ne, tm), 0)
    work = logits
    vals, onehots = [], []
    for k in range(TOP_K):
        mx = jnp.max(work, axis=0, keepdims=True)
        sel = jnp.min(jnp.where(work == mx, eidx, ne), axis=0, keepdims=True)
        oh = eidx == sel
        idx_ref[k:k + 1, :] = sel
        vals.append(mx)
        onehots.append(oh)
        work = jnp.where(oh, -jnp.inf, work)
    exps = [jnp.exp(v - vals[0]) for v in vals]
    denom = exps[0]
    for e in exps[1:]:
        denom = denom + e
    for k in range(TOP_K):
        prob_ref[k:k + 1, :] = exps[k] / denom

    cnt = onehots[0].astype(F32)
    for oh in onehots[1:]:
        cnt = cnt + oh.astype(F32)
    r = lax.broadcasted_iota(I32, (tm, tm), 0)
    c = lax.broadcasted_iota(I32, (tm, tm), 1)
    before = (r < c).astype(BF16)
    prefix = jnp.dot(cnt.astype(BF16), before, preferred_element_type=F32)
    prefix = prefix + base_ref[:, 0:1]
    for k in range(TOP_K):
        rk = jnp.sum(jnp.where(onehots[k], prefix, 0.0), axis=0, keepdims=True)
        rank_ref[k:k + 1, :] = rk.astype(I32)
    base_ref[...] = base_ref[...] + jnp.sum(cnt, axis=1, keepdims=True)
    cnt_ref[...] = base_ref[...]


def _router(hres, g, wr_t, br, tm):
    t, d = hres.shape
    ne = wr_t.shape[0]
    tok = lambda i: (0, i)
    return pl.pallas_call(
        _router_body,
        grid=(t // tm,),
        in_specs=[pl.BlockSpec((tm, d), lambda i: (i, 0)),
                  _full_spec((1, d)), _full_spec((ne, d)), _full_spec((ne, LANE))],
        out_specs=[pl.BlockSpec((tm, d // 2), lambda i: (i, 0)),
                   pl.BlockSpec((TOP_K, tm), tok),
                   pl.BlockSpec((TOP_K, tm), tok),
                   pl.BlockSpec((TOP_K, tm), tok),
                   _full_spec((ne, LANE))],
        out_shape=[jax.ShapeDtypeStruct((t, d // 2), U32),
                   jax.ShapeDtypeStruct((TOP_K, t), I32),
                   jax.ShapeDtypeStruct((TOP_K, t), F32),
                   jax.ShapeDtypeStruct((TOP_K, t), I32),
                   jax.ShapeDtypeStruct((ne, LANE), F32)],
        scratch_shapes=[pltpu.VMEM((ne, LANE), F32)],
        compiler_params=_cparams(1),
        name="router",
    )(hres, g.reshape(1, d), wr_t, jnp.broadcast_to(br.reshape(ne, 1), (ne, LANE)))


def _dispatch_body(dest_ref, cnt_ref, seg_ref, tot_ref, hp_ref, xs_ref, zrow_ref, zblk_ref,
                   row_sem, pad_sem, *, t_total, ne, n_blocks):
    i = pl.program_id(0)
    tm = hp_ref.shape[0]

    def pad_copy(slot):
        return pltpu.make_async_copy(zrow_ref.at[pl.ds(0, 1)], xs_ref.at[pl.ds(slot, 1)], pad_sem)

    def tail_copy(b):
        return pltpu.make_async_copy(
            zblk_ref, xs_ref.at[pl.ds(pl.multiple_of(b * MOE_BLK, MOE_BLK), MOE_BLK)], pad_sem)

    @pl.when(i == 0)
    def _():
        zrow_ref[...] = jnp.zeros_like(zrow_ref)
        zblk_ref[...] = jnp.zeros_like(zblk_ref)

        def expert(e, carry):
            used = cnt_ref[e]
            end = (used + MOE_BLK - 1) // MOE_BLK * MOE_BLK

            def start(r, c):
                pad_copy(seg_ref[e] + r).start()
                return c
            lax.fori_loop(used, end, start, 0)

            def wait(r, c):
                pad_copy(seg_ref[e] + r).wait()
                return c
            lax.fori_loop(used, end, wait, 0)
            return carry
        lax.fori_loop(0, ne, expert, 0)

        def tail_start(b, c):
            tail_copy(b).start()
            return c
        lax.fori_loop(tot_ref[0], n_blocks, tail_start, 0)

        def tail_wait(b, c):
            tail_copy(b).wait()
            return c
        lax.fori_loop(tot_ref[0], n_blocks, tail_wait, 0)

    for k in range(TOP_K):
        def issue(r, c, k=k):
            slot = dest_ref[k * t_total + i * tm + r]
            pltpu.make_async_copy(hp_ref.at[pl.ds(r, 1)], xs_ref.at[pl.ds(slot, 1)],
                                  row_sem).start()
            return c
        lax.fori_loop(0, tm, issue, 0, unroll=8)
    for k in range(TOP_K):
        pltpu.make_async_copy(hp_ref, xs_ref.at[pl.ds(0, tm)], row_sem).wait()


def _moe_dispatch(dest_flat, cnt, seg_start, total_blk, hp, n_rows, *, tm):
    t, w = hp.shape
    ne = cnt.shape[0]
    return pl.pallas_call(
        functools.partial(_dispatch_body, t_total=t, ne=ne, n_blocks=n_rows // MOE_BLK),
        grid_spec=pltpu.PrefetchScalarGridSpec(
            num_scalar_prefetch=4,
            grid=(t // tm,),
            in_specs=[pl.BlockSpec((tm, w), lambda i, *_: (i, 0))],
            out_specs=pl.BlockSpec(memory_space=pl.ANY),
            scratch_shapes=[pltpu.VMEM((SUBLANE, w), U32),
                            pltpu.VMEM((MOE_BLK, w), U32),
                            pltpu.SemaphoreType.DMA,
                            pltpu.SemaphoreType.DMA],
        ),
        out_shape=jax.ShapeDtypeStruct((n_rows, w), U32),
        compiler_params=_cparams(1),
        name="moe_dispatch",
    )(dest_flat, cnt, seg_start, total_blk, hp)


def _sb_specs(rows, width_in, n_tiles):
    def x_map(s, j, e, start, nblk, tot):
        return (pl.multiple_of(start[s], MOE_BLK), 0)

    def wcol(s, j, nblk):
        return jnp.where(nblk[s] > 0, j, n_tiles - 1)

    return pl.BlockSpec((pl.Element(rows), pl.Element(width_in)), x_map), wcol


def _sb_run(start_ref, nblk_ref, tot_ref, o_ref, stage_ref, zblk_ref, out_sems, zero_sem,
            compute_rows, *, n_blocks):
    s = pl.program_id(0)
    j = pl.program_id(1)
    nj = pl.num_programs(1)
    q = s * nj + j
    slot = q % 2
    tile = stage_ref.shape[2]
    col = pl.multiple_of(j * tile, LANE)
    nb = nblk_ref[s]

    def out_copy(sl, b, row0):
        r0 = pl.multiple_of(b * MOE_BLK, MOE_BLK)
        return pltpu.make_async_copy(
            stage_ref.at[sl, pl.ds(r0, MOE_BLK)],
            o_ref.at[pl.ds(pl.multiple_of(row0 + r0, MOE_BLK), MOE_BLK), pl.ds(col, tile)],
            out_sems.at[sl])

    def drain(sl, count):
        def one(b, c):
            out_copy(sl, 0, 0).wait()
            return c
        lax.fori_loop(0, count, one, 0)

    def zero_copy(b):
        return pltpu.make_async_copy(
            zblk_ref,
            o_ref.at[pl.ds(pl.multiple_of(b * MOE_BLK, MOE_BLK), MOE_BLK), pl.ds(col, tile)],
            zero_sem)

    @pl.when(q >= 2)
    def _():
        drain(slot, nblk_ref[(q - 2) // nj])

    @pl.when(s == 0)
    def _():
        zblk_ref[...] = jnp.zeros_like(zblk_ref)

        def zs(b, c):
            zero_copy(b).start()
            return c
        lax.fori_loop(tot_ref[0], n_blocks, zs, 0)

        def zw(b, c):
            zero_copy(b).wait()
            return c
        lax.fori_loop(tot_ref[0], n_blocks, zw, 0)

    def run(b0, k):
        compute_rows(slot, pl.multiple_of(b0 * MOE_BLK, MOE_BLK), k * MOE_BLK)
        for i in range(k):
            out_copy(slot, b0 + i, start_ref[s]).start()

    k = 1 << (MOE_SB_BLKS.bit_length() - 1)
    while k >= 1:
        @pl.when(nb & k != 0)
        def _(k=k):
            run(nb & ~(2 * k - 1), k)
        k //= 2

    @pl.when(q == pl.num_programs(0) * nj - 1)
    def _():
        drain(slot, nb)

        @pl.when(q >= 1)
        def _():
            drain(1 - slot, nblk_ref[(q - 1) // nj])


def _gate_up_body(e_ref, start_ref, nblk_ref, tot_ref, x_ref, wg_ref, wu_ref, bg_ref, bu_ref,
                  o_ref, wbf_ref, stage_ref, zblk_ref, out_sems, zero_sem, *, n_blocks):
    half = x_ref.shape[1]
    tj = wg_ref.shape[1]

    @pl.when(nblk_ref[pl.program_id(0)] > 0)
    def _():
        wbf_ref[:, :tj] = wg_ref[...].astype(BF16)
        wbf_ref[:, tj:] = wu_ref[...].astype(BF16)

    bias = jnp.concatenate([bg_ref[...], bu_ref[...]], axis=1)

    def compute_rows(slot, r0, n_rows):
        xp = x_ref[pl.ds(r0, n_rows), :]
        x_lo = lax.bitcast_convert_type(xp << 16, F32).astype(BF16)
        x_hi = lax.bitcast_convert_type(xp & jnp.uint32(0xFFFF0000), F32).astype(BF16)
        gu = (jnp.dot(x_lo, wbf_ref[:half, :], preferred_element_type=F32)
              + jnp.dot(x_hi, wbf_ref[half:, :], preferred_element_type=F32) + bias)
        gate = jnp.minimum(gu[:, :tj], SWIGLU_LIMIT)
        up = jnp.clip(gu[:, tj:], -SWIGLU_LIMIT, SWIGLU_LIMIT)
        act = (up + 1.0) * (gate * _sigmoid(gate * SWIGLU_ALPHA))
        stage_ref[slot, pl.ds(r0, n_rows), :] = act.astype(stage_ref.dtype)

    _sb_run(start_ref, nblk_ref, tot_ref, o_ref, stage_ref, zblk_ref, out_sems, zero_sem,
            compute_rows, n_blocks=n_blocks)


def _sb_scratch(rows, tile, dtype):
    return [pltpu.VMEM((2, rows, tile), dtype), pltpu.VMEM((MOE_BLK, tile), dtype),
            pltpu.SemaphoreType.DMA((2,)), pltpu.SemaphoreType.DMA]


def _moe_gate_up(sb, n_sb, xs, w_gu, b_gu, *, tj):
    ne, d, de2 = w_gu.shape
    de = de2 // 2
    nj = de // tj
    rows = MOE_SB_BLKS * MOE_BLK
    n_rows, half = xs.shape
    x_spec, wcol = _sb_specs(rows, half, nj)
    in_specs = [
        x_spec,
        pl.BlockSpec((None, d, tj), lambda s, j, e, st, nb, tt: (e[s], 0, wcol(s, j, nb))),
        pl.BlockSpec((None, d, tj), lambda s, j, e, st, nb, tt: (e[s], 0, nj + wcol(s, j, nb))),
        pl.BlockSpec((None, 1, tj), lambda s, j, e, st, nb, tt: (e[s], 0, wcol(s, j, nb))),
        pl.BlockSpec((None, 1, tj), lambda s, j, e, st, nb, tt: (e[s], 0, nj + wcol(s, j, nb))),
    ]
    return pl.pallas_call(
        functools.partial(_gate_up_body, n_blocks=n_rows // MOE_BLK),
        grid_spec=pltpu.PrefetchScalarGridSpec(
            num_scalar_prefetch=4,
            grid=(n_sb, nj),
            in_specs=in_specs,
            out_specs=pl.BlockSpec(memory_space=pl.ANY),
            scratch_shapes=[pltpu.VMEM((d, 2 * tj), BF16)] + _sb_scratch(rows, tj, BF16),
        ),
        out_shape=jax.ShapeDtypeStruct((n_rows, de), BF16),
        compiler_params=_cparams(2),
        name="moe_gate_up",
    )(*sb, xs, w_gu, w_gu, b_gu.reshape(ne, 1, de2), b_gu.reshape(ne, 1, de2))


def _down_body(e_ref, start_ref, nblk_ref, tot_ref, a_ref, w_ref, b_ref, o_ref, wbf_ref,
               stage_ref, zblk_ref, out_sems, zero_sem, *, n_blocks):
    @pl.when(nblk_ref[pl.program_id(0)] > 0)
    def _():
        wbf_ref[...] = w_ref[...].astype(BF16)

    bias = b_ref[...]

    def compute_rows(slot, r0, n_rows):
        stage_ref[slot, pl.ds(r0, n_rows), :] = jnp.dot(
            a_ref[pl.ds(r0, n_rows), :], wbf_ref[...], preferred_element_type=F32) + bias

    _sb_run(start_ref, nblk_ref, tot_ref, o_ref, stage_ref, zblk_ref, out_sems, zero_sem,
            compute_rows, n_blocks=n_blocks)


def _moe_down(sb, n_sb, act, w_down, b_down, *, tn):
    ne, de, d = w_down.shape
    nn = d // tn
    rows = MOE_SB_BLKS * MOE_BLK
    n_rows = act.shape[0]
    a_spec, wcol = _sb_specs(rows, de, nn)
    in_specs = [
        a_spec,
        pl.BlockSpec((None, de, tn), lambda s, j, e, st, nb, tt: (e[s], 0, wcol(s, j, nb))),
        pl.BlockSpec((None, 1, tn), lambda s, j, e, st, nb, tt: (e[s], 0, wcol(s, j, nb))),
    ]
    return pl.pallas_call(
        functools.partial(_down_body, n_blocks=n_rows // MOE_BLK),
        grid_spec=pltpu.PrefetchScalarGridSpec(
            num_scalar_prefetch=4,
            grid=(n_sb, nn),
            in_specs=in_specs,
            out_specs=pl.BlockSpec(memory_space=pl.ANY),
            scratch_shapes=[pltpu.VMEM((de, tn), BF16)] + _sb_scratch(rows, tn, F32),
        ),
        out_shape=jax.ShapeDtypeStruct((n_rows, d), F32),
        compiler_params=_cparams(2),
        name="moe_down",
    )(*sb, act, w_down, b_down.reshape(ne, 1, d))


def _moe_schedule(cnt, n_pairs, ne):
    n_blocks = n_pairs // MOE_BLK + ne + MOE_SB_BLKS - 1
    s_max = -(-(n_pairs // MOE_BLK + ne) // MOE_SB_BLKS) + ne
    pblk = (cnt + MOE_BLK - 1) // MOE_BLK
    seg_end_blk = jnp.cumsum(pblk)
    seg_start = (seg_end_blk - pblk) * MOE_BLK
    total_blk = seg_end_blk[-1:]
    nsb_e = (pblk + MOE_SB_BLKS - 1) // MOE_SB_BLKS
    sb_end = jnp.cumsum(nsb_e)
    sb_first = sb_end - nsb_e
    n_sb = sb_end[-1]
    s_ids = jnp.arange(s_max, dtype=I32)
    used = s_ids < n_sb
    e_of_s = jnp.minimum(jnp.searchsorted(sb_end, s_ids, side="right"), ne - 1).astype(I32)
    i_in_e = s_ids - sb_first[e_of_s]
    start = seg_start[e_of_s] + i_in_e * (MOE_SB_BLKS * MOE_BLK)
    nblk = jnp.where(used, jnp.clip(pblk[e_of_s] - MOE_SB_BLKS * i_in_e, 0, MOE_SB_BLKS), 0)
    last = jnp.maximum(n_sb - 1, 0)
    sb = (jnp.where(used, e_of_s, e_of_s[last]).astype(I32),
          jnp.where(used, start, start[last]).astype(I32), nblk.astype(I32),
          total_blk.astype(I32))
    return sb, n_sb.astype(I32), seg_start.astype(I32), n_blocks * MOE_BLK


def _combine_body(dest_ref, y_ref, h_ref, p_ref, g_ref, o_ref, ybuf, sems, *, t_total):
    i = pl.program_id(0)
    n = pl.num_programs(0)
    tc = h_ref.shape[0]

    def issue(tile, slot):
        for k in range(TOP_K):
            def one(r, carry, k=k):
                row = dest_ref[k * t_total + tile * tc + r]
                pltpu.make_async_copy(y_ref.at[pl.ds(row, 1)],
                                      ybuf.at[slot, k, pl.ds(r, 1)],
                                      sems.at[slot]).start()
                return carry
            lax.fori_loop(0, tc, one, 0, unroll=8)

    @pl.when(i == 0)
    def _():
        issue(0, 0)

    slot = i % 2
    for k in range(TOP_K):
        pltpu.make_async_copy(y_ref.at[pl.ds(0, tc)], ybuf.at[slot, k], sems.at[slot]).wait()

    @pl.when(i + 1 < n)
    def _():
        issue(i + 1, 1 - slot)

    acc = h_ref[...]
    p = p_ref[...]
    for k in range(TOP_K):
        acc = acc + p[:, k:k + 1] * ybuf[slot, k]
    ms = jnp.mean(acc * acc, axis=-1, keepdims=True)
    o_ref[...] = acc * lax.rsqrt(ms + EPS) * g_ref[...]


def _combine(dest_flat, y_sorted, hres, p_t, final_g, *, tc):
    t, d = hres.shape
    return pl.pallas_call(
        functools.partial(_combine_body, t_total=t),
        grid_spec=pltpu.PrefetchScalarGridSpec(
            num_scalar_prefetch=1,
            grid=(t // tc,),
            in_specs=[pl.BlockSpec(memory_space=pl.ANY),
                      pl.BlockSpec((tc, d), lambda i, *_: (i, 0)),
                      pl.BlockSpec((tc, TOP_K), lambda i, *_: (i, 0)),
                      pl.BlockSpec((1, d), lambda i, *_: (0, 0))],
            out_specs=pl.BlockSpec((tc, d), lambda i, *_: (i, 0)),
            scratch_shapes=[pltpu.VMEM((2, TOP_K, tc, d), F32),
                            pltpu.SemaphoreType.DMA((2,))],
        ),
        out_shape=jax.ShapeDtypeStruct((t, d), F32),
        compiler_params=_cparams(1),
        name="moe_combine",
    )(dest_flat, y_sorted, hres, p_t, final_g.reshape(1, d))


def _pick(n, prefs):
    for p in prefs:
        if n % p == 0:
            return p
    raise ValueError(f"no tile in {prefs} divides {n}")


def kernel(x, meta_tokens, norm_mix_g, w_in, ssd_conv_w, ssd_conv_b, dt_bias, a_log, d_skip,
           ssd_norm_g, w_ssd_out, sc_conv_w, w_sc_out, w_o, norm_ffn_g, w_router, b_router,
           w_gate_up, b_gate_up, w_down, b_down, final_g):
    batch, seq, d = x.shape
    assert norm_mix_g.shape[0] == 1, "meta-token handling assumes a single layer"
    assert seq % SSD_CHUNK == 0 and N_META <= SSD_CHUNK
    t = batch * seq
    heads = d // SSD_HEAD_DIM
    gn = SSD_GROUPS * SSD_STATE
    conv_dim = d + 2 * gn
    xbc_off = d
    dt_off = xbc_off + conv_dim
    sc_off = dt_off + heads
    gate_off = sc_off + 3 * d
    ne = w_router.shape[-1]
    de = w_down.shape[2]

    x2 = x.reshape(t, d)
    w_in_nk = jnp.swapaxes(w_in[0], 0, 1)

    tm = _pick(seq, (1024, 512, 256, 128))
    tn = _pick(d, (512, 256, 128))

    h = _rmsnorm_bf16(x2, norm_mix_g[0], _pick(t, (256, 128)))
    h_meta = _rmsnorm_bf16(meta_tokens, norm_mix_g[0], N_META)

    proj = functools.partial(_matmul, h, w_in_nk, tm=tm, out_dtype=F32, w_is_nk=True)
    z = proj(0, d, tn=tn, name="proj_z")
    xbc, xbc_meta = _proj_conv(h, w_in_nk, [xbc_off], h_meta, ssd_conv_w[0],
                               ssd_conv_b[0].reshape(1, conv_dim), width=conv_dim, tm=tm,
                               tn=tn, seq=seq, mode="xbc", name="proj_xbc")
    dtr, dt_meta = proj(dt_off, LANE, tn=LANE, side=h_meta, name="proj_dt")
    u_sc = _proj_conv(h, w_in_nk, [sc_off, sc_off + d, sc_off + 2 * d], h_meta,
                      sc_conv_w[0], None, width=d, tm=tm, tn=_pick(d, (256, 128)), seq=seq,
                      mode="sc", name="proj_sc")
    gates = proj(gate_off, 2 * d, tn=tn, name="proj_gate")

    e_mat = (jnp.arange(d, dtype=I32)[None, :] // SSD_HEAD_DIM
             == jnp.arange(heads, dtype=I32)[:, None]).astype(BF16)
    e2 = jnp.concatenate([e_mat, e_mat], axis=0)
    pad = SSD_CHUNK - N_META
    dtb = dt_bias[0].reshape(1, heads)
    alog = a_log[0].reshape(1, heads)
    s0 = _ssd_meta(jnp.pad(xbc_meta, ((pad, 0), (0, 0))),
                   jnp.pad(dt_meta, ((pad, 0), (0, 0))), dtb, alog, e2, heads)
    dskip_x = jnp.repeat(d_skip[0], SSD_HEAD_DIM).reshape(1, d)
    g_ssd = _ssd_main(xbc, dtr, z, s0, dtb, alog, dskip_x, ssd_norm_g[0].reshape(1, d), e2,
                      heads=heads, batch=batch)

    y_a = _matmul(g_ssd, w_ssd_out[0], 0, d, tm=tm, tn=tn, out_dtype=F32,
                  extras=[(gates, 0)],
                  epilogue=lambda acc, ga: _sigmoid(ga) * acc, name="ssd_out")
    mixed = _matmul(u_sc, w_sc_out[0], 0, d, tm=tm, tn=tn, out_dtype=BF16,
                    extras=[(gates, d), (y_a, 0)],
                    epilogue=lambda acc, gb, ya: ya + _sigmoid(gb) * acc, name="sc_out")
    hres = _matmul(mixed, w_o[0], 0, d, tm=tm, tn=tn, out_dtype=F32,
                   extras=[(x2, 0)], epilogue=lambda acc, xr: xr + acc, name="w_o")

    rt = _pick(t, (512, 256, 128))
    hp, top_idx, probs, rank, counts = _router(hres, norm_ffn_g[0], w_router[0].T,
                                               b_router[0], rt)
    cnt = counts[:, 0].astype(I32)
    sb, n_sb, seg_start, n_slots = _moe_schedule(cnt, t * TOP_K, ne)
    e_ids = jnp.arange(ne, dtype=I32)[:, None, None]
    dest = rank + jnp.sum(jnp.where(top_idx[None] == e_ids, seg_start[:, None, None], 0),
                          axis=0)
    dest_flat = dest.reshape(-1)

    xs = _moe_dispatch(dest_flat, cnt, seg_start, sb[3], hp, n_slots,
                       tm=_pick(t, (512, 256, 128)))
    act = _moe_gate_up(sb, n_sb, xs, w_gate_up[0], b_gate_up[0], tj=_pick(de, (256, 128)))
    y_sorted = _moe_down(sb, n_sb, act, w_down[0], b_down[0],
                         tn=_pick(d, (1024, 512, 256, 128)))
    out = _combine(dest_flat, y_sorted, hres, probs.T, final_g, tc=_pick(t, (128,)))
    return out.reshape(batch, seq, d)
```

```python
import functools

import jax
import jax.numpy as jnp
from jax import lax
from jax.experimental import pallas as pl
from jax.experimental.pallas import tpu as pltpu

F32 = jnp.float32
BF16 = jnp.bfloat16
I32 = jnp.int32
U32 = jnp.uint32

N_META = 16
SSD_HEAD_DIM = 64
SSD_GROUPS = 8
SSD_STATE = 128
SSD_CONV = 4
SSD_CHUNK = 128
SC_CONV = 3
TOP_K = 4
SWIGLU_LIMIT = 7.0
SWIGLU_ALPHA = 1.702
EPS = 1e-5

LANE = 128
SUBLANE = 8
MOE_BLK = 128
MOE_SB_BLKS = 10
EPILOGUE_ROWS = 256
VMEM_MB = 56


def _cparams(n_axes, vmem_mb=VMEM_MB):
    return pltpu.CompilerParams(
        dimension_semantics=("arbitrary",) * n_axes,
        vmem_limit_bytes=vmem_mb * 1024 * 1024,
    )


def _sigmoid(v):
    return 1.0 / (1.0 + jnp.exp(-v))


def _rms_body(x_ref, g_ref, o_ref):
    x = x_ref[...]
    ms = jnp.mean(x * x, axis=-1, keepdims=True)
    o_ref[...] = (x * lax.rsqrt(ms + EPS) * g_ref[...]).astype(o_ref.dtype)


def _rmsnorm_bf16(x2, g, tm):
    m, d = x2.shape
    return pl.pallas_call(
        _rms_body,
        grid=(m // tm,),
        in_specs=[pl.BlockSpec((tm, d), lambda i: (i, 0)),
                  pl.BlockSpec((1, d), lambda i: (0, 0))],
        out_specs=pl.BlockSpec((tm, d), lambda i: (i, 0)),
        out_shape=jax.ShapeDtypeStruct((m, d), BF16),
        compiler_params=_cparams(1),
        name="rmsnorm",
    )(x2, g.reshape(1, d))


def _chunked_matmul(a_ref, wbf_ref, acc_ref, finish):
    tm = a_ref.shape[0]
    ch = acc_ref.shape[1]
    n = tm // ch

    def rows(c):
        return slice(c * ch, (c + 1) * ch)

    for c in range(n):
        if c > 0:
            finish(rows(c - 1), acc_ref[(c - 1) % 2])
        acc_ref[c % 2] = jnp.dot(a_ref[rows(c), :], wbf_ref[...], preferred_element_type=F32)
    finish(rows(n - 1), acc_ref[(n - 1) % 2])


def _mm_body(*refs, n_extra, has_side, epilogue, w_is_nk):
    it = iter(refs)
    a_ref = next(it)
    w_ref = next(it)
    side_ref = next(it) if has_side else None
    extras = [next(it) for _ in range(n_extra)]
    o_ref = next(it)
    oside_ref = next(it) if has_side else None
    wbf_ref = next(it)
    acc_ref = next(it) if epilogue is not None else None

    @pl.when(pl.program_id(1) == 0)
    def _():
        w = w_ref[...]
        wbf_ref[...] = (w.T if w_is_nk else w).astype(BF16)
        if has_side:
            oside_ref[...] = jnp.dot(side_ref[...], wbf_ref[...],
                                     preferred_element_type=F32)

    if epilogue is None:
        o_ref[...] = jnp.dot(a_ref[...], wbf_ref[...],
                             preferred_element_type=F32).astype(o_ref.dtype)
    else:
        def finish(rows, acc):
            o_ref[rows, :] = epilogue(acc, *[e[rows, :] for e in extras]).astype(o_ref.dtype)
        _chunked_matmul(a_ref, wbf_ref, acc_ref, finish)


def _matmul(a, w, col_off, n_cols, *, tm, tn, out_dtype, side=None, extras=(),
            epilogue=None, w_is_nk=False, name="matmul"):
    m, k = a.shape
    assert m % tm == 0 and n_cols % tn == 0
    if w_is_nk:
        assert col_off % SUBLANE == 0
        w_spec = pl.BlockSpec((pl.Element(tn), pl.Element(k)),
                              lambda n, i: (pl.multiple_of(col_off + n * tn, SUBLANE), 0))
    else:
        assert col_off % tn == 0
        w_spec = pl.BlockSpec((k, tn), lambda n, i, cb=col_off // tn: (0, n + cb))
    in_specs = [pl.BlockSpec((tm, k), lambda n, i: (i, 0)), w_spec]
    args = [a, w]
    if side is not None:
        in_specs.append(pl.BlockSpec((side.shape[0], k), lambda n, i: (0, 0)))
        args.append(side)
    for arr, off in extras:
        assert off % tn == 0
        in_specs.append(pl.BlockSpec((tm, tn), lambda n, i, o=off // tn: (i, n + o)))
        args.append(arr)
    out_shape = [jax.ShapeDtypeStruct((m, n_cols), out_dtype)]
    out_specs = [pl.BlockSpec((tm, tn), lambda n, i: (i, n))]
    if side is not None:
        out_shape.append(jax.ShapeDtypeStruct((side.shape[0], n_cols), F32))
        out_specs.append(pl.BlockSpec((side.shape[0], tn), lambda n, i: (0, n)))
    res = pl.pallas_call(
        functools.partial(_mm_body, n_extra=len(extras), has_side=side is not None,
                          epilogue=epilogue, w_is_nk=w_is_nk),
        grid=(n_cols // tn, m // tm),
        in_specs=in_specs,
        out_specs=out_specs,
        out_shape=out_shape,
        scratch_shapes=[pltpu.VMEM((k, tn), BF16)]
        + ([pltpu.VMEM((2, min(tm, EPILOGUE_ROWS), tn), F32)] if epilogue is not None else []),
        compiler_params=_cparams(2),
        name=name,
    )(*args)
    return res if side is not None else res[0]


def _causal_conv(raw, prev, w, n_tap):
    acc = raw * w[n_tap - 1:n_tap, :]
    row8 = lax.broadcasted_iota(I32, (SUBLANE, raw.shape[1]), 0)
    for s in range(1, n_tap):
        sh = pltpu.roll(raw, s, 0)
        shp = pltpu.roll(prev, s, 0)
        first = jnp.where(row8 < s, shp, sh[:SUBLANE])
        sh = jnp.concatenate([first, sh[SUBLANE:]], axis=0)
        acc = acc + sh * w[n_tap - 1 - s:n_tap - s, :]
    return acc


def _proj_conv_body(*refs, n_w, n_tap, tiles_per_batch, mode):
    it = iter(refs)
    a_ref = next(it)
    w_refs = [next(it) for _ in range(n_w)]
    side_ref = next(it)
    cw_ref = next(it)
    cb_ref = next(it) if mode == "xbc" else None
    o_ref = next(it)
    oside_ref = next(it) if mode == "xbc" else None
    wbf_ref = next(it)
    prev_ref = next(it)
    tail_ref = next(it)
    acc_ref = next(it)
    tn = o_ref.shape[1]
    cw = cw_ref[...]

    def pre(acc):
        return acc if mode == "xbc" else acc[:, tn:2 * tn] * acc[:, 2 * tn:]

    def post(y, acc):
        if mode == "xbc":
            v = y + cb_ref[...]
            return v * _sigmoid(v)
        return acc[:, :tn] * y

    m = pl.program_id(1)

    @pl.when(m == 0)
    def _():
        for i, w_ref in enumerate(w_refs):
            wbf_ref[:, i * tn:(i + 1) * tn] = w_ref[...].T.astype(BF16)
        acc_m = jnp.dot(side_ref[...], wbf_ref[...], preferred_element_type=F32)
        u_m = pre(acc_m)
        tail_ref[...] = u_m[N_META - SUBLANE:, :]
        if mode == "xbc":
            y_m = _causal_conv(u_m, jnp.zeros((SUBLANE, tn), F32), cw, n_tap)
            oside_ref[...] = post(y_m, acc_m)

    @pl.when(m % tiles_per_batch == 0)
    def _():
        prev_ref[...] = tail_ref[...]

    if mode == "xbc":
        ch = acc_ref.shape[1] - SUBLANE
        for c in range(a_ref.shape[0] // ch):
            sl = c % 2
            acc_ref[sl, :SUBLANE, :] = prev_ref[...]
            acc_ref[sl, SUBLANE:, :] = jnp.dot(a_ref[c * ch:(c + 1) * ch, :], wbf_ref[...],
                                               preferred_element_type=F32)
            y = None
            for k in range(n_tap):
                r0 = SUBLANE - (n_tap - 1) + k
                term = acc_ref[sl, r0:r0 + ch, :] * cw[k:k + 1, :]
                y = term if y is None else y + term
            prev_ref[...] = acc_ref[sl, ch:, :]
            o_ref[c * ch:(c + 1) * ch, :] = post(y, None).astype(o_ref.dtype)
    else:
        def finish(rows, acc):
            u = pre(acc)
            y = _causal_conv(u, prev_ref[...], cw, n_tap)
            prev_ref[...] = u[u.shape[0] - SUBLANE:, :]
            o_ref[rows, :] = post(y, acc).astype(o_ref.dtype)
        _chunked_matmul(a_ref, wbf_ref, acc_ref, finish)


def _proj_conv(a, w_nk, row_offs, side, cw, cb, *, width, tm, tn, seq, mode, name):
    m, k = a.shape
    n_w = len(row_offs)
    n_tap = cw.shape[0]
    ms = side.shape[0]

    def w_spec(off):
        return pl.BlockSpec((pl.Element(tn), pl.Element(k)),
                            lambda n, i: (pl.multiple_of(off + n * tn, SUBLANE), 0))

    in_specs = [pl.BlockSpec((tm, k), lambda n, i: (i, 0))]
    in_specs += [w_spec(off) for off in row_offs]
    in_specs += [pl.BlockSpec((ms, k), lambda n, i: (0, 0)),
                 pl.BlockSpec((n_tap, tn), lambda n, i: (0, n))]
    args = [a] + [w_nk] * n_w + [side, cw]
    out_shape = [jax.ShapeDtypeStruct((m, width), F32 if mode == "xbc" else BF16)]
    out_specs = [pl.BlockSpec((tm, tn), lambda n, i: (i, n))]
    if mode == "xbc":
        in_specs.append(pl.BlockSpec((1, tn), lambda n, i: (0, n)))
        args.append(cb)
        out_shape.append(jax.ShapeDtypeStruct((ms, width), F32))
        out_specs.append(pl.BlockSpec((ms, tn), lambda n, i: (0, n)))
    res = pl.pallas_call(
        functools.partial(_proj_conv_body, n_w=n_w, n_tap=n_tap,
                          tiles_per_batch=seq // tm, mode=mode),
        grid=(width // tn, m // tm),
        in_specs=in_specs,
        out_specs=out_specs,
        out_shape=out_shape,
        scratch_shapes=[pltpu.VMEM((k, n_w * tn), BF16),
                        pltpu.VMEM((SUBLANE, tn), F32),
                        pltpu.VMEM((SUBLANE, tn), F32),
                        pltpu.VMEM((2, min(tm, EPILOGUE_ROWS) + SUBLANE * (mode == "xbc"),
                                    n_w * tn), F32)],
        compiler_params=_cparams(2, VMEM_MB + 4 * (n_w > 1)),
        name=name,
    )(*args)
    return res if mode == "xbc" else res[0]


def _ssd_chunk(xbc, dt_raw, dtb, alog, e2, *, heads, meta):
    L = xbc.shape[0]
    di = heads * SSD_HEAD_DIM
    gn = SSD_GROUPS * SSD_STATE
    dt_in = dt_raw[:, :heads] + dtb
    dtv = jnp.maximum(dt_in, 0.0) + jnp.log1p(jnp.exp(-jnp.abs(dt_in)))
    if meta:
        dtv = dtv * (lax.broadcasted_iota(I32, (L, 1), 0) >= L - N_META).astype(F32)
    xs = xbc[:, :di]
    bm = xbc[:, di:di + gn]
    cm = xbc[:, di + gn:]
    a_neg = -jnp.exp(alog)
    ad = dtv * a_neg
    r = lax.broadcasted_iota(I32, (L, L), 0)
    c = lax.broadcasted_iota(I32, (L, L), 1)
    tril = r >= c
    acum = jnp.dot(tril.astype(F32), ad, precision=lax.Precision.HIGHEST,
                   preferred_element_type=F32)
    stack = jnp.concatenate([dtv, dtv * jnp.exp(acum[L - 1:L, :] - acum), jnp.exp(acum)],
                            axis=0)
    hi = stack.astype(BF16)
    lo = (stack - hi.astype(F32)).astype(BF16)
    exp_x = jnp.dot(jnp.concatenate([hi, lo], axis=1), e2, preferred_element_type=F32)
    dt_x, dd_x, ea_x = exp_x[:L], exp_x[L:2 * L], exp_x[2 * L:]
    bt = bm.T.astype(BF16)
    return dict(xs=xs, cm=cm, bt=bt, acum=acum, ea_x=ea_x, x_dt=(xs * dt_x).astype(BF16),
                x_dec=(xs * dd_x).astype(BF16), tril=tril)


def _ssd_state_update(q, st_ref, heads):
    gw = heads * SSD_HEAD_DIM // SSD_GROUPS
    L = q["ea_x"].shape[0]
    for g in range(SSD_GROUPS):
        sl = slice(g * gw, (g + 1) * gw)
        btg = q["bt"][g * SSD_STATE:(g + 1) * SSD_STATE, :]
        new = jnp.dot(btg, q["x_dec"][:, sl], preferred_element_type=F32)
        st_ref[:, sl] = st_ref[:, sl] * q["ea_x"][L - 1:L, sl] + new


def _ssd_body(xbc_ref, dt_ref, z_ref, s0_ref, dtb_ref, alog_ref, dskip_ref, ng_ref, e_ref,
              o_ref, st_ref, *, heads):
    @pl.when(pl.program_id(1) == 0)
    def _():
        st_ref[...] = s0_ref[...]

    L = xbc_ref.shape[0]
    q = _ssd_chunk(xbc_ref[...], dt_ref[...], dtb_ref[...], alog_ref[...], e_ref[...],
                   heads=heads, meta=False)

    gw = heads * SSD_HEAD_DIM // SSD_GROUPS
    j_heads = heads // SSD_GROUPS
    acum = q["acum"]
    acum_t = acum.T
    cmb = q["cm"].astype(BF16)
    x_dt_b = q["x_dt"]
    lane = lax.broadcasted_iota(I32, (L, 2 * SSD_HEAD_DIM), 1)
    lo = lane < SSD_HEAD_DIM
    neg_inf = jnp.float32(-jnp.inf)

    def decay_mat(h):
        col = jnp.broadcast_to(acum[:, h:h + 1], (L, L))
        row = jnp.broadcast_to(acum_t[h:h + 1, :], (L, L))
        return jnp.exp(jnp.where(q["tril"], col - row, neg_inf))

    for g in range(SSD_GROUPS):
        sl = slice(g * gw, (g + 1) * gw)
        cg = cmb[:, g * SSD_STATE:(g + 1) * SSD_STATE]
        btg = q["bt"][g * SSD_STATE:(g + 1) * SSD_STATE, :]
        cb_mat = jnp.dot(cg, btg, preferred_element_type=F32)
        parts = []
        for p in range(j_heads // 2):
            h1 = g * j_heads + 2 * p
            slab = x_dt_b[:, h1 * SSD_HEAD_DIM:(h1 + 2) * SSD_HEAD_DIM]
            lhs = jnp.concatenate([(cb_mat * decay_mat(h1)).astype(BF16),
                                   (cb_mat * decay_mat(h1 + 1)).astype(BF16)], axis=1)
            zero = jnp.zeros_like(slab)
            rhs = jnp.concatenate([jnp.where(lo, slab, zero),
                                   jnp.where(lo, zero, slab)], axis=0)
            parts.append(jnp.dot(lhs, rhs, preferred_element_type=F32))
        y_diag = parts[0] if len(parts) == 1 else jnp.concatenate(parts, axis=1)
        st_g = st_ref[:, sl]
        y_off = jnp.dot(cg, st_g.astype(BF16), preferred_element_type=F32)
        y = y_diag + y_off * q["ea_x"][:, sl] + dskip_ref[:, sl] * q["xs"][:, sl]
        zg = z_ref[:, sl]
        gated = y * (zg * _sigmoid(zg))
        ms = jnp.mean(gated * gated, axis=-1, keepdims=True)
        o_ref[:, sl] = (gated * lax.rsqrt(ms + EPS) * ng_ref[:, sl]).astype(o_ref.dtype)
        new = jnp.dot(btg, q["x_dec"][:, sl], preferred_element_type=F32)
        st_ref[:, sl] = st_g * q["ea_x"][L - 1:L, sl] + new


def _ssd_meta_body(xbc_ref, dt_ref, dtb_ref, alog_ref, e_ref, st_ref, *, heads):
    st_ref[...] = jnp.zeros_like(st_ref)
    q = _ssd_chunk(xbc_ref[...], dt_ref[...], dtb_ref[...], alog_ref[...], e_ref[...],
                   heads=heads, meta=True)
    _ssd_state_update(q, st_ref, heads)


def _full_spec(shape):
    nd = len(shape)
    return pl.BlockSpec(shape, lambda *_: (0,) * nd)


def _ssd_meta(xbc_m, dt_m, dtb, alog, e2, heads):
    di = heads * SSD_HEAD_DIM
    args = [xbc_m, dt_m, dtb, alog, e2]
    return pl.pallas_call(
        functools.partial(_ssd_meta_body, heads=heads),
        grid=(1,),
        in_specs=[_full_spec(a.shape) for a in args],
        out_specs=_full_spec((SSD_STATE, di)),
        out_shape=jax.ShapeDtypeStruct((SSD_STATE, di), F32),
        compiler_params=_cparams(1),
        name="ssd_meta",
    )(*args)


def _ssd_main(xbc, dt, z, s0, dtb, alog, dskip_x, ng, e2, *, heads, batch):
    t, cd = xbc.shape
    di = heads * SSD_HEAD_DIM
    L = SSD_CHUNK
    nc = t // batch // L
    row = lambda b, c: (b * nc + c, 0)
    in_specs = [pl.BlockSpec((L, cd), row),
                pl.BlockSpec((L, LANE), row),
                pl.BlockSpec((L, di), row)]
    consts = [s0, dtb, alog, dskip_x, ng, e2]
    in_specs += [_full_spec(a.shape) for a in consts]
    return pl.pallas_call(
        functools.partial(_ssd_body, heads=heads),
        grid=(batch, nc),
        in_specs=in_specs,
        out_specs=pl.BlockSpec((L, di), row),
        out_shape=jax.ShapeDtypeStruct((t, di), BF16),
        scratch_shapes=[pltpu.VMEM((SSD_STATE, di), F32)],
        compiler_params=_cparams(2),
        name="ssd_main",
    )(xbc, dt, z, *consts)


def _router_body(x_ref, g_ref, wr_ref, br_ref, hp_ref, idx_ref, prob_ref, rank_ref,
                 cnt_ref, base_ref):
    @pl.when(pl.program_id(0) == 0)
    def _():
        base_ref[...] = jnp.zeros_like(base_ref)

    x = x_ref[...]
    tm, d = x.shape
    ne = wr_ref.shape[0]
    ms = jnp.mean(x * x, axis=-1, keepdims=True)
    h = x * lax.rsqrt(ms + EPS) * g_ref[...]
    half = d // 2
    lo_bits = lax.bitcast_convert_type(h[:, :half].astype(BF16).astype(F32), U32)
    hi_bits = lax.bitcast_convert_type(h[:, half:].astype(BF16).astype(F32), U32)
    hp_ref[...] = (lo_bits >> 16) | (hi_bits & jnp.uint32(0xFFFF0000))

    logits = lax.dot_general(wr_ref[...], h, (((1,), (1,)), ((), ())),
                             precision=lax.Precision.HIGHEST,
                             preferred_element_type=F32) + br_ref[:, 0:1]
    eidx = lax.broadcasted_iota(I32, (ne, tm), 0)
    work = logits
    vals, onehots = [], []
    for k in range(TOP_K):
        mx = jnp.max(work, axis=0, keepdims=True)
        sel = jnp.min(jnp.where(work == mx, eidx, ne), axis=0, keepdims=True)
        oh = eidx == sel
        idx_ref[k:k + 1, :] = sel
        vals.append(mx)
        onehots.append(oh)
        work = jnp.where(oh, -jnp.inf, work)
    exps = [jnp.exp(v - vals[0]) for v in vals]
    denom = exps[0]
    for e in exps[1:]:
        denom = denom + e
    for k in range(TOP_K):
        prob_ref[k:k + 1, :] = exps[k] / denom

    cnt = onehots[0].astype(F32)
    for oh in onehots[1:]:
        cnt = cnt + oh.astype(F32)
    r = lax.broadcasted_iota(I32, (tm, tm), 0)
    c = lax.broadcasted_iota(I32, (tm, tm), 1)
    before = (r < c).astype(BF16)
    prefix = jnp.dot(cnt.astype(BF16), before, preferred_element_type=F32)
    prefix = prefix + base_ref[:, 0:1]
    for k in range(TOP_K):
        rk = jnp.sum(jnp.where(onehots[k], prefix, 0.0), axis=0, keepdims=True)
        rank_ref[k:k + 1, :] = rk.astype(I32)
    base_ref[...] = base_ref[...] + jnp.sum(cnt, axis=1, keepdims=True)
    cnt_ref[...] = base_ref[...]


def _router(hres, g, wr_t, br, tm):
    t, d = hres.shape
    ne = wr_t.shape[0]
    tok = lambda i: (0, i)
    return pl.pallas_call(
        _router_body,
        grid=(t // tm,),
        in_specs=[pl.BlockSpec((tm, d), lambda i: (i, 0)),
                  _full_spec((1, d)), _full_spec((ne, d)), _full_spec((ne, LANE))],
        out_specs=[pl.BlockSpec((tm, d // 2), lambda i: (i, 0)),
                   pl.BlockSpec((TOP_K, tm), tok),
                   pl.BlockSpec((TOP_K, tm), tok),
                   pl.BlockSpec((TOP_K, tm), tok),
                   _full_spec((ne, LANE))],
        out_shape=[jax.ShapeDtypeStruct((t, d // 2), U32),
                   jax.ShapeDtypeStruct((TOP_K, t), I32),
                   jax.ShapeDtypeStruct((TOP_K, t), F32),
                   jax.ShapeDtypeStruct((TOP_K, t), I32),
                   jax.ShapeDtypeStruct((ne, LANE), F32)],
        scratch_shapes=[pltpu.VMEM((ne, LANE), F32)],
        compiler_params=_cparams(1),
        name="router",
    )(hres, g.reshape(1, d), wr_t, jnp.broadcast_to(br.reshape(ne, 1), (ne, LANE)))


def _dispatch_body(dest_ref, cnt_ref, seg_ref, tot_ref, hp_ref, xs_ref, zrow_ref, zblk_ref,
                   row_sem, pad_sem, *, t_total, ne, n_blocks):
    i = pl.program_id(0)
    tm = hp_ref.shape[0] * SUBLANE

    def pad_copy(slot):
        return pltpu.make_async_copy(zrow_ref.at[pl.ds(0, 1)], xs_ref.at[pl.ds(slot, 1)], pad_sem)

    def tail_copy(b):
        return pltpu.make_async_copy(
            zblk_ref, xs_ref.at[pl.ds(pl.multiple_of(b * MOE_BLK, MOE_BLK), MOE_BLK)], pad_sem)

    @pl.when(i == 0)
    def _():
        zrow_ref[...] = jnp.zeros_like(zrow_ref)
        zblk_ref[...] = jnp.zeros_like(zblk_ref)

        def expert(e, carry):
            used = cnt_ref[e]
            end = (used + MOE_BLK - 1) // MOE_BLK * MOE_BLK

            def start(r, c):
                pad_copy(seg_ref[e] + r).start()
                return c
            lax.fori_loop(used, end, start, 0)

            def wait(r, c):
                pad_copy(seg_ref[e] + r).wait()
                return c
            lax.fori_loop(used, end, wait, 0)
            return carry
        lax.fori_loop(0, ne, expert, 0)

        def tail_start(b, c):
            tail_copy(b).start()
            return c
        lax.fori_loop(tot_ref[0], n_blocks, tail_start, 0)

        def tail_wait(b, c):
            tail_copy(b).wait()
            return c
        lax.fori_loop(tot_ref[0], n_blocks, tail_wait, 0)

    def row_copy(g, u, slot):
        return pltpu.make_async_copy(hp_ref.at[g, pl.ds(u, 1)], xs_ref.at[pl.ds(slot, 1)],
                                     row_sem)

    def issue(g, c):
        for k in range(TOP_K):
            base = k * t_total + i * tm + g * SUBLANE
            for u in range(SUBLANE):
                row_copy(g, u, dest_ref[base + u]).start()
        return c
    lax.fori_loop(0, tm // SUBLANE, issue, 0, unroll=4)

    def drain(g, c):
        for _ in range(TOP_K * SUBLANE):
            row_copy(0, 0, 0).wait()
        return c
    lax.fori_loop(0, tm // SUBLANE, drain, 0)


def _moe_dispatch(dest_flat, cnt, seg_start, total_blk, hp, n_rows, *, tm):
    t, w = hp.shape
    ne = cnt.shape[0]
    return pl.pallas_call(
        functools.partial(_dispatch_body, t_total=t, ne=ne, n_blocks=n_rows // MOE_BLK),
        grid_spec=pltpu.PrefetchScalarGridSpec(
            num_scalar_prefetch=4,
            grid=(t // tm,),
            in_specs=[pl.BlockSpec((tm // SUBLANE, SUBLANE, w), lambda i, *_: (i, 0, 0))],
            out_specs=pl.BlockSpec(memory_space=pl.ANY),
            scratch_shapes=[pltpu.VMEM((SUBLANE, w), U32),
                            pltpu.VMEM((MOE_BLK, w), U32),
                            pltpu.SemaphoreType.DMA,
                            pltpu.SemaphoreType.DMA],
        ),
        out_shape=jax.ShapeDtypeStruct((n_rows, w), U32),
        compiler_params=_cparams(1),
        name="moe_dispatch",
    )(dest_flat, cnt, seg_start, total_blk, hp.reshape(t // SUBLANE, SUBLANE, w))


def _sb_specs(rows, width_in, n_tiles):
    def x_map(s, j, e, start, nblk, tot):
        return (pl.multiple_of(start[s], MOE_BLK), 0)

    def wcol(s, j, nblk):
        return jnp.where(nblk[s] > 0, j, n_tiles - 1)

    return pl.BlockSpec((pl.Element(rows), pl.Element(width_in)), x_map), wcol


def _sb_run(start_ref, nblk_ref, tot_ref, o_ref, stage_ref, zblk_ref, out_sems, zero_sem,
            compute_rows, *, n_blocks):
    s = pl.program_id(0)
    j = pl.program_id(1)
    nj = pl.num_programs(1)
    q = s * nj + j
    slot = q % 2
    tile = stage_ref.shape[2]
    col = pl.multiple_of(j * tile, LANE)
    nb = nblk_ref[s]

    def out_copy(sl, b, row0):
        r0 = pl.multiple_of(b * MOE_BLK, MOE_BLK)
        return pltpu.make_async_copy(
            stage_ref.at[sl, pl.ds(r0, MOE_BLK)],
            o_ref.at[pl.ds(pl.multiple_of(row0 + r0, MOE_BLK), MOE_BLK), pl.ds(col, tile)],
            out_sems.at[sl])

    def drain(sl, count):
        def one(b, c):
            out_copy(sl, 0, 0).wait()
            return c
        lax.fori_loop(0, count, one, 0)

    def zero_copy(b):
        return pltpu.make_async_copy(
            zblk_ref,
            o_ref.at[pl.ds(pl.multiple_of(b * MOE_BLK, MOE_BLK), MOE_BLK), pl.ds(col, tile)],
            zero_sem)

    @pl.when(q >= 2)
    def _():
        drain(slot, nblk_ref[(q - 2) // nj])

    @pl.when(s == 0)
    def _():
        zblk_ref[...] = jnp.zeros_like(zblk_ref)

        def zs(b, c):
            zero_copy(b).start()
            return c
        lax.fori_loop(tot_ref[0], n_blocks, zs, 0)

        def zw(b, c):
            zero_copy(b).wait()
            return c
        lax.fori_loop(tot_ref[0], n_blocks, zw, 0)

    def run(b0, k):
        compute_rows(slot, pl.multiple_of(b0 * MOE_BLK, MOE_BLK), k * MOE_BLK)
        for i in range(k):
            out_copy(slot, b0 + i, start_ref[s]).start()

    k = 1 << (MOE_SB_BLKS.bit_length() - 1)
    while k >= 1:
        @pl.when(nb & k != 0)
        def _(k=k):
            run(nb & ~(2 * k - 1), k)
        k //= 2

    @pl.when(q == pl.num_programs(0) * nj - 1)
    def _():
        drain(slot, nb)

        @pl.when(q >= 1)
        def _():
            drain(1 - slot, nblk_ref[(q - 1) // nj])


def _gate_up_body(e_ref, start_ref, nblk_ref, tot_ref, x_ref, wg_ref, wu_ref, bg_ref, bu_ref,
                  o_ref, wbf_ref, stage_ref, zblk_ref, out_sems, zero_sem, *, n_blocks):
    half = x_ref.shape[1]
    tj = wg_ref.shape[1]

    @pl.when(nblk_ref[pl.program_id(0)] > 0)
    def _():
        wbf_ref[:, :tj] = wg_ref[...].astype(BF16)
        wbf_ref[:, tj:] = wu_ref[...].astype(BF16)

    bias = jnp.concatenate([bg_ref[...], bu_ref[...]], axis=1)

    def compute_rows(slot, r0, n_rows):
        xp = x_ref[pl.ds(r0, n_rows), :]
        x_lo = lax.bitcast_convert_type(xp << 16, F32).astype(BF16)
        x_hi = lax.bitcast_convert_type(xp & jnp.uint32(0xFFFF0000), F32).astype(BF16)
        gu = (jnp.dot(x_lo, wbf_ref[:half, :], preferred_element_type=F32)
              + jnp.dot(x_hi, wbf_ref[half:, :], preferred_element_type=F32) + bias)
        gate = jnp.minimum(gu[:, :tj], SWIGLU_LIMIT)
        up = jnp.clip(gu[:, tj:], -SWIGLU_LIMIT, SWIGLU_LIMIT)
        act = (up + 1.0) * (gate * _sigmoid(gate * SWIGLU_ALPHA))
        stage_ref[slot, pl.ds(r0, n_rows), :] = act.astype(stage_ref.dtype)

    _sb_run(start_ref, nblk_ref, tot_ref, o_ref, stage_ref, zblk_ref, out_sems, zero_sem,
            compute_rows, n_blocks=n_blocks)


def _sb_scratch(rows, tile, dtype):
    return [pltpu.VMEM((2, rows, tile), dtype), pltpu.VMEM((MOE_BLK, tile), dtype),
            pltpu.SemaphoreType.DMA((2,)), pltpu.SemaphoreType.DMA]


def _moe_gate_up(sb, n_sb, xs, w_gu, b_gu, *, tj):
    ne, d, de2 = w_gu.shape
    de = de2 // 2
    nj = de // tj
    rows = MOE_SB_BLKS * MOE_BLK
    n_rows, half = xs.shape
    x_spec, wcol = _sb_specs(rows, half, nj)
    in_specs = [
        x_spec,
        pl.BlockSpec((None, d, tj), lambda s, j, e, st, nb, tt: (e[s], 0, wcol(s, j, nb))),
        pl.BlockSpec((None, d, tj), lambda s, j, e, st, nb, tt: (e[s], 0, nj + wcol(s, j, nb))),
        pl.BlockSpec((None, 1, tj), lambda s, j, e, st, nb, tt: (e[s], 0, wcol(s, j, nb))),
        pl.BlockSpec((None, 1, tj), lambda s, j, e, st, nb, tt: (e[s], 0, nj + wcol(s, j, nb))),
    ]
    return pl.pallas_call(
        functools.partial(_gate_up_body, n_blocks=n_rows // MOE_BLK),
        grid_spec=pltpu.PrefetchScalarGridSpec(
            num_scalar_prefetch=4,
            grid=(n_sb, nj),
            in_specs=in_specs,
            out_specs=pl.BlockSpec(memory_space=pl.ANY),
            scratch_shapes=[pltpu.VMEM((d, 2 * tj), BF16)] + _sb_scratch(rows, tj, BF16),
        ),
        out_shape=jax.ShapeDtypeStruct((n_rows, de), BF16),
        compiler_params=_cparams(2),
        name="moe_gate_up",
    )(*sb, xs, w_gu, w_gu, b_gu.reshape(ne, 1, de2), b_gu.reshape(ne, 1, de2))


def _down_body(e_ref, start_ref, nblk_ref, tot_ref, a_ref, w_ref, b_ref, o_ref, wbf_ref,
               stage_ref, zblk_ref, out_sems, zero_sem, *, n_blocks):
    @pl.when(nblk_ref[pl.program_id(0)] > 0)
    def _():
        wbf_ref[...] = w_ref[...].astype(BF16)

    bias = b_ref[...]

    def compute_rows(slot, r0, n_rows):
        stage_ref[slot, pl.ds(r0, n_rows), :] = jnp.dot(
            a_ref[pl.ds(r0, n_rows), :], wbf_ref[...], preferred_element_type=F32) + bias

    _sb_run(start_ref, nblk_ref, tot_ref, o_ref, stage_ref, zblk_ref, out_sems, zero_sem,
            compute_rows, n_blocks=n_blocks)


def _moe_down(sb, n_sb, act, w_down, b_down, *, tn):
    ne, de, d = w_down.shape
    nn = d // tn
    rows = MOE_SB_BLKS * MOE_BLK
    n_rows = act.shape[0]
    a_spec, wcol = _sb_specs(rows, de, nn)
    in_specs = [
        a_spec,
        pl.BlockSpec((None, de, tn), lambda s, j, e, st, nb, tt: (e[s], 0, wcol(s, j, nb))),
        pl.BlockSpec((None, 1, tn), lambda s, j, e, st, nb, tt: (e[s], 0, wcol(s, j, nb))),
    ]
    return pl.pallas_call(
        functools.partial(_down_body, n_blocks=n_rows // MOE_BLK),
        grid_spec=pltpu.PrefetchScalarGridSpec(
            num_scalar_prefetch=4,
            grid=(n_sb, nn),
            in_specs=in_specs,
            out_specs=pl.BlockSpec(memory_space=pl.ANY),
            scratch_shapes=[pltpu.VMEM((de, tn), BF16)] + _sb_scratch(rows, tn, F32),
        ),
        out_shape=jax.ShapeDtypeStruct((n_rows, d), F32),
        compiler_params=_cparams(2),
        name="moe_down",
    )(*sb, act, w_down, b_down.reshape(ne, 1, d))


def _moe_schedule(cnt, n_pairs, ne):
    n_blocks = n_pairs // MOE_BLK + ne + MOE_SB_BLKS - 1
    s_max = -(-(n_pairs // MOE_BLK + ne) // MOE_SB_BLKS) + ne
    pblk = (cnt + MOE_BLK - 1) // MOE_BLK
    seg_end_blk = jnp.cumsum(pblk)
    seg_start = (seg_end_blk - pblk) * MOE_BLK
    total_blk = seg_end_blk[-1:]
    nsb_e = (pblk + MOE_SB_BLKS - 1) // MOE_SB_BLKS
    sb_end = jnp.cumsum(nsb_e)
    sb_first = sb_end - nsb_e
    n_sb = sb_end[-1]
    s_ids = jnp.arange(s_max, dtype=I32)
    used = s_ids < n_sb
    e_of_s = jnp.minimum(jnp.searchsorted(sb_end, s_ids, side="right"), ne - 1).astype(I32)
    i_in_e = s_ids - sb_first[e_of_s]
    start = seg_start[e_of_s] + i_in_e * (MOE_SB_BLKS * MOE_BLK)
    nblk = jnp.where(used, jnp.clip(pblk[e_of_s] - MOE_SB_BLKS * i_in_e, 0, MOE_SB_BLKS), 0)
    last = jnp.maximum(n_sb - 1, 0)
    sb = (jnp.where(used, e_of_s, e_of_s[last]).astype(I32),
          jnp.where(used, start, start[last]).astype(I32), nblk.astype(I32),
          total_blk.astype(I32))
    return sb, n_sb.astype(I32), seg_start.astype(I32), n_blocks * MOE_BLK


def _combine_body(dest_ref, y_ref, h_ref, p_ref, g_ref, o_ref, ybuf, sems, *, t_total):
    i = pl.program_id(0)
    n = pl.num_programs(0)
    tc, d = h_ref.shape
    groups = tc // SUBLANE

    def row_copy(slot, k, g, u, row):
        return pltpu.make_async_copy(y_ref.at[pl.ds(row, 1)], ybuf.at[slot, k, g, pl.ds(u, 1)],
                                     sems.at[slot])

    def issue(tile, slot):
        def one(g, carry):
            for k in range(TOP_K):
                base = k * t_total + tile * tc + g * SUBLANE
                for u in range(SUBLANE):
                    row_copy(slot, k, g, u, dest_ref[base + u]).start()
            return carry
        lax.fori_loop(0, groups, one, 0, unroll=4)

    @pl.when(i == 0)
    def _():
        issue(0, 0)

    slot = i % 2

    def drain(g, carry):
        for _ in range(TOP_K * SUBLANE):
            row_copy(slot, 0, 0, 0, 0).wait()
        return carry
    lax.fori_loop(0, groups, drain, 0)

    @pl.when(i + 1 < n)
    def _():
        issue(i + 1, 1 - slot)

    p = p_ref[...]
    cw = min(d, 4 * LANE)
    ssq = jnp.zeros((tc, 1), F32)
    for c in range(d // cw):
        cols = slice(c * cw, (c + 1) * cw)
        acc = h_ref[:, cols]
        for k in range(TOP_K):
            acc = acc + p[:, k:k + 1] * ybuf[slot, k, :, :, cols].reshape(tc, cw)
        o_ref[:, cols] = acc
        ssq = ssq + jnp.sum(acc * acc, axis=-1, keepdims=True)
    scale = lax.rsqrt(ssq * (1.0 / d) + EPS)
    for c in range(d // cw):
        cols = slice(c * cw, (c + 1) * cw)
        o_ref[:, cols] = o_ref[:, cols] * scale * g_ref[:, cols]


def _combine(dest_flat, y_sorted, hres, p_t, final_g, *, tc):
    t, d = hres.shape
    return pl.pallas_call(
        functools.partial(_combine_body, t_total=t),
        grid_spec=pltpu.PrefetchScalarGridSpec(
            num_scalar_prefetch=1,
            grid=(t // tc,),
            in_specs=[pl.BlockSpec(memory_space=pl.ANY),
                      pl.BlockSpec((tc, d), lambda i, *_: (i, 0)),
                      pl.BlockSpec((tc, TOP_K), lambda i, *_: (i, 0)),
                      pl.BlockSpec((1, d), lambda i, *_: (0, 0))],
            out_specs=pl.BlockSpec((tc, d), lambda i, *_: (i, 0)),
            scratch_shapes=[pltpu.VMEM((2, TOP_K, tc // SUBLANE, SUBLANE, d), F32),
                            pltpu.SemaphoreType.DMA((2,))],
        ),
        out_shape=jax.ShapeDtypeStruct((t, d), F32),
        compiler_params=_cparams(1),
        name="moe_combine",
    )(dest_flat, y_sorted, hres, p_t, final_g.reshape(1, d))


def _pick(n, prefs):
    for p in prefs:
        if n % p == 0:
            return p
    raise ValueError(f"no tile in {prefs} divides {n}")


def kernel(x, meta_tokens, norm_mix_g, w_in, ssd_conv_w, ssd_conv_b, dt_bias, a_log, d_skip,
           ssd_norm_g, w_ssd_out, sc_conv_w, w_sc_out, w_o, norm_ffn_g, w_router, b_router,
           w_gate_up, b_gate_up, w_down, b_down, final_g):
    batch, seq, d = x.shape
    assert norm_mix_g.shape[0] == 1, "meta-token handling assumes a single layer"
    assert seq % SSD_CHUNK == 0 and N_META <= SSD_CHUNK
    t = batch * seq
    heads = d // SSD_HEAD_DIM
    gn = SSD_GROUPS * SSD_STATE
    conv_dim = d + 2 * gn
    xbc_off = d
    dt_off = xbc_off + conv_dim
    sc_off = dt_off + heads
    gate_off = sc_off + 3 * d
    ne = w_router.shape[-1]
    de = w_down.shape[2]

    x2 = x.reshape(t, d)
    w_in_nk = jnp.swapaxes(w_in[0], 0, 1)

    tm = _pick(seq, (1024, 512, 256, 128))
    tn = _pick(d, (512, 256, 128))

    h = _rmsnorm_bf16(x2, norm_mix_g[0], _pick(t, (256, 128)))
    h_meta = _rmsnorm_bf16(meta_tokens, norm_mix_g[0], N_META)

    proj = functools.partial(_matmul, h, w_in_nk, tm=tm, out_dtype=F32, w_is_nk=True)
    z = proj(0, d, tn=tn, name="proj_z")
    xbc, xbc_meta = _proj_conv(h, w_in_nk, [xbc_off], h_meta, ssd_conv_w[0],
                               ssd_conv_b[0].reshape(1, conv_dim), width=conv_dim, tm=tm,
                               tn=tn, seq=seq, mode="xbc", name="proj_xbc")
    dtr, dt_meta = proj(dt_off, LANE, tn=LANE, side=h_meta, name="proj_dt")
    u_sc = _proj_conv(h, w_in_nk, [sc_off, sc_off + d, sc_off + 2 * d], h_meta,
                      sc_conv_w[0], None, width=d, tm=tm, tn=_pick(d, (256, 128)), seq=seq,
                      mode="sc", name="proj_sc")
    gates = proj(gate_off, 2 * d, tn=tn, name="proj_gate")

    e_mat = (jnp.arange(d, dtype=I32)[None, :] // SSD_HEAD_DIM
             == jnp.arange(heads, dtype=I32)[:, None]).astype(BF16)
    e2 = jnp.concatenate([e_mat, e_mat], axis=0)
    pad = SSD_CHUNK - N_META
    dtb = dt_bias[0].reshape(1, heads)
    alog = a_log[0].reshape(1, heads)
    s0 = _ssd_meta(jnp.pad(xbc_meta, ((pad, 0), (0, 0))),
                   jnp.pad(dt_meta, ((pad, 0), (0, 0))), dtb, alog, e2, heads)
    dskip_x = jnp.repeat(d_skip[0], SSD_HEAD_DIM).reshape(1, d)
    g_ssd = _ssd_main(xbc, dtr, z, s0, dtb, alog, dskip_x, ssd_norm_g[0].reshape(1, d), e2,
                      heads=heads, batch=batch)

    y_a = _matmul(g_ssd, w_ssd_out[0], 0, d, tm=tm, tn=tn, out_dtype=F32,
                  extras=[(gates, 0)],
                  epilogue=lambda acc, ga: _sigmoid(ga) * acc, name="ssd_out")
    mixed = _matmul(u_sc, w_sc_out[0], 0, d, tm=tm, tn=tn, out_dtype=BF16,
                    extras=[(gates, d), (y_a, 0)],
                    epilogue=lambda acc, gb, ya: ya + _sigmoid(gb) * acc, name="sc_out")
    hres = _matmul(mixed, w_o[0], 0, d, tm=tm, tn=tn, out_dtype=F32,
                   extras=[(x2, 0)], epilogue=lambda acc, xr: xr + acc, name="w_o")

    rt = _pick(t, (512, 256, 128))
    hp, top_idx, probs, rank, counts = _router(hres, norm_ffn_g[0], w_router[0].T,
                                               b_router[0], rt)
    cnt = counts[:, 0].astype(I32)
    sb, n_sb, seg_start, n_slots = _moe_schedule(cnt, t * TOP_K, ne)
    e_ids = jnp.arange(ne, dtype=I32)[:, None, None]
    dest = rank + jnp.sum(jnp.where(top_idx[None] == e_ids, seg_start[:, None, None], 0),
                          axis=0)
    dest_flat = dest.reshape(-1)

    xs = _moe_dispatch(dest_flat, cnt, seg_start, sb[3], hp, n_slots,
                       tm=_pick(t, (512, 256, 128)))
    act = _moe_gate_up(sb, n_sb, xs, w_gate_up[0], b_gate_up[0], tj=_pick(de, (256, 128)))
    y_sorted = _moe_down(sb, n_sb, act, w_down[0], b_down[0],
                         tn=_pick(d, (1024, 512, 256, 128)))
    out = _combine(dest_flat, y_sorted, hres, probs.T, final_g, tc=_pick(t, (128,)))
    return out.reshape(batch, seq, d)
```

```python
import functools

import jax
import jax.numpy as jnp
from jax import lax
from jax.experimental import pallas as pl
from jax.experimental.pallas import tpu as pltpu

F32 = jnp.float32
BF16 = jnp.bfloat16
I32 = jnp.int32
U32 = jnp.uint32

N_META = 16
SSD_HEAD_DIM = 64
SSD_GROUPS = 8
SSD_STATE = 128
SSD_CONV = 4
SSD_CHUNK = 128
SC_CONV = 3
TOP_K = 4
SWIGLU_LIMIT = 7.0
SWIGLU_ALPHA = 1.702
EPS = 1e-5

LANE = 128
SUBLANE = 8
MOE_BLK = 128
MOE_SB_BLKS = 10
EPILOGUE_ROWS = 256
COMBINE_BUFS = 3
VMEM_MB = 56


def _cparams(n_axes, vmem_mb=VMEM_MB):
    return pltpu.CompilerParams(
        dimension_semantics=("arbitrary",) * n_axes,
        vmem_limit_bytes=vmem_mb * 1024 * 1024,
    )


def _sigmoid(v):
    return 1.0 / (1.0 + jnp.exp(-v))


def _rms_body(x_ref, g_ref, o_ref):
    x = x_ref[...]
    ms = jnp.mean(x * x, axis=-1, keepdims=True)
    o_ref[...] = (x * lax.rsqrt(ms + EPS) * g_ref[...]).astype(o_ref.dtype)


def _rmsnorm_bf16(x2, g, tm):
    m, d = x2.shape
    return pl.pallas_call(
        _rms_body,
        grid=(m // tm,),
        in_specs=[pl.BlockSpec((tm, d), lambda i: (i, 0)),
                  pl.BlockSpec((1, d), lambda i: (0, 0))],
        out_specs=pl.BlockSpec((tm, d), lambda i: (i, 0)),
        out_shape=jax.ShapeDtypeStruct((m, d), BF16),
        compiler_params=_cparams(1),
        name="rmsnorm",
    )(x2, g.reshape(1, d))


def _chunked_matmul(a_ref, wbf_ref, acc_ref, finish):
    tm = a_ref.shape[0]
    ch = acc_ref.shape[1]
    n = tm // ch

    def rows(c):
        return slice(c * ch, (c + 1) * ch)

    for c in range(n):
        if c > 0:
            finish(rows(c - 1), acc_ref[(c - 1) % 2])
        acc_ref[c % 2] = jnp.dot(a_ref[rows(c), :], wbf_ref[...], preferred_element_type=F32)
    finish(rows(n - 1), acc_ref[(n - 1) % 2])


def _mm_body(*refs, n_extra, has_side, epilogue, w_is_nk):
    it = iter(refs)
    a_ref = next(it)
    w_ref = next(it)
    side_ref = next(it) if has_side else None
    extras = [next(it) for _ in range(n_extra)]
    o_ref = next(it)
    oside_ref = next(it) if has_side else None
    wbf_ref = next(it)
    acc_ref = next(it) if epilogue is not None else None

    @pl.when(pl.program_id(1) == 0)
    def _():
        w = w_ref[...]
        wbf_ref[...] = (w.T if w_is_nk else w).astype(BF16)
        if has_side:
            oside_ref[...] = jnp.dot(side_ref[...], wbf_ref[...],
                                     preferred_element_type=F32)

    if epilogue is None:
        o_ref[...] = jnp.dot(a_ref[...], wbf_ref[...],
                             preferred_element_type=F32).astype(o_ref.dtype)
    else:
        def finish(rows, acc):
            o_ref[rows, :] = epilogue(acc, *[e[rows, :] for e in extras]).astype(o_ref.dtype)
        _chunked_matmul(a_ref, wbf_ref, acc_ref, finish)


def _matmul(a, w, col_off, n_cols, *, tm, tn, out_dtype, side=None, extras=(),
            epilogue=None, w_is_nk=False, name="matmul"):
    m, k = a.shape
    assert m % tm == 0 and n_cols % tn == 0
    if w_is_nk:
        assert col_off % SUBLANE == 0
        w_spec = pl.BlockSpec((pl.Element(tn), pl.Element(k)),
                              lambda n, i: (pl.multiple_of(col_off + n * tn, SUBLANE), 0))
    else:
        assert col_off % tn == 0
        w_spec = pl.BlockSpec((k, tn), lambda n, i, cb=col_off // tn: (0, n + cb))
    in_specs = [pl.BlockSpec((tm, k), lambda n, i: (i, 0)), w_spec]
    args = [a, w]
    if side is not None:
        in_specs.append(pl.BlockSpec((side.shape[0], k), lambda n, i: (0, 0)))
        args.append(side)
    for arr, off in extras:
        assert off % tn == 0
        in_specs.append(pl.BlockSpec((tm, tn), lambda n, i, o=off // tn: (i, n + o)))
        args.append(arr)
    out_shape = [jax.ShapeDtypeStruct((m, n_cols), out_dtype)]
    out_specs = [pl.BlockSpec((tm, tn), lambda n, i: (i, n))]
    if side is not None:
        out_shape.append(jax.ShapeDtypeStruct((side.shape[0], n_cols), F32))
        out_specs.append(pl.BlockSpec((side.shape[0], tn), lambda n, i: (0, n)))
    res = pl.pallas_call(
        functools.partial(_mm_body, n_extra=len(extras), has_side=side is not None,
                          epilogue=epilogue, w_is_nk=w_is_nk),
        grid=(n_cols // tn, m // tm),
        in_specs=in_specs,
        out_specs=out_specs,
        out_shape=out_shape,
        scratch_shapes=[pltpu.VMEM((k, tn), BF16)]
        + ([pltpu.VMEM((2, min(tm, EPILOGUE_ROWS), tn), F32)] if epilogue is not None else []),
        compiler_params=_cparams(2),
        name=name,
    )(*args)
    return res if side is not None else res[0]


def _causal_conv(raw, prev, w, n_tap):
    acc = raw * w[n_tap - 1:n_tap, :]
    row8 = lax.broadcasted_iota(I32, (SUBLANE, raw.shape[1]), 0)
    for s in range(1, n_tap):
        sh = pltpu.roll(raw, s, 0)
        shp = pltpu.roll(prev, s, 0)
        first = jnp.where(row8 < s, shp, sh[:SUBLANE])
        sh = jnp.concatenate([first, sh[SUBLANE:]], axis=0)
        acc = acc + sh * w[n_tap - 1 - s:n_tap - s, :]
    return acc


def _proj_conv_body(*refs, n_w, n_tap, tiles_per_batch, mode):
    it = iter(refs)
    a_ref = next(it)
    w_refs = [next(it) for _ in range(n_w)]
    side_ref = next(it)
    cw_ref = next(it)
    cb_ref = next(it) if mode == "xbc" else None
    o_ref = next(it)
    oside_ref = next(it) if mode == "xbc" else None
    wbf_ref = next(it)
    prev_ref = next(it)
    tail_ref = next(it)
    acc_ref = next(it)
    tn = o_ref.shape[1]
    cw = cw_ref[...]

    def pre(acc):
        return acc if mode == "xbc" else acc[:, tn:2 * tn] * acc[:, 2 * tn:]

    def post(y, acc):
        if mode == "xbc":
            v = y + cb_ref[...]
            return v * _sigmoid(v)
        return acc[:, :tn] * y

    m = pl.program_id(1)

    @pl.when(m == 0)
    def _():
        for i, w_ref in enumerate(w_refs):
            wbf_ref[:, i * tn:(i + 1) * tn] = w_ref[...].T.astype(BF16)
        acc_m = jnp.dot(side_ref[...], wbf_ref[...], preferred_element_type=F32)
        u_m = pre(acc_m)
        tail_ref[...] = u_m[N_META - SUBLANE:, :]
        if mode == "xbc":
            y_m = _causal_conv(u_m, jnp.zeros((SUBLANE, tn), F32), cw, n_tap)
            oside_ref[...] = post(y_m, acc_m)

    @pl.when(m % tiles_per_batch == 0)
    def _():
        prev_ref[...] = tail_ref[...]

    if mode == "xbc":
        ch = acc_ref.shape[1] - SUBLANE
        for c in range(a_ref.shape[0] // ch):
            sl = c % 2
            acc_ref[sl, :SUBLANE, :] = prev_ref[...]
            acc_ref[sl, SUBLANE:, :] = jnp.dot(a_ref[c * ch:(c + 1) * ch, :], wbf_ref[...],
                                               preferred_element_type=F32)
            y = None
            for k in range(n_tap):
                r0 = SUBLANE - (n_tap - 1) + k
                term = acc_ref[sl, r0:r0 + ch, :] * cw[k:k + 1, :]
                y = term if y is None else y + term
            prev_ref[...] = acc_ref[sl, ch:, :]
            o_ref[c * ch:(c + 1) * ch, :] = post(y, None).astype(o_ref.dtype)
    else:
        def finish(rows, acc):
            u = pre(acc)
            y = _causal_conv(u, prev_ref[...], cw, n_tap)
            prev_ref[...] = u[u.shape[0] - SUBLANE:, :]
            o_ref[rows, :] = post(y, acc).astype(o_ref.dtype)
        _chunked_matmul(a_ref, wbf_ref, acc_ref, finish)


def _proj_conv(a, w_nk, row_offs, side, cw, cb, *, width, tm, tn, seq, mode, name):
    m, k = a.shape
    n_w = len(row_offs)
    n_tap = cw.shape[0]
    ms = side.shape[0]

    def w_spec(off):
        return pl.BlockSpec((pl.Element(tn), pl.Element(k)),
                            lambda n, i: (pl.multiple_of(off + n * tn, SUBLANE), 0))

    in_specs = [pl.BlockSpec((tm, k), lambda n, i: (i, 0))]
    in_specs += [w_spec(off) for off in row_offs]
    in_specs += [pl.BlockSpec((ms, k), lambda n, i: (0, 0)),
                 pl.BlockSpec((n_tap, tn), lambda n, i: (0, n))]
    args = [a] + [w_nk] * n_w + [side, cw]
    out_shape = [jax.ShapeDtypeStruct((m, width), F32 if mode == "xbc" else BF16)]
    out_specs = [pl.BlockSpec((tm, tn), lambda n, i: (i, n))]
    if mode == "xbc":
        in_specs.append(pl.BlockSpec((1, tn), lambda n, i: (0, n)))
        args.append(cb)
        out_shape.append(jax.ShapeDtypeStruct((ms, width), F32))
        out_specs.append(pl.BlockSpec((ms, tn), lambda n, i: (0, n)))
    res = pl.pallas_call(
        functools.partial(_proj_conv_body, n_w=n_w, n_tap=n_tap,
                          tiles_per_batch=seq // tm, mode=mode),
        grid=(width // tn, m // tm),
        in_specs=in_specs,
        out_specs=out_specs,
        out_shape=out_shape,
        scratch_shapes=[pltpu.VMEM((k, n_w * tn), BF16),
                        pltpu.VMEM((SUBLANE, tn), F32),
                        pltpu.VMEM((SUBLANE, tn), F32),
                        pltpu.VMEM((2, min(tm, EPILOGUE_ROWS) + SUBLANE * (mode == "xbc"),
                                    n_w * tn), F32)],
        compiler_params=_cparams(2, VMEM_MB + 4 * (n_w > 1)),
        name=name,
    )(*args)
    return res if mode == "xbc" else res[0]


def _ssd_chunk(xbc, dt_raw, dtb, alog, e2, *, heads, meta):
    L = xbc.shape[0]
    di = heads * SSD_HEAD_DIM
    gn = SSD_GROUPS * SSD_STATE
    dt_in = dt_raw[:, :heads] + dtb
    dtv = jnp.maximum(dt_in, 0.0) + jnp.log1p(jnp.exp(-jnp.abs(dt_in)))
    if meta:
        dtv = dtv * (lax.broadcasted_iota(I32, (L, 1), 0) >= L - N_META).astype(F32)
    xs = xbc[:, :di]
    bm = xbc[:, di:di + gn]
    cm = xbc[:, di + gn:]
    a_neg = -jnp.exp(alog)
    ad = dtv * a_neg
    r = lax.broadcasted_iota(I32, (L, L), 0)
    c = lax.broadcasted_iota(I32, (L, L), 1)
    tril = r >= c
    acum = jnp.dot(tril.astype(F32), ad, precision=lax.Precision.HIGHEST,
                   preferred_element_type=F32)
    stack = jnp.concatenate([dtv, dtv * jnp.exp(acum[L - 1:L, :] - acum), jnp.exp(acum)],
                            axis=0)
    hi = stack.astype(BF16)
    lo = (stack - hi.astype(F32)).astype(BF16)
    exp_x = jnp.dot(jnp.concatenate([hi, lo], axis=1), e2, preferred_element_type=F32)
    dt_x, dd_x, ea_x = exp_x[:L], exp_x[L:2 * L], exp_x[2 * L:]
    bt = bm.T.astype(BF16)
    return dict(xs=xs, cm=cm, bt=bt, acum=acum, ea_x=ea_x, x_dt=(xs * dt_x).astype(BF16),
                x_dec=(xs * dd_x).astype(BF16), tril=tril)


def _ssd_state_update(q, st_ref, heads):
    gw = heads * SSD_HEAD_DIM // SSD_GROUPS
    L = q["ea_x"].shape[0]
    for g in range(SSD_GROUPS):
        sl = slice(g * gw, (g + 1) * gw)
        btg = q["bt"][g * SSD_STATE:(g + 1) * SSD_STATE, :]
        new = jnp.dot(btg, q["x_dec"][:, sl], preferred_element_type=F32)
        st_ref[:, sl] = st_ref[:, sl] * q["ea_x"][L - 1:L, sl] + new


def _ssd_body(xbc_ref, dt_ref, z_ref, s0_ref, dtb_ref, alog_ref, dskip_ref, ng_ref, e_ref,
              o_ref, st_ref, *, heads):
    @pl.when(pl.program_id(1) == 0)
    def _():
        st_ref[...] = s0_ref[...]

    L = xbc_ref.shape[0]
    q = _ssd_chunk(xbc_ref[...], dt_ref[...], dtb_ref[...], alog_ref[...], e_ref[...],
                   heads=heads, meta=False)

    gw = heads * SSD_HEAD_DIM // SSD_GROUPS
    j_heads = heads // SSD_GROUPS
    acum = q["acum"]
    acum_t = acum.T
    cmb = q["cm"].astype(BF16)
    x_dt_b = q["x_dt"]
    lane = lax.broadcasted_iota(I32, (L, 2 * SSD_HEAD_DIM), 1)
    lo = lane < SSD_HEAD_DIM
    neg_inf = jnp.float32(-jnp.inf)

    def decay_mat(h):
        col = jnp.broadcast_to(acum[:, h:h + 1], (L, L))
        row = jnp.broadcast_to(acum_t[h:h + 1, :], (L, L))
        return jnp.exp(jnp.where(q["tril"], col - row, neg_inf))

    for g in range(SSD_GROUPS):
        sl = slice(g * gw, (g + 1) * gw)
        cg = cmb[:, g * SSD_STATE:(g + 1) * SSD_STATE]
        btg = q["bt"][g * SSD_STATE:(g + 1) * SSD_STATE, :]
        cb_mat = jnp.dot(cg, btg, preferred_element_type=F32)
        parts = []
        for p in range(j_heads // 2):
            h1 = g * j_heads + 2 * p
            slab = x_dt_b[:, h1 * SSD_HEAD_DIM:(h1 + 2) * SSD_HEAD_DIM]
            lhs = jnp.concatenate([(cb_mat * decay_mat(h1)).astype(BF16),
                                   (cb_mat * decay_mat(h1 + 1)).astype(BF16)], axis=1)
            zero = jnp.zeros_like(slab)
            rhs = jnp.concatenate([jnp.where(lo, slab, zero),
                                   jnp.where(lo, zero, slab)], axis=0)
            parts.append(jnp.dot(lhs, rhs, preferred_element_type=F32))
        y_diag = parts[0] if len(parts) == 1 else jnp.concatenate(parts, axis=1)
        st_g = st_ref[:, sl]
        y_off = jnp.dot(cg, st_g.astype(BF16), preferred_element_type=F32)
        y = y_diag + y_off * q["ea_x"][:, sl] + dskip_ref[:, sl] * q["xs"][:, sl]
        zg = z_ref[:, sl]
        gated = y * (zg * _sigmoid(zg))
        ms = jnp.mean(gated * gated, axis=-1, keepdims=True)
        o_ref[:, sl] = (gated * lax.rsqrt(ms + EPS) * ng_ref[:, sl]).astype(o_ref.dtype)
        new = jnp.dot(btg, q["x_dec"][:, sl], preferred_element_type=F32)
        st_ref[:, sl] = st_g * q["ea_x"][L - 1:L, sl] + new


def _ssd_meta_body(xbc_ref, dt_ref, dtb_ref, alog_ref, e_ref, st_ref, *, heads):
    st_ref[...] = jnp.zeros_like(st_ref)
    q = _ssd_chunk(xbc_ref[...], dt_ref[...], dtb_ref[...], alog_ref[...], e_ref[...],
                   heads=heads, meta=True)
    _ssd_state_update(q, st_ref, heads)


def _full_spec(shape):
    nd = len(shape)
    return pl.BlockSpec(shape, lambda *_: (0,) * nd)


def _ssd_meta(xbc_m, dt_m, dtb, alog, e2, heads):
    di = heads * SSD_HEAD_DIM
    args = [xbc_m, dt_m, dtb, alog, e2]
    return pl.pallas_call(
        functools.partial(_ssd_meta_body, heads=heads),
        grid=(1,),
        in_specs=[_full_spec(a.shape) for a in args],
        out_specs=_full_spec((SSD_STATE, di)),
        out_shape=jax.ShapeDtypeStruct((SSD_STATE, di), F32),
        compiler_params=_cparams(1),
        name="ssd_meta",
    )(*args)


def _ssd_main(xbc, dt, z, s0, dtb, alog, dskip_x, ng, e2, *, heads, batch):
    t, cd = xbc.shape
    di = heads * SSD_HEAD_DIM
    L = SSD_CHUNK
    nc = t // batch // L
    row = lambda b, c: (b * nc + c, 0)
    in_specs = [pl.BlockSpec((L, cd), row),
                pl.BlockSpec((L, LANE), row),
                pl.BlockSpec((L, di), row)]
    consts = [s0, dtb, alog, dskip_x, ng, e2]
    in_specs += [_full_spec(a.shape) for a in consts]
    return pl.pallas_call(
        functools.partial(_ssd_body, heads=heads),
        grid=(batch, nc),
        in_specs=in_specs,
        out_specs=pl.BlockSpec((L, di), row),
        out_shape=jax.ShapeDtypeStruct((t, di), BF16),
        scratch_shapes=[pltpu.VMEM((SSD_STATE, di), F32)],
        compiler_params=_cparams(2),
        name="ssd_main",
    )(xbc, dt, z, *consts)


def _router_body(x_ref, g_ref, wr_ref, br_ref, hp_ref, idx_ref, prob_ref, rank_ref,
                 cnt_ref, base_ref):
    @pl.when(pl.program_id(0) == 0)
    def _():
        base_ref[...] = jnp.zeros_like(base_ref)

    x = x_ref[...]
    tm, d = x.shape
    ne = wr_ref.shape[0]
    ms = jnp.mean(x * x, axis=-1, keepdims=True)
    h = x * lax.rsqrt(ms + EPS) * g_ref[...]
    half = d // 2
    h_hi = h.astype(BF16)
    h_hi32 = h_hi.astype(F32)
    lo_bits = lax.bitcast_convert_type(h_hi32[:, :half], U32)
    hi_bits = lax.bitcast_convert_type(h_hi32[:, half:], U32)
    hp_ref[...] = (lo_bits >> 16) | (hi_bits & jnp.uint32(0xFFFF0000))

    w = wr_ref[...]
    w_hi = w.astype(BF16)
    w_lo = (w - w_hi.astype(F32)).astype(BF16)
    h_lo = (h - h_hi32).astype(BF16)
    nt = (((1,), (1,)), ((), ()))
    part = lax.dot_general(jnp.concatenate([w_hi, w_lo], axis=0), h_hi, nt,
                           preferred_element_type=F32)
    logits = (part[:ne] + part[ne:]
              + lax.dot_general(w_hi, h_lo, nt, preferred_element_type=F32)
              + br_ref[:, 0:1])
    eidx = lax.broadcasted_iota(I32, (ne, tm), 0)
    work = logits
    vals, onehots = [], []
    for k in range(TOP_K):
        mx = jnp.max(work, axis=0, keepdims=True)
        sel = jnp.min(jnp.where(work == mx, eidx, ne), axis=0, keepdims=True)
        oh = eidx == sel
        idx_ref[k:k + 1, :] = sel
        vals.append(mx)
        onehots.append(oh)
        work = jnp.where(oh, -jnp.inf, work)
    exps = [jnp.exp(v - vals[0]) for v in vals]
    denom = exps[0]
    for e in exps[1:]:
        denom = denom + e
    for k in range(TOP_K):
        prob_ref[k:k + 1, :] = exps[k] / denom

    cnt = onehots[0].astype(F32)
    for oh in onehots[1:]:
        cnt = cnt + oh.astype(F32)
    r = lax.broadcasted_iota(I32, (tm, tm), 0)
    c = lax.broadcasted_iota(I32, (tm, tm), 1)
    before = (r < c).astype(BF16)
    prefix = jnp.dot(cnt.astype(BF16), before, preferred_element_type=F32)
    prefix = prefix + base_ref[:, 0:1]
    for k in range(TOP_K):
        rk = jnp.sum(jnp.where(onehots[k], prefix, 0.0), axis=0, keepdims=True)
        rank_ref[k:k + 1, :] = rk.astype(I32)
    base_ref[...] = base_ref[...] + jnp.sum(cnt, axis=1, keepdims=True)
    cnt_ref[...] = base_ref[...]


def _router(hres, g, wr_t, br, tm):
    t, d = hres.shape
    ne = wr_t.shape[0]
    tok = lambda i: (0, i)
    return pl.pallas_call(
        _router_body,
        grid=(t // tm,),
        in_specs=[pl.BlockSpec((tm, d), lambda i: (i, 0)),
                  _full_spec((1, d)), _full_spec((ne, d)), _full_spec((ne, LANE))],
        out_specs=[pl.BlockSpec((tm, d // 2), lambda i: (i, 0)),
                   pl.BlockSpec((TOP_K, tm), tok),
                   pl.BlockSpec((TOP_K, tm), tok),
                   pl.BlockSpec((TOP_K, tm), tok),
                   _full_spec((ne, LANE))],
        out_shape=[jax.ShapeDtypeStruct((t, d // 2), U32),
                   jax.ShapeDtypeStruct((TOP_K, t), I32),
                   jax.ShapeDtypeStruct((TOP_K, t), F32),
                   jax.ShapeDtypeStruct((TOP_K, t), I32),
                   jax.ShapeDtypeStruct((ne, LANE), F32)],
        scratch_shapes=[pltpu.VMEM((ne, LANE), F32)],
        compiler_params=_cparams(1),
        name="router",
    )(hres, g.reshape(1, d), wr_t, jnp.broadcast_to(br.reshape(ne, 1), (ne, LANE)))


def _dispatch_body(dest_ref, cnt_ref, seg_ref, tot_ref, hp_ref, xs_ref, zrow_ref, zblk_ref,
                   row_sem, pad_sem, *, t_total, ne, n_blocks):
    i = pl.program_id(0)
    tm = hp_ref.shape[0] * SUBLANE

    def pad_copy(slot):
        return pltpu.make_async_copy(zrow_ref.at[pl.ds(0, 1)], xs_ref.at[pl.ds(slot, 1)], pad_sem)

    def tail_copy(b):
        return pltpu.make_async_copy(
            zblk_ref, xs_ref.at[pl.ds(pl.multiple_of(b * MOE_BLK, MOE_BLK), MOE_BLK)], pad_sem)

    @pl.when(i == 0)
    def _():
        zrow_ref[...] = jnp.zeros_like(zrow_ref)
        zblk_ref[...] = jnp.zeros_like(zblk_ref)

        def expert(e, carry):
            used = cnt_ref[e]
            end = (used + MOE_BLK - 1) // MOE_BLK * MOE_BLK

            def start(r, c):
                pad_copy(seg_ref[e] + r).start()
                return c
            lax.fori_loop(used, end, start, 0)

            def wait(r, c):
                pad_copy(seg_ref[e] + r).wait()
                return c
            lax.fori_loop(used, end, wait, 0)
            return carry
        lax.fori_loop(0, ne, expert, 0)

        def tail_start(b, c):
            tail_copy(b).start()
            return c
        lax.fori_loop(tot_ref[0], n_blocks, tail_start, 0)

        def tail_wait(b, c):
            tail_copy(b).wait()
            return c
        lax.fori_loop(tot_ref[0], n_blocks, tail_wait, 0)

    def row_copy(g, u, slot):
        return pltpu.make_async_copy(hp_ref.at[g, pl.ds(u, 1)], xs_ref.at[pl.ds(slot, 1)],
                                     row_sem)

    def issue(g, c):
        for k in range(TOP_K):
            base = k * t_total + i * tm + g * SUBLANE
            for u in range(SUBLANE):
                row_copy(g, u, dest_ref[base + u]).start()
        return c
    lax.fori_loop(0, tm // SUBLANE, issue, 0, unroll=4)

    def drain(g, c):
        for _ in range(TOP_K * SUBLANE):
            row_copy(0, 0, 0).wait()
        return c
    lax.fori_loop(0, tm // SUBLANE, drain, 0)


def _moe_dispatch(dest_flat, cnt, seg_start, total_blk, hp, n_rows, *, tm):
    t, w = hp.shape
    ne = cnt.shape[0]
    return pl.pallas_call(
        functools.partial(_dispatch_body, t_total=t, ne=ne, n_blocks=n_rows // MOE_BLK),
        grid_spec=pltpu.PrefetchScalarGridSpec(
            num_scalar_prefetch=4,
            grid=(t // tm,),
            in_specs=[pl.BlockSpec((tm // SUBLANE, SUBLANE, w), lambda i, *_: (i, 0, 0))],
            out_specs=pl.BlockSpec(memory_space=pl.ANY),
            scratch_shapes=[pltpu.VMEM((SUBLANE, w), U32),
                            pltpu.VMEM((MOE_BLK, w), U32),
                            pltpu.SemaphoreType.DMA,
                            pltpu.SemaphoreType.DMA],
        ),
        out_shape=jax.ShapeDtypeStruct((n_rows, w), U32),
        compiler_params=_cparams(1),
        name="moe_dispatch",
    )(dest_flat, cnt, seg_start, total_blk, hp.reshape(t // SUBLANE, SUBLANE, w))


def _sb_specs(rows, width_in, n_tiles):
    def x_map(s, j, e, start, nblk, tot):
        return (pl.multiple_of(start[s], MOE_BLK), 0)

    def wcol(s, j, nblk):
        return jnp.where(nblk[s] > 0, j, n_tiles - 1)

    return pl.BlockSpec((pl.Element(rows), pl.Element(width_in)), x_map), wcol


def _sb_run(start_ref, nblk_ref, tot_ref, o_ref, stage_ref, zblk_ref, out_sems, zero_sem,
            compute_rows, *, n_blocks):
    s = pl.program_id(0)
    j = pl.program_id(1)
    nj = pl.num_programs(1)
    q = s * nj + j
    slot = q % 2
    tile = stage_ref.shape[2]
    col = pl.multiple_of(j * tile, LANE)
    nb = nblk_ref[s]

    def out_copy(sl, b, row0):
        r0 = pl.multiple_of(b * MOE_BLK, MOE_BLK)
        return pltpu.make_async_copy(
            stage_ref.at[sl, pl.ds(r0, MOE_BLK)],
            o_ref.at[pl.ds(pl.multiple_of(row0 + r0, MOE_BLK), MOE_BLK), pl.ds(col, tile)],
            out_sems.at[sl])

    def drain(sl, count):
        def one(b, c):
            out_copy(sl, 0, 0).wait()
            return c
        lax.fori_loop(0, count, one, 0)

    def zero_copy(b):
        return pltpu.make_async_copy(
            zblk_ref,
            o_ref.at[pl.ds(pl.multiple_of(b * MOE_BLK, MOE_BLK), MOE_BLK), pl.ds(col, tile)],
            zero_sem)

    @pl.when(q >= 2)
    def _():
        drain(slot, nblk_ref[(q - 2) // nj])

    @pl.when(s == 0)
    def _():
        zblk_ref[...] = jnp.zeros_like(zblk_ref)

        def zs(b, c):
            zero_copy(b).start()
            return c
        lax.fori_loop(tot_ref[0], n_blocks, zs, 0)

        def zw(b, c):
            zero_copy(b).wait()
            return c
        lax.fori_loop(tot_ref[0], n_blocks, zw, 0)

    def run(b0, k):
        compute_rows(slot, pl.multiple_of(b0 * MOE_BLK, MOE_BLK), k * MOE_BLK)
        for i in range(k):
            out_copy(slot, b0 + i, start_ref[s]).start()

    k = 1 << (MOE_SB_BLKS.bit_length() - 1)
    while k >= 1:
        @pl.when(nb & k != 0)
        def _(k=k):
            run(nb & ~(2 * k - 1), k)
        k //= 2

    @pl.when(q == pl.num_programs(0) * nj - 1)
    def _():
        drain(slot, nb)

        @pl.when(q >= 1)
        def _():
            drain(1 - slot, nblk_ref[(q - 1) // nj])


def _gate_up_body(e_ref, start_ref, nblk_ref, tot_ref, x_ref, wg_ref, wu_ref, bg_ref, bu_ref,
                  o_ref, wbf_ref, stage_ref, zblk_ref, out_sems, zero_sem, *, n_blocks):
    half = x_ref.shape[1]
    tj = wg_ref.shape[1]

    @pl.when(nblk_ref[pl.program_id(0)] > 0)
    def _():
        wbf_ref[:, :tj] = wg_ref[...].astype(BF16)
        wbf_ref[:, tj:] = wu_ref[...].astype(BF16)

    bias = jnp.concatenate([bg_ref[...], bu_ref[...]], axis=1)

    def compute_rows(slot, r0, n_rows):
        xp = x_ref[pl.ds(r0, n_rows), :]
        x_lo = lax.bitcast_convert_type(xp << 16, F32).astype(BF16)
        x_hi = lax.bitcast_convert_type(xp & jnp.uint32(0xFFFF0000), F32).astype(BF16)
        x = jnp.concatenate([x_lo, x_hi], axis=1)
        gu = jnp.dot(x, wbf_ref[...], preferred_element_type=F32) + bias
        gate = jnp.minimum(gu[:, :tj], SWIGLU_LIMIT)
        up = jnp.clip(gu[:, tj:], -SWIGLU_LIMIT, SWIGLU_LIMIT)
        act = (up + 1.0) * (gate * _sigmoid(gate * SWIGLU_ALPHA))
        stage_ref[slot, pl.ds(r0, n_rows), :] = act.astype(stage_ref.dtype)

    _sb_run(start_ref, nblk_ref, tot_ref, o_ref, stage_ref, zblk_ref, out_sems, zero_sem,
            compute_rows, n_blocks=n_blocks)


def _sb_scratch(rows, tile, dtype):
    return [pltpu.VMEM((2, rows, tile), dtype), pltpu.VMEM((MOE_BLK, tile), dtype),
            pltpu.SemaphoreType.DMA((2,)), pltpu.SemaphoreType.DMA]


def _moe_gate_up(sb, n_sb, xs, w_gu, b_gu, *, tj):
    ne, d, de2 = w_gu.shape
    de = de2 // 2
    nj = de // tj
    rows = MOE_SB_BLKS * MOE_BLK
    n_rows, half = xs.shape
    x_spec, wcol = _sb_specs(rows, half, nj)
    in_specs = [
        x_spec,
        pl.BlockSpec((None, d, tj), lambda s, j, e, st, nb, tt: (e[s], 0, wcol(s, j, nb))),
        pl.BlockSpec((None, d, tj), lambda s, j, e, st, nb, tt: (e[s], 0, nj + wcol(s, j, nb))),
        pl.BlockSpec((None, 1, tj), lambda s, j, e, st, nb, tt: (e[s], 0, wcol(s, j, nb))),
        pl.BlockSpec((None, 1, tj), lambda s, j, e, st, nb, tt: (e[s], 0, nj + wcol(s, j, nb))),
    ]
    return pl.pallas_call(
        functools.partial(_gate_up_body, n_blocks=n_rows // MOE_BLK),
        grid_spec=pltpu.PrefetchScalarGridSpec(
            num_scalar_prefetch=4,
            grid=(n_sb, nj),
            in_specs=in_specs,
            out_specs=pl.BlockSpec(memory_space=pl.ANY),
            scratch_shapes=[pltpu.VMEM((d, 2 * tj), BF16)] + _sb_scratch(rows, tj, BF16),
        ),
        out_shape=jax.ShapeDtypeStruct((n_rows, de), BF16),
        compiler_params=_cparams(2),
        name="moe_gate_up",
    )(*sb, xs, w_gu, w_gu, b_gu.reshape(ne, 1, de2), b_gu.reshape(ne, 1, de2))


def _down_body(e_ref, start_ref, nblk_ref, tot_ref, a_ref, w_ref, b_ref, o_ref, wbf_ref,
               stage_ref, zblk_ref, out_sems, zero_sem, *, n_blocks):
    @pl.when(nblk_ref[pl.program_id(0)] > 0)
    def _():
        wbf_ref[...] = w_ref[...].astype(BF16)

    bias = b_ref[...]

    def compute_rows(slot, r0, n_rows):
        stage_ref[slot, pl.ds(r0, n_rows), :] = jnp.dot(
            a_ref[pl.ds(r0, n_rows), :], wbf_ref[...], preferred_element_type=F32) + bias

    _sb_run(start_ref, nblk_ref, tot_ref, o_ref, stage_ref, zblk_ref, out_sems, zero_sem,
            compute_rows, n_blocks=n_blocks)


def _moe_down(sb, n_sb, act, w_down, b_down, *, tn):
    ne, de, d = w_down.shape
    nn = d // tn
    rows = MOE_SB_BLKS * MOE_BLK
    n_rows = act.shape[0]
    a_spec, wcol = _sb_specs(rows, de, nn)
    in_specs = [
        a_spec,
        pl.BlockSpec((None, de, tn), lambda s, j, e, st, nb, tt: (e[s], 0, wcol(s, j, nb))),
        pl.BlockSpec((None, 1, tn), lambda s, j, e, st, nb, tt: (e[s], 0, wcol(s, j, nb))),
    ]
    return pl.pallas_call(
        functools.partial(_down_body, n_blocks=n_rows // MOE_BLK),
        grid_spec=pltpu.PrefetchScalarGridSpec(
            num_scalar_prefetch=4,
            grid=(n_sb, nn),
            in_specs=in_specs,
            out_specs=pl.BlockSpec(memory_space=pl.ANY),
            scratch_shapes=[pltpu.VMEM((de, tn), BF16)] + _sb_scratch(rows, tn, F32),
        ),
        out_shape=jax.ShapeDtypeStruct((n_rows, d), F32),
        compiler_params=_cparams(2),
        name="moe_down",
    )(*sb, act, w_down, b_down.reshape(ne, 1, d))


def _moe_schedule(cnt, n_pairs, ne):
    n_blocks = n_pairs // MOE_BLK + ne + MOE_SB_BLKS - 1
    s_max = -(-(n_pairs // MOE_BLK + ne) // MOE_SB_BLKS) + ne
    pblk = (cnt + MOE_BLK - 1) // MOE_BLK
    seg_end_blk = jnp.cumsum(pblk)
    seg_start = (seg_end_blk - pblk) * MOE_BLK
    total_blk = seg_end_blk[-1:]
    nsb_e = (pblk + MOE_SB_BLKS - 1) // MOE_SB_BLKS
    sb_end = jnp.cumsum(nsb_e)
    sb_first = sb_end - nsb_e
    n_sb = sb_end[-1]
    s_ids = jnp.arange(s_max, dtype=I32)
    used = s_ids < n_sb
    e_of_s = jnp.minimum(jnp.searchsorted(sb_end, s_ids, side="right"), ne - 1).astype(I32)
    i_in_e = s_ids - sb_first[e_of_s]
    start = seg_start[e_of_s] + i_in_e * (MOE_SB_BLKS * MOE_BLK)
    nblk = jnp.where(used, jnp.clip(pblk[e_of_s] - MOE_SB_BLKS * i_in_e, 0, MOE_SB_BLKS), 0)
    last = jnp.maximum(n_sb - 1, 0)
    sb = (jnp.where(used, e_of_s, e_of_s[last]).astype(I32),
          jnp.where(used, start, start[last]).astype(I32), nblk.astype(I32),
          total_blk.astype(I32))
    return sb, n_sb.astype(I32), seg_start.astype(I32), n_blocks * MOE_BLK


def _combine_body(dest_ref, y_ref, h_ref, p_ref, g_ref, o_ref, ybuf, sems, *, t_total):
    i = pl.program_id(0)
    n = pl.num_programs(0)
    tc, d = h_ref.shape
    groups = tc // SUBLANE

    def row_copy(slot, k, g, u, row):
        return pltpu.make_async_copy(y_ref.at[pl.ds(row, 1)], ybuf.at[slot, k, g, pl.ds(u, 1)],
                                     sems.at[slot])

    def issue(tile, slot):
        def one(g, carry):
            for k in range(TOP_K):
                base = k * t_total + tile * tc + g * SUBLANE
                for u in range(SUBLANE):
                    row_copy(slot, k, g, u, dest_ref[base + u]).start()
            return carry
        lax.fori_loop(0, groups, one, 0, unroll=4)

    @pl.when(i == 0)
    def _():
        for t0 in range(COMBINE_BUFS - 1):
            @pl.when(t0 < n)
            def _(t0=t0):
                issue(t0, t0)

    slot = i % COMBINE_BUFS

    def drain(g, carry):
        for _ in range(TOP_K * SUBLANE):
            row_copy(slot, 0, 0, 0, 0).wait()
        return carry
    lax.fori_loop(0, groups, drain, 0)

    @pl.when(i + COMBINE_BUFS - 1 < n)
    def _():
        issue(i + COMBINE_BUFS - 1, (i + COMBINE_BUFS - 1) % COMBINE_BUFS)

    p = p_ref[...]
    cw = min(d, 4 * LANE)
    ssq = jnp.zeros((tc, 1), F32)
    for c in range(d // cw):
        cols = slice(c * cw, (c + 1) * cw)
        acc = h_ref[:, cols]
        for k in range(TOP_K):
            acc = acc + p[:, k:k + 1] * ybuf[slot, k, :, :, cols].reshape(tc, cw)
        o_ref[:, cols] = acc
        ssq = ssq + jnp.sum(acc * acc, axis=-1, keepdims=True)
    scale = lax.rsqrt(ssq * (1.0 / d) + EPS)
    for c in range(d // cw):
        cols = slice(c * cw, (c + 1) * cw)
        o_ref[:, cols] = o_ref[:, cols] * scale * g_ref[:, cols]


def _combine(dest_flat, y_sorted, hres, p_t, final_g, *, tc):
    t, d = hres.shape
    return pl.pallas_call(
        functools.partial(_combine_body, t_total=t),
        grid_spec=pltpu.PrefetchScalarGridSpec(
            num_scalar_prefetch=1,
            grid=(t // tc,),
            in_specs=[pl.BlockSpec(memory_space=pl.ANY),
                      pl.BlockSpec((tc, d), lambda i, *_: (i, 0)),
                      pl.BlockSpec((tc, TOP_K), lambda i, *_: (i, 0)),
                      pl.BlockSpec((1, d), lambda i, *_: (0, 0))],
            out_specs=pl.BlockSpec((tc, d), lambda i, *_: (i, 0)),
            scratch_shapes=[pltpu.VMEM((COMBINE_BUFS, TOP_K, tc // SUBLANE, SUBLANE, d), F32),
                            pltpu.SemaphoreType.DMA((COMBINE_BUFS,))],
        ),
        out_shape=jax.ShapeDtypeStruct((t, d), F32),
        compiler_params=_cparams(1),
        name="moe_combine",
    )(dest_flat, y_sorted, hres, p_t, final_g.reshape(1, d))


def _pick(n, prefs):
    for p in prefs:
        if n % p == 0:
            return p
    raise ValueError(f"no tile in {prefs} divides {n}")


def kernel(x, meta_tokens, norm_mix_g, w_in, ssd_conv_w, ssd_conv_b, dt_bias, a_log, d_skip,
           ssd_norm_g, w_ssd_out, sc_conv_w, w_sc_out, w_o, norm_ffn_g, w_router, b_router,
           w_gate_up, b_gate_up, w_down, b_down, final_g):
    batch, seq, d = x.shape
    assert norm_mix_g.shape[0] == 1, "meta-token handling assumes a single layer"
    assert seq % SSD_CHUNK == 0 and N_META <= SSD_CHUNK
    t = batch * seq
    heads = d // SSD_HEAD_DIM
    gn = SSD_GROUPS * SSD_STATE
    conv_dim = d + 2 * gn
    xbc_off = d
    dt_off = xbc_off + conv_dim
    sc_off = dt_off + heads
    gate_off = sc_off + 3 * d
    ne = w_router.shape[-1]
    de = w_down.shape[2]

    x2 = x.reshape(t, d)
    w_in_nk = jnp.swapaxes(w_in[0], 0, 1)

    tm = _pick(seq, (1024, 512, 256, 128))
    tn = _pick(d, (512, 256, 128))

    h = _rmsnorm_bf16(x2, norm_mix_g[0], _pick(t, (256, 128)))
    h_meta = _rmsnorm_bf16(meta_tokens, norm_mix_g[0], N_META)

    proj = functools.partial(_matmul, h, w_in_nk, tm=tm, out_dtype=F32, w_is_nk=True)
    z = proj(0, d, tn=tn, name="proj_z")
    xbc, xbc_meta = _proj_conv(h, w_in_nk, [xbc_off], h_meta, ssd_conv_w[0],
                               ssd_conv_b[0].reshape(1, conv_dim), width=conv_dim, tm=tm,
                               tn=tn, seq=seq, mode="xbc", name="proj_xbc")
    dtr, dt_meta = proj(dt_off, LANE, tn=LANE, side=h_meta, name="proj_dt")
    u_sc = _proj_conv(h, w_in_nk, [sc_off, sc_off + d, sc_off + 2 * d], h_meta,
                      sc_conv_w[0], None, width=d, tm=tm, tn=_pick(d, (256, 128)), seq=seq,
                      mode="sc", name="proj_sc")
    gates = proj(gate_off, 2 * d, tn=tn, name="proj_gate")

    e_mat = (jnp.arange(d, dtype=I32)[None, :] // SSD_HEAD_DIM
             == jnp.arange(heads, dtype=I32)[:, None]).astype(BF16)
    e2 = jnp.concatenate([e_mat, e_mat], axis=0)
    pad = SSD_CHUNK - N_META
    dtb = dt_bias[0].reshape(1, heads)
    alog = a_log[0].reshape(1, heads)
    s0 = _ssd_meta(jnp.pad(xbc_meta, ((pad, 0), (0, 0))),
                   jnp.pad(dt_meta, ((pad, 0), (0, 0))), dtb, alog, e2, heads)
    dskip_x = jnp.repeat(d_skip[0], SSD_HEAD_DIM).reshape(1, d)
    g_ssd = _ssd_main(xbc, dtr, z, s0, dtb, alog, dskip_x, ssd_norm_g[0].reshape(1, d), e2,
                      heads=heads, batch=batch)

    y_a = _matmul(g_ssd, w_ssd_out[0], 0, d, tm=tm, tn=tn, out_dtype=F32,
                  extras=[(gates, 0)],
                  epilogue=lambda acc, ga: _sigmoid(ga) * acc, name="ssd_out")
    mixed = _matmul(u_sc, w_sc_out[0], 0, d, tm=tm, tn=tn, out_dtype=BF16,
                    extras=[(gates, d), (y_a, 0)],
                    epilogue=lambda acc, gb, ya: ya + _sigmoid(gb) * acc, name="sc_out")
    hres = _matmul(mixed, w_o[0], 0, d, tm=tm, tn=tn, out_dtype=F32,
                   extras=[(x2, 0)], epilogue=lambda acc, xr: xr + acc, name="w_o")

    rt = _pick(t, (512, 256, 128))
    hp, top_idx, probs, rank, counts = _router(hres, norm_ffn_g[0], w_router[0].T,
                                               b_router[0], rt)
    cnt = counts[:, 0].astype(I32)
    sb, n_sb, seg_start, n_slots = _moe_schedule(cnt, t * TOP_K, ne)
    e_ids = jnp.arange(ne, dtype=I32)[:, None, None]
    dest = rank + jnp.sum(jnp.where(top_idx[None] == e_ids, seg_start[:, None, None], 0),
                          axis=0)
    dest_flat = dest.reshape(-1)

    xs = _moe_dispatch(dest_flat, cnt, seg_start, sb[3], hp, n_slots,
                       tm=_pick(t, (512, 256, 128)))
    act = _moe_gate_up(sb, n_sb, xs, w_gate_up[0], b_gate_up[0], tj=_pick(de, (256, 128)))
    y_sorted = _moe_down(sb, n_sb, act, w_down[0], b_down[0],
                         tn=_pick(d, (1024, 512, 256, 128)))
    out = _combine(dest_flat, y_sorted, hres, probs.T, final_g, tc=_pick(t, (128,)))
    return out.reshape(batch, seq, d)
```

```python
import functools

import jax
import jax.numpy as jnp
from jax import lax
from jax.experimental import pallas as pl
from jax.experimental.pallas import tpu as pltpu

F32 = jnp.float32
BF16 = jnp.bfloat16
I32 = jnp.int32
U32 = jnp.uint32

N_META = 16
SSD_HEAD_DIM = 64
SSD_GROUPS = 8
SSD_STATE = 128
SSD_CONV = 4
SSD_CHUNK = 128
SC_CONV = 3
TOP_K = 4
SWIGLU_LIMIT = 7.0
SWIGLU_ALPHA = 1.702
EPS = 1e-5

LANE = 128
SUBLANE = 8
MOE_BLK = 128
MOE_SB_BLKS = 10
EPILOGUE_ROWS = 256
COMBINE_BUFS = 3
VMEM_MB = 56
VMEM_MB_WIDE = 60


def _cparams(n_axes, vmem_mb=VMEM_MB):
    return pltpu.CompilerParams(
        dimension_semantics=("arbitrary",) * n_axes,
        vmem_limit_bytes=vmem_mb * 1024 * 1024,
    )


def _sigmoid(v):
    return 1.0 / (1.0 + jnp.exp(-v))


def _rms_body(x_ref, g_ref, o_ref):
    x = x_ref[...]
    ms = jnp.mean(x * x, axis=-1, keepdims=True)
    o_ref[...] = (x * lax.rsqrt(ms + EPS) * g_ref[...]).astype(o_ref.dtype)


def _rmsnorm_bf16(x2, g, tm):
    m, d = x2.shape
    return pl.pallas_call(
        _rms_body,
        grid=(m // tm,),
        in_specs=[pl.BlockSpec((tm, d), lambda i: (i, 0)),
                  pl.BlockSpec((1, d), lambda i: (0, 0))],
        out_specs=pl.BlockSpec((tm, d), lambda i: (i, 0)),
        out_shape=jax.ShapeDtypeStruct((m, d), BF16),
        compiler_params=_cparams(1),
        name="rmsnorm",
    )(x2, g.reshape(1, d))


def _chunked_matmul(a_ref, wbf_ref, acc_ref, finish):
    tm = a_ref.shape[0]
    ch = acc_ref.shape[1]
    n = tm // ch

    def rows(c):
        return slice(c * ch, (c + 1) * ch)

    for c in range(n):
        if c > 0:
            finish(rows(c - 1), acc_ref[(c - 1) % 2])
        acc_ref[c % 2] = jnp.dot(a_ref[rows(c), :], wbf_ref[...], preferred_element_type=F32)
    finish(rows(n - 1), acc_ref[(n - 1) % 2])


def _mm_body(*refs, n_extra, has_side, epilogue, w_is_nk):
    it = iter(refs)
    a_ref = next(it)
    w_ref = next(it)
    side_ref = next(it) if has_side else None
    extras = [next(it) for _ in range(n_extra)]
    o_ref = next(it)
    oside_ref = next(it) if has_side else None
    wbf_ref = next(it)
    acc_ref = next(it) if epilogue is not None else None

    @pl.when(pl.program_id(1) == 0)
    def _():
        w = w_ref[...]
        wbf_ref[...] = (w.T if w_is_nk else w).astype(BF16)
        if has_side:
            oside_ref[...] = jnp.dot(side_ref[...], wbf_ref[...],
                                     preferred_element_type=F32)

    if epilogue is None:
        o_ref[...] = jnp.dot(a_ref[...], wbf_ref[...],
                             preferred_element_type=F32).astype(o_ref.dtype)
    else:
        def finish(rows, acc):
            o_ref[rows, :] = epilogue(acc, *[e[rows, :] for e in extras]).astype(o_ref.dtype)
        _chunked_matmul(a_ref, wbf_ref, acc_ref, finish)


def _matmul(a, w, col_off, n_cols, *, tm, tn, out_dtype, side=None, extras=(),
            epilogue=None, w_is_nk=False, name="matmul"):
    m, k = a.shape
    assert m % tm == 0 and n_cols % tn == 0
    if w_is_nk:
        assert col_off % SUBLANE == 0
        w_spec = pl.BlockSpec((pl.Element(tn), pl.Element(k)),
                              lambda n, i: (pl.multiple_of(col_off + n * tn, SUBLANE), 0))
    else:
        assert col_off % tn == 0
        w_spec = pl.BlockSpec((k, tn), lambda n, i, cb=col_off // tn: (0, n + cb))
    in_specs = [pl.BlockSpec((tm, k), lambda n, i: (i, 0)), w_spec]
    args = [a, w]
    if side is not None:
        in_specs.append(pl.BlockSpec((side.shape[0], k), lambda n, i: (0, 0)))
        args.append(side)
    for arr, off in extras:
        assert off % tn == 0
        in_specs.append(pl.BlockSpec((tm, tn), lambda n, i, o=off // tn: (i, n + o)))
        args.append(arr)
    out_shape = [jax.ShapeDtypeStruct((m, n_cols), out_dtype)]
    out_specs = [pl.BlockSpec((tm, tn), lambda n, i: (i, n))]
    if side is not None:
        out_shape.append(jax.ShapeDtypeStruct((side.shape[0], n_cols), F32))
        out_specs.append(pl.BlockSpec((side.shape[0], tn), lambda n, i: (0, n)))
    res = pl.pallas_call(
        functools.partial(_mm_body, n_extra=len(extras), has_side=side is not None,
                          epilogue=epilogue, w_is_nk=w_is_nk),
        grid=(n_cols // tn, m // tm),
        in_specs=in_specs,
        out_specs=out_specs,
        out_shape=out_shape,
        scratch_shapes=[pltpu.VMEM((k, tn), BF16)]
        + ([pltpu.VMEM((2, min(tm, EPILOGUE_ROWS), tn), F32)] if epilogue is not None else []),
        compiler_params=_cparams(2),
        name=name,
    )(*args)
    return res if side is not None else res[0]


def _causal_conv(raw, prev, w, n_tap):
    acc = raw * w[n_tap - 1:n_tap, :]
    row8 = lax.broadcasted_iota(I32, (SUBLANE, raw.shape[1]), 0)
    for s in range(1, n_tap):
        sh = pltpu.roll(raw, s, 0)
        shp = pltpu.roll(prev, s, 0)
        first = jnp.where(row8 < s, shp, sh[:SUBLANE])
        sh = jnp.concatenate([first, sh[SUBLANE:]], axis=0)
        acc = acc + sh * w[n_tap - 1 - s:n_tap - s, :]
    return acc


def _proj_conv_body(*refs, n_w, n_tap, tiles_per_batch, mode):
    it = iter(refs)
    a_ref = next(it)
    w_refs = [next(it) for _ in range(n_w)]
    side_ref = next(it)
    cw_ref = next(it)
    cb_ref = next(it) if mode == "xbc" else None
    o_ref = next(it)
    oside_ref = next(it) if mode == "xbc" else None
    wbf_ref = next(it)
    prev_ref = next(it)
    tail_ref = next(it)
    u_ref = next(it)
    tn = o_ref.shape[1]
    cw = cw_ref[...]

    def pre(acc):
        return acc if mode == "xbc" else acc[:, tn:2 * tn] * acc[:, 2 * tn:]

    def post(y, acc):
        if mode == "xbc":
            v = y + cb_ref[...]
            return v * _sigmoid(v)
        return acc[:, :tn] * y

    m = pl.program_id(1)

    @pl.when(m == 0)
    def _():
        for i, w_ref in enumerate(w_refs):
            wbf_ref[:, i * tn:(i + 1) * tn] = w_ref[...].T.astype(BF16)
        acc_m = jnp.dot(side_ref[...], wbf_ref[...], preferred_element_type=F32)
        u_m = pre(acc_m)
        tail_ref[...] = u_m[N_META - SUBLANE:, :]
        if mode == "xbc":
            y_m = _causal_conv(u_m, jnp.zeros((SUBLANE, tn), F32), cw, n_tap)
            oside_ref[...] = post(y_m, acc_m)

    @pl.when(m % tiles_per_batch == 0)
    def _():
        prev_ref[...] = tail_ref[...]

    ch = u_ref.shape[0] - SUBLANE
    for c in range(a_ref.shape[0] // ch):
        rows = slice(c * ch, (c + 1) * ch)
        acc = jnp.dot(a_ref[rows, :], wbf_ref[...], preferred_element_type=F32)
        u_ref[:SUBLANE, :] = prev_ref[...]
        u_ref[SUBLANE:, :] = pre(acc)
        y = None
        for k in range(n_tap):
            r0 = SUBLANE - (n_tap - 1) + k
            term = u_ref[r0:r0 + ch, :] * cw[k:k + 1, :]
            y = term if y is None else y + term
        prev_ref[...] = u_ref[ch:, :]
        o_ref[rows, :] = post(y, acc).astype(o_ref.dtype)


def _proj_conv(a, w_nk, row_offs, side, cw, cb, *, width, tm, tn, seq, mode, name):
    m, k = a.shape
    n_w = len(row_offs)
    n_tap = cw.shape[0]
    ms = side.shape[0]

    def w_spec(off):
        return pl.BlockSpec((pl.Element(tn), pl.Element(k)),
                            lambda n, i: (pl.multiple_of(off + n * tn, SUBLANE), 0))

    in_specs = [pl.BlockSpec((tm, k), lambda n, i: (i, 0))]
    in_specs += [w_spec(off) for off in row_offs]
    in_specs += [pl.BlockSpec((ms, k), lambda n, i: (0, 0)),
                 pl.BlockSpec((n_tap, tn), lambda n, i: (0, n))]
    args = [a] + [w_nk] * n_w + [side, cw]
    out_shape = [jax.ShapeDtypeStruct((m, width), F32 if mode == "xbc" else BF16)]
    out_specs = [pl.BlockSpec((tm, tn), lambda n, i: (i, n))]
    if mode == "xbc":
        in_specs.append(pl.BlockSpec((1, tn), lambda n, i: (0, n)))
        args.append(cb)
        out_shape.append(jax.ShapeDtypeStruct((ms, width), F32))
        out_specs.append(pl.BlockSpec((ms, tn), lambda n, i: (0, n)))
    res = pl.pallas_call(
        functools.partial(_proj_conv_body, n_w=n_w, n_tap=n_tap,
                          tiles_per_batch=seq // tm, mode=mode),
        grid=(width // tn, m // tm),
        in_specs=in_specs,
        out_specs=out_specs,
        out_shape=out_shape,
        scratch_shapes=[pltpu.VMEM((k, n_w * tn), BF16),
                        pltpu.VMEM((SUBLANE, tn), F32),
                        pltpu.VMEM((SUBLANE, tn), F32),
                        pltpu.VMEM((min(tm, EPILOGUE_ROWS) + SUBLANE, tn), F32)],
        compiler_params=_cparams(2, VMEM_MB if n_w == 1 else VMEM_MB_WIDE),
        name=name,
    )(*args)
    return res if mode == "xbc" else res[0]


def _ssd_chunk(xbc, dt_raw, dtb, alog, e2, *, heads, meta):
    L = xbc.shape[0]
    di = heads * SSD_HEAD_DIM
    gn = SSD_GROUPS * SSD_STATE
    dt_in = dt_raw[:, :heads] + dtb
    dtv = jnp.maximum(dt_in, 0.0) + jnp.log1p(jnp.exp(-jnp.abs(dt_in)))
    if meta:
        dtv = dtv * (lax.broadcasted_iota(I32, (L, 1), 0) >= L - N_META).astype(F32)
    xs = xbc[:, :di]
    bm = xbc[:, di:di + gn]
    cm = xbc[:, di + gn:]
    a_neg = -jnp.exp(alog)
    ad = dtv * a_neg
    r = lax.broadcasted_iota(I32, (L, L), 0)
    c = lax.broadcasted_iota(I32, (L, L), 1)
    tril = r >= c
    acum = jnp.dot(tril.astype(F32), ad, precision=lax.Precision.HIGHEST,
                   preferred_element_type=F32)
    stack = jnp.concatenate([dtv, dtv * jnp.exp(acum[L - 1:L, :] - acum), jnp.exp(acum)],
                            axis=0)
    hi = stack.astype(BF16)
    lo = (stack - hi.astype(F32)).astype(BF16)
    exp_x = jnp.dot(jnp.concatenate([hi, lo], axis=1), e2, preferred_element_type=F32)
    dt_x, dd_x, ea_x = exp_x[:L], exp_x[L:2 * L], exp_x[2 * L:]
    bt = bm.T.astype(BF16)
    return dict(xs=xs, cm=cm, bt=bt, acum=acum, ea_x=ea_x, x_dt=(xs * dt_x).astype(BF16),
                x_dec=(xs * dd_x).astype(BF16), tril=tril)


def _ssd_state_update(q, st_ref, heads):
    gw = heads * SSD_HEAD_DIM // SSD_GROUPS
    L = q["ea_x"].shape[0]
    for g in range(SSD_GROUPS):
        sl = slice(g * gw, (g + 1) * gw)
        btg = q["bt"][g * SSD_STATE:(g + 1) * SSD_STATE, :]
        new = jnp.dot(btg, q["x_dec"][:, sl], preferred_element_type=F32)
        st_ref[:, sl] = st_ref[:, sl] * q["ea_x"][L - 1:L, sl] + new


def _ssd_body(xbc_ref, dt_ref, z_ref, s0_ref, dtb_ref, alog_ref, dskip_ref, ng_ref, e_ref,
              o_ref, st_ref, *, heads):
    @pl.when(pl.program_id(1) == 0)
    def _():
        st_ref[...] = s0_ref[...]

    L = xbc_ref.shape[0]
    q = _ssd_chunk(xbc_ref[...], dt_ref[...], dtb_ref[...], alog_ref[...], e_ref[...],
                   heads=heads, meta=False)

    gw = heads * SSD_HEAD_DIM // SSD_GROUPS
    j_heads = heads // SSD_GROUPS
    acum = q["acum"]
    acum_t = acum.T
    cmb = q["cm"].astype(BF16)
    x_dt_b = q["x_dt"]
    lane = lax.broadcasted_iota(I32, (L, 2 * SSD_HEAD_DIM), 1)
    lo = lane < SSD_HEAD_DIM
    neg_inf = jnp.float32(-jnp.inf)

    def decay_mat(h):
        col = jnp.broadcast_to(acum[:, h:h + 1], (L, L))
        row = jnp.broadcast_to(acum_t[h:h + 1, :], (L, L))
        return jnp.exp(jnp.where(q["tril"], col - row, neg_inf))

    for g in range(SSD_GROUPS):
        sl = slice(g * gw, (g + 1) * gw)
        cg = cmb[:, g * SSD_STATE:(g + 1) * SSD_STATE]
        btg = q["bt"][g * SSD_STATE:(g + 1) * SSD_STATE, :]
        cb_mat = jnp.dot(cg, btg, preferred_element_type=F32)
        parts = []
        for p in range(j_heads // 2):
            h1 = g * j_heads + 2 * p
            slab = x_dt_b[:, h1 * SSD_HEAD_DIM:(h1 + 2) * SSD_HEAD_DIM]
            lhs = jnp.concatenate([(cb_mat * decay_mat(h1)).astype(BF16),
                                   (cb_mat * decay_mat(h1 + 1)).astype(BF16)], axis=1)
            zero = jnp.zeros_like(slab)
            rhs = jnp.concatenate([jnp.where(lo, slab, zero),
                                   jnp.where(lo, zero, slab)], axis=0)
            parts.append(jnp.dot(lhs, rhs, preferred_element_type=F32))
        y_diag = parts[0] if len(parts) == 1 else jnp.concatenate(parts, axis=1)
        st_g = st_ref[:, sl]
        y_off = jnp.dot(cg, st_g.astype(BF16), preferred_element_type=F32)
        y = y_diag + y_off * q["ea_x"][:, sl] + dskip_ref[:, sl] * q["xs"][:, sl]
        zg = z_ref[:, sl]
        gated = y * (zg * _sigmoid(zg))
        ms = jnp.mean(gated * gated, axis=-1, keepdims=True)
        o_ref[:, sl] = (gated * lax.rsqrt(ms + EPS) * ng_ref[:, sl]).astype(o_ref.dtype)
        new = jnp.dot(btg, q["x_dec"][:, sl], preferred_element_type=F32)
        st_ref[:, sl] = st_g * q["ea_x"][L - 1:L, sl] + new


def _ssd_meta_body(xbc_ref, dt_ref, dtb_ref, alog_ref, e_ref, st_ref, *, heads):
    st_ref[...] = jnp.zeros_like(st_ref)
    q = _ssd_chunk(xbc_ref[...], dt_ref[...], dtb_ref[...], alog_ref[...], e_ref[...],
                   heads=heads, meta=True)
    _ssd_state_update(q, st_ref, heads)


def _full_spec(shape):
    nd = len(shape)
    return pl.BlockSpec(shape, lambda *_: (0,) * nd)


def _ssd_meta(xbc_m, dt_m, dtb, alog, e2, heads):
    di = heads * SSD_HEAD_DIM
    args = [xbc_m, dt_m, dtb, alog, e2]
    return pl.pallas_call(
        functools.partial(_ssd_meta_body, heads=heads),
        grid=(1,),
        in_specs=[_full_spec(a.shape) for a in args],
        out_specs=_full_spec((SSD_STATE, di)),
        out_shape=jax.ShapeDtypeStruct((SSD_STATE, di), F32),
        compiler_params=_cparams(1),
        name="ssd_meta",
    )(*args)


def _ssd_main(xbc, dt, z, s0, dtb, alog, dskip_x, ng, e2, *, heads, batch):
    t, cd = xbc.shape
    di = heads * SSD_HEAD_DIM
    L = SSD_CHUNK
    nc = t // batch // L
    row = lambda b, c: (b * nc + c, 0)
    in_specs = [pl.BlockSpec((L, cd), row),
                pl.BlockSpec((L, LANE), row),
                pl.BlockSpec((L, di), row)]
    consts = [s0, dtb, alog, dskip_x, ng, e2]
    in_specs += [_full_spec(a.shape) for a in consts]
    return pl.pallas_call(
        functools.partial(_ssd_body, heads=heads),
        grid=(batch, nc),
        in_specs=in_specs,
        out_specs=pl.BlockSpec((L, di), row),
        out_shape=jax.ShapeDtypeStruct((t, di), BF16),
        scratch_shapes=[pltpu.VMEM((SSD_STATE, di), F32)],
        compiler_params=_cparams(2),
        name="ssd_main",
    )(xbc, dt, z, *consts)


def _router_body(x_ref, g_ref, wr_ref, br_ref, hp_ref, idx_ref, prob_ref, rank_ref,
                 cnt_ref, base_ref):
    @pl.when(pl.program_id(0) == 0)
    def _():
        base_ref[...] = jnp.zeros_like(base_ref)

    x = x_ref[...]
    tm, d = x.shape
    ne = wr_ref.shape[0]
    ms = jnp.mean(x * x, axis=-1, keepdims=True)
    h = x * lax.rsqrt(ms + EPS) * g_ref[...]
    half = d // 2
    h_hi = h.astype(BF16)
    h_hi32 = h_hi.astype(F32)
    lo_bits = lax.bitcast_convert_type(h_hi32[:, :half], U32)
    hi_bits = lax.bitcast_convert_type(h_hi32[:, half:], U32)
    hp_ref[...] = (lo_bits >> 16) | (hi_bits & jnp.uint32(0xFFFF0000))

    w = wr_ref[...]
    w_hi = w.astype(BF16)
    w_lo = (w - w_hi.astype(F32)).astype(BF16)
    h_lo = (h - h_hi32).astype(BF16)
    nt = (((1,), (1,)), ((), ()))
    part = lax.dot_general(jnp.concatenate([w_hi, w_lo], axis=0), h_hi, nt,
                           preferred_element_type=F32)
    logits = (part[:ne] + part[ne:]
              + lax.dot_general(w_hi, h_lo, nt, preferred_element_type=F32)
              + br_ref[:, 0:1])
    eidx = lax.broadcasted_iota(I32, (ne, tm), 0)
    work = logits
    vals, onehots = [], []
    for k in range(TOP_K):
        mx = jnp.max(work, axis=0, keepdims=True)
        sel = jnp.min(jnp.where(work == mx, eidx, ne), axis=0, keepdims=True)
        oh = eidx == sel
        idx_ref[k:k + 1, :] = sel
        vals.append(mx)
        onehots.append(oh)
        work = jnp.where(oh, -jnp.inf, work)
    exps = [jnp.exp(v - vals[0]) for v in vals]
    denom = exps[0]
    for e in exps[1:]:
        denom = denom + e
    for k in range(TOP_K):
        prob_ref[k:k + 1, :] = exps[k] / denom

    cnt = onehots[0].astype(F32)
    for oh in onehots[1:]:
        cnt = cnt + oh.astype(F32)
    r = lax.broadcasted_iota(I32, (tm, tm), 0)
    c = lax.broadcasted_iota(I32, (tm, tm), 1)
    before = (r < c).astype(BF16)
    prefix = jnp.dot(cnt.astype(BF16), before, preferred_element_type=F32)
    prefix = prefix + base_ref[:, 0:1]
    for k in range(TOP_K):
        rk = jnp.sum(jnp.where(onehots[k], prefix, 0.0), axis=0, keepdims=True)
        rank_ref[k:k + 1, :] = rk.astype(I32)
    base_ref[...] = base_ref[...] + jnp.sum(cnt, axis=1, keepdims=True)
    cnt_ref[...] = base_ref[...]


def _router(hres, g, wr_t, br, tm):
    t, d = hres.shape
    ne = wr_t.shape[0]
    tok = lambda i: (0, i)
    return pl.pallas_call(
        _router_body,
        grid=(t // tm,),
        in_specs=[pl.BlockSpec((tm, d), lambda i: (i, 0)),
                  _full_spec((1, d)), _full_spec((ne, d)), _full_spec((ne, LANE))],
        out_specs=[pl.BlockSpec((tm, d // 2), lambda i: (i, 0)),
                   pl.BlockSpec((TOP_K, tm), tok),
                   pl.BlockSpec((TOP_K, tm), tok),
                   pl.BlockSpec((TOP_K, tm), tok),
                   _full_spec((ne, LANE))],
        out_shape=[jax.ShapeDtypeStruct((t, d // 2), U32),
                   jax.ShapeDtypeStruct((TOP_K, t), I32),
                   jax.ShapeDtypeStruct((TOP_K, t), F32),
                   jax.ShapeDtypeStruct((TOP_K, t), I32),
                   jax.ShapeDtypeStruct((ne, LANE), F32)],
        scratch_shapes=[pltpu.VMEM((ne, LANE), F32)],
        compiler_params=_cparams(1),
        name="router",
    )(hres, g.reshape(1, d), wr_t, jnp.broadcast_to(br.reshape(ne, 1), (ne, LANE)))


def _dispatch_body(dest_ref, cnt_ref, seg_ref, tot_ref, hp_ref, xs_ref, zrow_ref, zblk_ref,
                   row_sem, pad_sem, *, t_total, ne, n_blocks):
    i = pl.program_id(0)
    tm = hp_ref.shape[0] * SUBLANE

    def pad_copy(slot):
        return pltpu.make_async_copy(zrow_ref.at[pl.ds(0, 1)], xs_ref.at[pl.ds(slot, 1)], pad_sem)

    def tail_copy(b):
        return pltpu.make_async_copy(
            zblk_ref, xs_ref.at[pl.ds(pl.multiple_of(b * MOE_BLK, MOE_BLK), MOE_BLK)], pad_sem)

    @pl.when(i == 0)
    def _():
        zrow_ref[...] = jnp.zeros_like(zrow_ref)
        zblk_ref[...] = jnp.zeros_like(zblk_ref)

        def expert(e, carry):
            used = cnt_ref[e]
            end = (used + MOE_BLK - 1) // MOE_BLK * MOE_BLK

            def start(r, c):
                pad_copy(seg_ref[e] + r).start()
                return c
            lax.fori_loop(used, end, start, 0)

            def wait(r, c):
                pad_copy(seg_ref[e] + r).wait()
                return c
            lax.fori_loop(used, end, wait, 0)
            return carry
        lax.fori_loop(0, ne, expert, 0)

        def tail_start(b, c):
            tail_copy(b).start()
            return c
        lax.fori_loop(tot_ref[0], n_blocks, tail_start, 0)

        def tail_wait(b, c):
            tail_copy(b).wait()
            return c
        lax.fori_loop(tot_ref[0], n_blocks, tail_wait, 0)

    def row_copy(g, u, slot):
        return pltpu.make_async_copy(hp_ref.at[g, pl.ds(u, 1)], xs_ref.at[pl.ds(slot, 1)],
                                     row_sem)

    def issue(g, c):
        for k in range(TOP_K):
            base = k * t_total + i * tm + g * SUBLANE
            for u in range(SUBLANE):
                row_copy(g, u, dest_ref[base + u]).start()
        return c
    lax.fori_loop(0, tm // SUBLANE, issue, 0, unroll=4)

    def drain(g, c):
        for _ in range(TOP_K * SUBLANE):
            row_copy(0, 0, 0).wait()
        return c
    lax.fori_loop(0, tm // SUBLANE, drain, 0)


def _moe_dispatch(dest_flat, cnt, seg_start, total_blk, hp, n_rows, *, tm):
    t, w = hp.shape
    ne = cnt.shape[0]
    return pl.pallas_call(
        functools.partial(_dispatch_body, t_total=t, ne=ne, n_blocks=n_rows // MOE_BLK),
        grid_spec=pltpu.PrefetchScalarGridSpec(
            num_scalar_prefetch=4,
            grid=(t // tm,),
            in_specs=[pl.BlockSpec((tm // SUBLANE, SUBLANE, w), lambda i, *_: (i, 0, 0))],
            out_specs=pl.BlockSpec(memory_space=pl.ANY),
            scratch_shapes=[pltpu.VMEM((SUBLANE, w), U32),
                            pltpu.VMEM((MOE_BLK, w), U32),
                            pltpu.SemaphoreType.DMA,
                            pltpu.SemaphoreType.DMA],
        ),
        out_shape=jax.ShapeDtypeStruct((n_rows, w), U32),
        compiler_params=_cparams(1),
        name="moe_dispatch",
    )(dest_flat, cnt, seg_start, total_blk, hp.reshape(t // SUBLANE, SUBLANE, w))


def _sb_specs(rows, width_in, n_tiles):
    def x_map(s, j, e, start, nblk, tot):
        return (pl.multiple_of(start[s], MOE_BLK), 0)

    def wcol(s, j, nblk):
        return jnp.where(nblk[s] > 0, j, n_tiles - 1)

    return pl.BlockSpec((pl.Element(rows), pl.Element(width_in)), x_map), wcol


def _sb_run(start_ref, nblk_ref, tot_ref, o_ref, stage_ref, zblk_ref, out_sems, zero_sem,
            compute_rows, *, n_blocks):
    s = pl.program_id(0)
    j = pl.program_id(1)
    nj = pl.num_programs(1)
    q = s * nj + j
    slot = q % 2
    tile = stage_ref.shape[2]
    col = pl.multiple_of(j * tile, LANE)
    nb = nblk_ref[s]

    def out_copy(sl, b, row0):
        r0 = pl.multiple_of(b * MOE_BLK, MOE_BLK)
        return pltpu.make_async_copy(
            stage_ref.at[sl, pl.ds(r0, MOE_BLK)],
            o_ref.at[pl.ds(pl.multiple_of(row0 + r0, MOE_BLK), MOE_BLK), pl.ds(col, tile)],
            out_sems.at[sl])

    def drain(sl, count):
        def one(b, c):
            out_copy(sl, 0, 0).wait()
            return c
        lax.fori_loop(0, count, one, 0)

    def zero_copy(b):
        return pltpu.make_async_copy(
            zblk_ref,
            o_ref.at[pl.ds(pl.multiple_of(b * MOE_BLK, MOE_BLK), MOE_BLK), pl.ds(col, tile)],
            zero_sem)

    @pl.when(q >= 2)
    def _():
        drain(slot, nblk_ref[(q - 2) // nj])

    @pl.when(s == 0)
    def _():
        zblk_ref[...] = jnp.zeros_like(zblk_ref)

        def zs(b, c):
            zero_copy(b).start()
            return c
        lax.fori_loop(tot_ref[0], n_blocks, zs, 0)

        def zw(b, c):
            zero_copy(b).wait()
            return c
        lax.fori_loop(tot_ref[0], n_blocks, zw, 0)

    def run(b0, k):
        compute_rows(slot, pl.multiple_of(b0 * MOE_BLK, MOE_BLK), k * MOE_BLK)
        for i in range(k):
            out_copy(slot, b0 + i, start_ref[s]).start()

    k = 1 << (MOE_SB_BLKS.bit_length() - 1)
    while k >= 1:
        @pl.when(nb & k != 0)
        def _(k=k):
            run(nb & ~(2 * k - 1), k)
        k //= 2

    @pl.when(q == pl.num_programs(0) * nj - 1)
    def _():
        drain(slot, nb)

        @pl.when(q >= 1)
        def _():
            drain(1 - slot, nblk_ref[(q - 1) // nj])


def _gate_up_body(e_ref, start_ref, nblk_ref, tot_ref, x_ref, wg_ref, wu_ref, bg_ref, bu_ref,
                  o_ref, wbf_ref, stage_ref, zblk_ref, out_sems, zero_sem, *, n_blocks):
    half = x_ref.shape[1]
    tj = wg_ref.shape[1]

    @pl.when(nblk_ref[pl.program_id(0)] > 0)
    def _():
        wbf_ref[:, :tj] = wg_ref[...].astype(BF16)
        wbf_ref[:, tj:] = wu_ref[...].astype(BF16)

    bias = jnp.concatenate([bg_ref[...], bu_ref[...]], axis=1)

    def compute_rows(slot, r0, n_rows):
        xp = x_ref[pl.ds(r0, n_rows), :]
        x_lo = lax.bitcast_convert_type(xp << 16, F32).astype(BF16)
        x_hi = lax.bitcast_convert_type(xp & jnp.uint32(0xFFFF0000), F32).astype(BF16)
        x = jnp.concatenate([x_lo, x_hi], axis=1)
        gu = jnp.dot(x, wbf_ref[...], preferred_element_type=F32) + bias
        gate = jnp.minimum(gu[:, :tj], SWIGLU_LIMIT)
        up = jnp.clip(gu[:, tj:], -SWIGLU_LIMIT, SWIGLU_LIMIT)
        act = (up + 1.0) * (gate * _sigmoid(gate * SWIGLU_ALPHA))
        stage_ref[slot, pl.ds(r0, n_rows), :] = act.astype(stage_ref.dtype)

    _sb_run(start_ref, nblk_ref, tot_ref, o_ref, stage_ref, zblk_ref, out_sems, zero_sem,
            compute_rows, n_blocks=n_blocks)


def _sb_scratch(rows, tile, dtype):
    return [pltpu.VMEM((2, rows, tile), dtype), pltpu.VMEM((MOE_BLK, tile), dtype),
            pltpu.SemaphoreType.DMA((2,)), pltpu.SemaphoreType.DMA]


def _moe_gate_up(sb, n_sb, xs, w_gu, b_gu, *, tj):
    ne, d, de2 = w_gu.shape
    de = de2 // 2
    nj = de // tj
    rows = MOE_SB_BLKS * MOE_BLK
    n_rows, half = xs.shape
    x_spec, wcol = _sb_specs(rows, half, nj)
    in_specs = [
        x_spec,
        pl.BlockSpec((None, d, tj), lambda s, j, e, st, nb, tt: (e[s], 0, wcol(s, j, nb))),
        pl.BlockSpec((None, d, tj), lambda s, j, e, st, nb, tt: (e[s], 0, nj + wcol(s, j, nb))),
        pl.BlockSpec((None, 1, tj), lambda s, j, e, st, nb, tt: (e[s], 0, wcol(s, j, nb))),
        pl.BlockSpec((None, 1, tj), lambda s, j, e, st, nb, tt: (e[s], 0, nj + wcol(s, j, nb))),
    ]
    return pl.pallas_call(
        functools.partial(_gate_up_body, n_blocks=n_rows // MOE_BLK),
        grid_spec=pltpu.PrefetchScalarGridSpec(
            num_scalar_prefetch=4,
            grid=(n_sb, nj),
            in_specs=in_specs,
            out_specs=pl.BlockSpec(memory_space=pl.ANY),
            scratch_shapes=[pltpu.VMEM((d, 2 * tj), BF16)] + _sb_scratch(rows, tj, BF16),
        ),
        out_shape=jax.ShapeDtypeStruct((n_rows, de), BF16),
        compiler_params=_cparams(2),
        name="moe_gate_up",
    )(*sb, xs, w_gu, w_gu, b_gu.reshape(ne, 1, de2), b_gu.reshape(ne, 1, de2))


def _down_body(e_ref, start_ref, nblk_ref, tot_ref, a_ref, w_ref, b_ref, o_ref, wbf_ref,
               stage_ref, zblk_ref, out_sems, zero_sem, *, n_blocks):
    @pl.when(nblk_ref[pl.program_id(0)] > 0)
    def _():
        wbf_ref[...] = w_ref[...].astype(BF16)

    bias = b_ref[...]

    def compute_rows(slot, r0, n_rows):
        stage_ref[slot, pl.ds(r0, n_rows), :] = jnp.dot(
            a_ref[pl.ds(r0, n_rows), :], wbf_ref[...], preferred_element_type=F32) + bias

    _sb_run(start_ref, nblk_ref, tot_ref, o_ref, stage_ref, zblk_ref, out_sems, zero_sem,
            compute_rows, n_blocks=n_blocks)


def _moe_down(sb, n_sb, act, w_down, b_down, *, tn):
    ne, de, d = w_down.shape
    nn = d // tn
    rows = MOE_SB_BLKS * MOE_BLK
    n_rows = act.shape[0]
    a_spec, wcol = _sb_specs(rows, de, nn)
    in_specs = [
        a_spec,
        pl.BlockSpec((None, de, tn), lambda s, j, e, st, nb, tt: (e[s], 0, wcol(s, j, nb))),
        pl.BlockSpec((None, 1, tn), lambda s, j, e, st, nb, tt: (e[s], 0, wcol(s, j, nb))),
    ]
    return pl.pallas_call(
        functools.partial(_down_body, n_blocks=n_rows // MOE_BLK),
        grid_spec=pltpu.PrefetchScalarGridSpec(
            num_scalar_prefetch=4,
            grid=(n_sb, nn),
            in_specs=in_specs,
            out_specs=pl.BlockSpec(memory_space=pl.ANY),
            scratch_shapes=[pltpu.VMEM((de, tn), BF16)] + _sb_scratch(rows, tn, F32),
        ),
        out_shape=jax.ShapeDtypeStruct((n_rows, d), F32),
        compiler_params=_cparams(2),
        name="moe_down",
    )(*sb, act, w_down, b_down.reshape(ne, 1, d))


def _moe_schedule(cnt, n_pairs, ne):
    n_blocks = n_pairs // MOE_BLK + ne + MOE_SB_BLKS - 1
    s_max = -(-(n_pairs // MOE_BLK + ne) // MOE_SB_BLKS) + ne
    pblk = (cnt + MOE_BLK - 1) // MOE_BLK
    seg_end_blk = jnp.cumsum(pblk)
    seg_start = (seg_end_blk - pblk) * MOE_BLK
    total_blk = seg_end_blk[-1:]
    nsb_e = (pblk + MOE_SB_BLKS - 1) // MOE_SB_BLKS
    sb_end = jnp.cumsum(nsb_e)
    sb_first = sb_end - nsb_e
    n_sb = sb_end[-1]
    s_ids = jnp.arange(s_max, dtype=I32)
    used = s_ids < n_sb
    e_of_s = jnp.minimum(jnp.searchsorted(sb_end, s_ids, side="right"), ne - 1).astype(I32)
    i_in_e = s_ids - sb_first[e_of_s]
    start = seg_start[e_of_s] + i_in_e * (MOE_SB_BLKS * MOE_BLK)
    nblk = jnp.where(used, jnp.clip(pblk[e_of_s] - MOE_SB_BLKS * i_in_e, 0, MOE_SB_BLKS), 0)
    last = jnp.maximum(n_sb - 1, 0)
    sb = (jnp.where(used, e_of_s, e_of_s[last]).astype(I32),
          jnp.where(used, start, start[last]).astype(I32), nblk.astype(I32),
          total_blk.astype(I32))
    return sb, n_sb.astype(I32), seg_start.astype(I32), n_blocks * MOE_BLK


def _combine_body(dest_ref, y_ref, h_ref, p_ref, g_ref, o_ref, ybuf, sems, *, t_total):
    i = pl.program_id(0)
    n = pl.num_programs(0)
    tc, d = h_ref.shape
    groups = tc // SUBLANE

    def row_copy(slot, k, g, u, row):
        return pltpu.make_async_copy(y_ref.at[pl.ds(row, 1)], ybuf.at[slot, k, g, pl.ds(u, 1)],
                                     sems.at[slot])

    def issue(tile, slot):
        def one(g, carry):
            for k in range(TOP_K):
                base = k * t_total + tile * tc + g * SUBLANE
                for u in range(SUBLANE):
                    row_copy(slot, k, g, u, dest_ref[base + u]).start()
            return carry
        lax.fori_loop(0, groups, one, 0, unroll=4)

    @pl.when(i == 0)
    def _():
        for t0 in range(COMBINE_BUFS - 1):
            @pl.when(t0 < n)
            def _(t0=t0):
                issue(t0, t0)

    slot = i % COMBINE_BUFS

    def drain(g, carry):
        for _ in range(TOP_K * SUBLANE):
            row_copy(slot, 0, 0, 0, 0).wait()
        return carry
    lax.fori_loop(0, groups, drain, 0)

    @pl.when(i + COMBINE_BUFS - 1 < n)
    def _():
        issue(i + COMBINE_BUFS - 1, (i + COMBINE_BUFS - 1) % COMBINE_BUFS)

    p = p_ref[...]
    cw = min(d, 4 * LANE)
    ssq = jnp.zeros((tc, 1), F32)
    for c in range(d // cw):
        cols = slice(c * cw, (c + 1) * cw)
        acc = h_ref[:, cols]
        for k in range(TOP_K):
            acc = acc + p[:, k:k + 1] * ybuf[slot, k, :, :, cols].reshape(tc, cw)
        o_ref[:, cols] = acc
        ssq = ssq + jnp.sum(acc * acc, axis=-1, keepdims=True)
    scale = lax.rsqrt(ssq * (1.0 / d) + EPS)
    for c in range(d // cw):
        cols = slice(c * cw, (c + 1) * cw)
        o_ref[:, cols] = o_ref[:, cols] * scale * g_ref[:, cols]


def _combine(dest_flat, y_sorted, hres, p_t, final_g, *, tc):
    t, d = hres.shape
    return pl.pallas_call(
        functools.partial(_combine_body, t_total=t),
        grid_spec=pltpu.PrefetchScalarGridSpec(
            num_scalar_prefetch=1,
            grid=(t // tc,),
            in_specs=[pl.BlockSpec(memory_space=pl.ANY),
                      pl.BlockSpec((tc, d), lambda i, *_: (i, 0)),
                      pl.BlockSpec((tc, TOP_K), lambda i, *_: (i, 0)),
                      pl.BlockSpec((1, d), lambda i, *_: (0, 0))],
            out_specs=pl.BlockSpec((tc, d), lambda i, *_: (i, 0)),
            scratch_shapes=[pltpu.VMEM((COMBINE_BUFS, TOP_K, tc // SUBLANE, SUBLANE, d), F32),
                            pltpu.SemaphoreType.DMA((COMBINE_BUFS,))],
        ),
        out_shape=jax.ShapeDtypeStruct((t, d), F32),
        compiler_params=_cparams(1),
        name="moe_combine",
    )(dest_flat, y_sorted, hres, p_t, final_g.reshape(1, d))


def _pick(n, prefs):
    for p in prefs:
        if n % p == 0:
            return p
    raise ValueError(f"no tile in {prefs} divides {n}")


def kernel(x, meta_tokens, norm_mix_g, w_in, ssd_conv_w, ssd_conv_b, dt_bias, a_log, d_skip,
           ssd_norm_g, w_ssd_out, sc_conv_w, w_sc_out, w_o, norm_ffn_g, w_router, b_router,
           w_gate_up, b_gate_up, w_down, b_down, final_g):
    batch, seq, d = x.shape
    assert norm_mix_g.shape[0] == 1, "meta-token handling assumes a single layer"
    assert seq % SSD_CHUNK == 0 and N_META <= SSD_CHUNK
    t = batch * seq
    heads = d // SSD_HEAD_DIM
    gn = SSD_GROUPS * SSD_STATE
    conv_dim = d + 2 * gn
    xbc_off = d
    dt_off = xbc_off + conv_dim
    sc_off = dt_off + heads
    gate_off = sc_off + 3 * d
    ne = w_router.shape[-1]
    de = w_down.shape[2]

    x2 = x.reshape(t, d)
    w_in_nk = jnp.swapaxes(w_in[0], 0, 1)

    tm = _pick(seq, (1024, 512, 256, 128))
    tn = _pick(d, (512, 256, 128))

    h = _rmsnorm_bf16(x2, norm_mix_g[0], _pick(t, (256, 128)))
    h_meta = _rmsnorm_bf16(meta_tokens, norm_mix_g[0], N_META)

    proj = functools.partial(_matmul, h, w_in_nk, tm=tm, out_dtype=F32, w_is_nk=True)
    z = proj(0, d, tn=tn, name="proj_z")
    xbc, xbc_meta = _proj_conv(h, w_in_nk, [xbc_off], h_meta, ssd_conv_w[0],
                               ssd_conv_b[0].reshape(1, conv_dim), width=conv_dim, tm=tm,
                               tn=tn, seq=seq, mode="xbc", name="proj_xbc")
    dtr, dt_meta = proj(dt_off, LANE, tn=LANE, side=h_meta, name="proj_dt")
    u_sc = _proj_conv(h, w_in_nk, [sc_off, sc_off + d, sc_off + 2 * d], h_meta,
                      sc_conv_w[0], None, width=d, tm=tm, tn=_pick(d, (256, 128)), seq=seq,
                      mode="sc", name="proj_sc")
    gates = proj(gate_off, 2 * d, tn=tn, name="proj_gate")

    e_mat = (jnp.arange(d, dtype=I32)[None, :] // SSD_HEAD_DIM
             == jnp.arange(heads, dtype=I32)[:, None]).astype(BF16)
    e2 = jnp.concatenate([e_mat, e_mat], axis=0)
    pad = SSD_CHUNK - N_META
    dtb = dt_bias[0].reshape(1, heads)
    alog = a_log[0].reshape(1, heads)
    s0 = _ssd_meta(jnp.pad(xbc_meta, ((pad, 0), (0, 0))),
                   jnp.pad(dt_meta, ((pad, 0), (0, 0))), dtb, alog, e2, heads)
    dskip_x = jnp.repeat(d_skip[0], SSD_HEAD_DIM).reshape(1, d)
    g_ssd = _ssd_main(xbc, dtr, z, s0, dtb, alog, dskip_x, ssd_norm_g[0].reshape(1, d), e2,
                      heads=heads, batch=batch)

    y_a = _matmul(g_ssd, w_ssd_out[0], 0, d, tm=tm, tn=tn, out_dtype=F32,
                  extras=[(gates, 0)],
                  epilogue=lambda acc, ga: _sigmoid(ga) * acc, name="ssd_out")
    mixed = _matmul(u_sc, w_sc_out[0], 0, d, tm=tm, tn=tn, out_dtype=BF16,
                    extras=[(gates, d), (y_a, 0)],
                    epilogue=lambda acc, gb, ya: ya + _sigmoid(gb) * acc, name="sc_out")
    hres = _matmul(mixed, w_o[0], 0, d, tm=tm, tn=tn, out_dtype=F32,
                   extras=[(x2, 0)], epilogue=lambda acc, xr: xr + acc, name="w_o")

    rt = _pick(t, (512, 256, 128))
    hp, top_idx, probs, rank, counts = _router(hres, norm_ffn_g[0], w_router[0].T,
                                               b_router[0], rt)
    cnt = counts[:, 0].astype(I32)
    sb, n_sb, seg_start, n_slots = _moe_schedule(cnt, t * TOP_K, ne)
    e_ids = jnp.arange(ne, dtype=I32)[:, None, None]
    dest = rank + jnp.sum(jnp.where(top_idx[None] == e_ids, seg_start[:, None, None], 0),
                          axis=0)
    dest_flat = dest.reshape(-1)

    xs = _moe_dispatch(dest_flat, cnt, seg_start, sb[3], hp, n_slots,
                       tm=_pick(t, (1024, 512, 256, 128)))
    act = _moe_gate_up(sb, n_sb, xs, w_gate_up[0], b_gate_up[0], tj=_pick(de, (256, 128)))
    y_sorted = _moe_down(sb, n_sb, act, w_down[0], b_down[0],
                         tn=_pick(d, (1024, 512, 256, 128)))
    out = _combine(dest_flat, y_sorted, hres, probs.T, final_g, tc=_pick(t, (128,)))
    return out.reshape(batch, seq, d)
```

```python
import functools

import jax
import jax.numpy as jnp
from jax import lax
from jax.experimental import pallas as pl
from jax.experimental.pallas import tpu as pltpu

F32 = jnp.float32
BF16 = jnp.bfloat16
I32 = jnp.int32
U32 = jnp.uint32

N_META = 16
SSD_HEAD_DIM = 64
SSD_GROUPS = 8
SSD_STATE = 128
SSD_CONV = 4
SSD_CHUNK = 128
SC_CONV = 3
TOP_K = 4
SWIGLU_LIMIT = 7.0
SWIGLU_ALPHA = 1.702
EPS = 1e-5

LANE = 128
SUBLANE = 8
MOE_BLK = 128
MOE_SB_BLKS = 10
EPILOGUE_ROWS = 256
COMBINE_BUFS = 3
VMEM_MB = 56
VMEM_MB_WIDE = 60


def _cparams(n_axes, vmem_mb=VMEM_MB):
    return pltpu.CompilerParams(
        dimension_semantics=("arbitrary",) * n_axes,
        vmem_limit_bytes=vmem_mb * 1024 * 1024,
    )


def _sigmoid(v):
    return 1.0 / (1.0 + jnp.exp(-v))


def _rms_body(x_ref, g_ref, o_ref):
    x = x_ref[...]
    ms = jnp.mean(x * x, axis=-1, keepdims=True)
    o_ref[...] = (x * lax.rsqrt(ms + EPS) * g_ref[...]).astype(o_ref.dtype)


def _rmsnorm_bf16(x2, g, tm):
    m, d = x2.shape
    return pl.pallas_call(
        _rms_body,
        grid=(m // tm,),
        in_specs=[pl.BlockSpec((tm, d), lambda i: (i, 0)),
                  pl.BlockSpec((1, d), lambda i: (0, 0))],
        out_specs=pl.BlockSpec((tm, d), lambda i: (i, 0)),
        out_shape=jax.ShapeDtypeStruct((m, d), BF16),
        compiler_params=_cparams(1),
        name="rmsnorm",
    )(x2, g.reshape(1, d))


def _chunked_matmul(a_ref, wbf_ref, acc_ref, finish):
    tm = a_ref.shape[0]
    ch = acc_ref.shape[1]
    n = tm // ch

    def rows(c):
        return slice(c * ch, (c + 1) * ch)

    for c in range(n):
        if c > 0:
            finish(rows(c - 1), acc_ref[(c - 1) % 2])
        acc_ref[c % 2] = jnp.dot(a_ref[rows(c), :], wbf_ref[...], preferred_element_type=F32)
    finish(rows(n - 1), acc_ref[(n - 1) % 2])


def _mm_body(*refs, n_extra, has_side, epilogue, w_is_nk):
    it = iter(refs)
    a_ref = next(it)
    w_ref = next(it)
    side_ref = next(it) if has_side else None
    extras = [next(it) for _ in range(n_extra)]
    o_ref = next(it)
    oside_ref = next(it) if has_side else None
    plain = epilogue is None and not has_side
    wbf_ref = None if plain else next(it)
    acc_ref = next(it) if epilogue is not None else None

    def w_bf16():
        w = w_ref[...]
        return (w.T if w_is_nk else w).astype(BF16)

    if plain:
        o_ref[...] = jnp.dot(a_ref[...], w_bf16(),
                             preferred_element_type=F32).astype(o_ref.dtype)
        return

    @pl.when(pl.program_id(1) == 0)
    def _():
        wbf_ref[...] = w_bf16()
        if has_side:
            oside_ref[...] = jnp.dot(side_ref[...], wbf_ref[...],
                                     preferred_element_type=F32)

    if epilogue is None:
        o_ref[...] = jnp.dot(a_ref[...], wbf_ref[...],
                             preferred_element_type=F32).astype(o_ref.dtype)
    else:
        def finish(rows, acc):
            o_ref[rows, :] = epilogue(acc, *[e[rows, :] for e in extras]).astype(o_ref.dtype)
        _chunked_matmul(a_ref, wbf_ref, acc_ref, finish)


def _matmul(a, w, col_off, n_cols, *, tm, tn, out_dtype, side=None, extras=(),
            epilogue=None, w_is_nk=False, name="matmul"):
    m, k = a.shape
    assert m % tm == 0 and n_cols % tn == 0
    if w_is_nk:
        assert col_off % SUBLANE == 0
        w_spec = pl.BlockSpec((pl.Element(tn), pl.Element(k)),
                              lambda n, i: (pl.multiple_of(col_off + n * tn, SUBLANE), 0))
    else:
        assert col_off % tn == 0
        w_spec = pl.BlockSpec((k, tn), lambda n, i, cb=col_off // tn: (0, n + cb))
    in_specs = [pl.BlockSpec((tm, k), lambda n, i: (i, 0)), w_spec]
    args = [a, w]
    if side is not None:
        in_specs.append(pl.BlockSpec((side.shape[0], k), lambda n, i: (0, 0)))
        args.append(side)
    for arr, off in extras:
        assert off % tn == 0
        in_specs.append(pl.BlockSpec((tm, tn), lambda n, i, o=off // tn: (i, n + o)))
        args.append(arr)
    out_shape = [jax.ShapeDtypeStruct((m, n_cols), out_dtype)]
    out_specs = [pl.BlockSpec((tm, tn), lambda n, i: (i, n))]
    if side is not None:
        out_shape.append(jax.ShapeDtypeStruct((side.shape[0], n_cols), F32))
        out_specs.append(pl.BlockSpec((side.shape[0], tn), lambda n, i: (0, n)))
    res = pl.pallas_call(
        functools.partial(_mm_body, n_extra=len(extras), has_side=side is not None,
                          epilogue=epilogue, w_is_nk=w_is_nk),
        grid=(n_cols // tn, m // tm),
        in_specs=in_specs,
        out_specs=out_specs,
        out_shape=out_shape,
        scratch_shapes=([] if epilogue is None and side is None
                        else [pltpu.VMEM((k, tn), BF16)])
        + ([pltpu.VMEM((2, min(tm, EPILOGUE_ROWS), tn), F32)] if epilogue is not None else []),
        compiler_params=_cparams(2),
        name=name,
    )(*args)
    return res if side is not None else res[0]


def _causal_conv(raw, prev, w, n_tap):
    acc = raw * w[n_tap - 1:n_tap, :]
    row8 = lax.broadcasted_iota(I32, (SUBLANE, raw.shape[1]), 0)
    for s in range(1, n_tap):
        sh = pltpu.roll(raw, s, 0)
        shp = pltpu.roll(prev, s, 0)
        first = jnp.where(row8 < s, shp, sh[:SUBLANE])
        sh = jnp.concatenate([first, sh[SUBLANE:]], axis=0)
        acc = acc + sh * w[n_tap - 1 - s:n_tap - s, :]
    return acc


def _proj_conv_body(*refs, n_w, n_tap, tiles_per_batch, mode):
    it = iter(refs)
    a_ref = next(it)
    w_refs = [next(it) for _ in range(n_w)]
    side_ref = next(it)
    cw_ref = next(it)
    cb_ref = next(it) if mode == "xbc" else None
    o_ref = next(it)
    oside_ref = next(it) if mode == "xbc" else None
    wbf_ref = next(it)
    prev_ref = next(it)
    tail_ref = next(it)
    u_ref = next(it)
    tn = o_ref.shape[1]
    cw = cw_ref[...]

    def pre(acc):
        return acc if mode == "xbc" else acc[:, tn:2 * tn] * acc[:, 2 * tn:]

    def post(y, acc):
        if mode == "xbc":
            v = y + cb_ref[...]
            return v * _sigmoid(v)
        return acc[:, :tn] * y

    m = pl.program_id(1)

    @pl.when(m == 0)
    def _():
        for i, w_ref in enumerate(w_refs):
            wbf_ref[:, i * tn:(i + 1) * tn] = w_ref[...].T.astype(BF16)
        acc_m = jnp.dot(side_ref[...], wbf_ref[...], preferred_element_type=F32)
        u_m = pre(acc_m)
        tail_ref[...] = u_m[N_META - SUBLANE:, :]
        if mode == "xbc":
            y_m = _causal_conv(u_m, jnp.zeros((SUBLANE, tn), F32), cw, n_tap)
            oside_ref[...] = post(y_m, acc_m)

    @pl.when(m % tiles_per_batch == 0)
    def _():
        prev_ref[...] = tail_ref[...]

    ch = u_ref.shape[0] - SUBLANE
    for c in range(a_ref.shape[0] // ch):
        rows = slice(c * ch, (c + 1) * ch)
        acc = jnp.dot(a_ref[rows, :], wbf_ref[...], preferred_element_type=F32)
        u_ref[:SUBLANE, :] = prev_ref[...]
        u_ref[SUBLANE:, :] = pre(acc)
        y = None
        for k in range(n_tap):
            r0 = SUBLANE - (n_tap - 1) + k
            term = u_ref[r0:r0 + ch, :] * cw[k:k + 1, :]
            y = term if y is None else y + term
        prev_ref[...] = u_ref[ch:, :]
        o_ref[rows, :] = post(y, acc).astype(o_ref.dtype)


def _proj_conv(a, w_nk, row_offs, side, cw, cb, *, width, tm, tn, seq, mode, name):
    m, k = a.shape
    n_w = len(row_offs)
    n_tap = cw.shape[0]
    ms = side.shape[0]

    def w_spec(off):
        return pl.BlockSpec((pl.Element(tn), pl.Element(k)),
                            lambda n, i: (pl.multiple_of(off + n * tn, SUBLANE), 0))

    in_specs = [pl.BlockSpec((tm, k), lambda n, i: (i, 0))]
    in_specs += [w_spec(off) for off in row_offs]
    in_specs += [pl.BlockSpec((ms, k), lambda n, i: (0, 0)),
                 pl.BlockSpec((n_tap, tn), lambda n, i: (0, n))]
    args = [a] + [w_nk] * n_w + [side, cw]
    out_shape = [jax.ShapeDtypeStruct((m, width), F32 if mode == "xbc" else BF16)]
    out_specs = [pl.BlockSpec((tm, tn), lambda n, i: (i, n))]
    if mode == "xbc":
        in_specs.append(pl.BlockSpec((1, tn), lambda n, i: (0, n)))
        args.append(cb)
        out_shape.append(jax.ShapeDtypeStruct((ms, width), F32))
        out_specs.append(pl.BlockSpec((ms, tn), lambda n, i: (0, n)))
    res = pl.pallas_call(
        functools.partial(_proj_conv_body, n_w=n_w, n_tap=n_tap,
                          tiles_per_batch=seq // tm, mode=mode),
        grid=(width // tn, m // tm),
        in_specs=in_specs,
        out_specs=out_specs,
        out_shape=out_shape,
        scratch_shapes=[pltpu.VMEM((k, n_w * tn), BF16),
                        pltpu.VMEM((SUBLANE, tn), F32),
                        pltpu.VMEM((SUBLANE, tn), F32),
                        pltpu.VMEM((min(tm, EPILOGUE_ROWS) + SUBLANE, tn), F32)],
        compiler_params=_cparams(2, VMEM_MB if n_w == 1 else VMEM_MB_WIDE),
        name=name,
    )(*args)
    return res if mode == "xbc" else res[0]


def _ssd_chunk(xbc, dt_raw, dtb, alog, e2, *, heads, meta):
    L = xbc.shape[0]
    di = heads * SSD_HEAD_DIM
    gn = SSD_GROUPS * SSD_STATE
    dt_in = dt_raw[:, :heads] + dtb
    dtv = jnp.maximum(dt_in, 0.0) + jnp.log1p(jnp.exp(-jnp.abs(dt_in)))
    if meta:
        dtv = dtv * (lax.broadcasted_iota(I32, (L, 1), 0) >= L - N_META).astype(F32)
    xs = xbc[:, :di]
    bm = xbc[:, di:di + gn]
    cm = xbc[:, di + gn:]
    a_neg = -jnp.exp(alog)
    ad = dtv * a_neg
    r = lax.broadcasted_iota(I32, (L, L), 0)
    c = lax.broadcasted_iota(I32, (L, L), 1)
    tril = r >= c
    acum = jnp.dot(tril.astype(F32), ad, precision=lax.Precision.HIGHEST,
                   preferred_element_type=F32)
    stack = jnp.concatenate([dtv, dtv * jnp.exp(acum[L - 1:L, :] - acum), jnp.exp(acum)],
                            axis=0)
    hi = stack.astype(BF16)
    lo = (stack - hi.astype(F32)).astype(BF16)
    exp_x = jnp.dot(jnp.concatenate([hi, lo], axis=1), e2, preferred_element_type=F32)
    dt_x, dd_x, ea_x = exp_x[:L], exp_x[L:2 * L], exp_x[2 * L:]
    bt = bm.T.astype(BF16)
    return dict(xs=xs, cm=cm, bt=bt, acum=acum, ea_x=ea_x, x_dt=(xs * dt_x).astype(BF16),
                x_dec=(xs * dd_x).astype(BF16), tril=tril)


def _ssd_state_update(q, st_ref, heads):
    gw = heads * SSD_HEAD_DIM // SSD_GROUPS
    L = q["ea_x"].shape[0]
    for g in range(SSD_GROUPS):
        sl = slice(g * gw, (g + 1) * gw)
        btg = q["bt"][g * SSD_STATE:(g + 1) * SSD_STATE, :]
        new = jnp.dot(btg, q["x_dec"][:, sl], preferred_element_type=F32)
        st_ref[:, sl] = st_ref[:, sl] * q["ea_x"][L - 1:L, sl] + new


def _ssd_body(xbc_ref, dt_ref, z_ref, s0_ref, dtb_ref, alog_ref, dskip_ref, ng_ref, e_ref,
              o_ref, st_ref, *, heads):
    @pl.when(pl.program_id(1) == 0)
    def _():
        st_ref[...] = s0_ref[...]

    L = xbc_ref.shape[0]
    q = _ssd_chunk(xbc_ref[...], dt_ref[...], dtb_ref[...], alog_ref[...], e_ref[...],
                   heads=heads, meta=False)

    gw = heads * SSD_HEAD_DIM // SSD_GROUPS
    j_heads = heads // SSD_GROUPS
    acum = q["acum"]
    acum_t = acum.T
    cmb = q["cm"].astype(BF16)
    x_dt_b = q["x_dt"]
    lane = lax.broadcasted_iota(I32, (L, 2 * SSD_HEAD_DIM), 1)
    lo = lane < SSD_HEAD_DIM
    neg_inf = jnp.float32(-jnp.inf)

    def decay_mat(h):
        col = jnp.broadcast_to(acum[:, h:h + 1], (L, L))
        row = jnp.broadcast_to(acum_t[h:h + 1, :], (L, L))
        return jnp.exp(jnp.where(q["tril"], col - row, neg_inf))

    for g in range(SSD_GROUPS):
        sl = slice(g * gw, (g + 1) * gw)
        cg = cmb[:, g * SSD_STATE:(g + 1) * SSD_STATE]
        btg = q["bt"][g * SSD_STATE:(g + 1) * SSD_STATE, :]
        cb_mat = jnp.dot(cg, btg, preferred_element_type=F32)
        parts = []
        for p in range(j_heads // 2):
            h1 = g * j_heads + 2 * p
            slab = x_dt_b[:, h1 * SSD_HEAD_DIM:(h1 + 2) * SSD_HEAD_DIM]
            lhs = jnp.concatenate([(cb_mat * decay_mat(h1)).astype(BF16),
                                   (cb_mat * decay_mat(h1 + 1)).astype(BF16)], axis=1)
            zero = jnp.zeros_like(slab)
            rhs = jnp.concatenate([jnp.where(lo, slab, zero),
                                   jnp.where(lo, zero, slab)], axis=0)
            parts.append(jnp.dot(lhs, rhs, preferred_element_type=F32))
        y_diag = parts[0] if len(parts) == 1 else jnp.concatenate(parts, axis=1)
        st_g = st_ref[:, sl]
        y_off = jnp.dot(cg, st_g.astype(BF16), preferred_element_type=F32)
        y = y_diag + y_off * q["ea_x"][:, sl] + dskip_ref[:, sl] * q["xs"][:, sl]
        zg = z_ref[:, sl]
        gated = y * (zg * _sigmoid(zg))
        ms = jnp.mean(gated * gated, axis=-1, keepdims=True)
        o_ref[:, sl] = (gated * lax.rsqrt(ms + EPS) * ng_ref[:, sl]).astype(o_ref.dtype)
        new = jnp.dot(btg, q["x_dec"][:, sl], preferred_element_type=F32)
        st_ref[:, sl] = st_g * q["ea_x"][L - 1:L, sl] + new


def _ssd_meta_body(xbc_ref, dt_ref, dtb_ref, alog_ref, e_ref, st_ref, *, heads):
    st_ref[...] = jnp.zeros_like(st_ref)
    q = _ssd_chunk(xbc_ref[...], dt_ref[...], dtb_ref[...], alog_ref[...], e_ref[...],
                   heads=heads, meta=True)
    _ssd_state_update(q, st_ref, heads)


def _full_spec(shape):
    nd = len(shape)
    return pl.BlockSpec(shape, lambda *_: (0,) * nd)


def _ssd_meta(xbc_m, dt_m, dtb, alog, e2, heads):
    di = heads * SSD_HEAD_DIM
    args = [xbc_m, dt_m, dtb, alog, e2]
    return pl.pallas_call(
        functools.partial(_ssd_meta_body, heads=heads),
        grid=(1,),
        in_specs=[_full_spec(a.shape) for a in args],
        out_specs=_full_spec((SSD_STATE, di)),
        out_shape=jax.ShapeDtypeStruct((SSD_STATE, di), F32),
        compiler_params=_cparams(1),
        name="ssd_meta",
    )(*args)


def _ssd_main(xbc, dt, z, s0, dtb, alog, dskip_x, ng, e2, *, heads, batch):
    t, cd = xbc.shape
    di = heads * SSD_HEAD_DIM
    L = SSD_CHUNK
    nc = t // batch // L
    row = lambda b, c: (b * nc + c, 0)
    in_specs = [pl.BlockSpec((L, cd), row),
                pl.BlockSpec((L, LANE), row),
                pl.BlockSpec((L, di), row)]
    consts = [s0, dtb, alog, dskip_x, ng, e2]
    in_specs += [_full_spec(a.shape) for a in consts]
    return pl.pallas_call(
        functools.partial(_ssd_body, heads=heads),
        grid=(batch, nc),
        in_specs=in_specs,
        out_specs=pl.BlockSpec((L, di), row),
        out_shape=jax.ShapeDtypeStruct((t, di), BF16),
        scratch_shapes=[pltpu.VMEM((SSD_STATE, di), F32)],
        compiler_params=_cparams(2),
        name="ssd_main",
    )(xbc, dt, z, *consts)


def _router_body(x_ref, g_ref, wr_ref, br_ref, hp_ref, idx_ref, prob_ref, rank_ref,
                 cnt_ref, base_ref):
    @pl.when(pl.program_id(0) == 0)
    def _():
        base_ref[...] = jnp.zeros_like(base_ref)

    x = x_ref[...]
    tm, d = x.shape
    ne = wr_ref.shape[0]
    ms = jnp.mean(x * x, axis=-1, keepdims=True)
    h = x * lax.rsqrt(ms + EPS) * g_ref[...]
    half = d // 2
    h_hi = h.astype(BF16)
    h_hi32 = h_hi.astype(F32)
    lo_bits = lax.bitcast_convert_type(h_hi32[:, :half], U32)
    hi_bits = lax.bitcast_convert_type(h_hi32[:, half:], U32)
    hp_ref[...] = (lo_bits >> 16) | (hi_bits & jnp.uint32(0xFFFF0000))

    w = wr_ref[...]
    w_hi = w.astype(BF16)
    w_lo = (w - w_hi.astype(F32)).astype(BF16)
    h_lo = (h - h_hi32).astype(BF16)
    nt = (((1,), (1,)), ((), ()))
    part = lax.dot_general(jnp.concatenate([w_hi, w_lo], axis=0), h_hi, nt,
                           preferred_element_type=F32)
    logits = (part[:ne] + part[ne:]
              + lax.dot_general(w_hi, h_lo, nt, preferred_element_type=F32)
              + br_ref[:, 0:1])
    eidx = lax.broadcasted_iota(I32, (ne, tm), 0)
    work = logits
    vals, onehots = [], []
    for k in range(TOP_K):
        mx = jnp.max(work, axis=0, keepdims=True)
        sel = jnp.min(jnp.where(work == mx, eidx, ne), axis=0, keepdims=True)
        oh = eidx == sel
        idx_ref[k:k + 1, :] = sel
        vals.append(mx)
        onehots.append(oh)
        work = jnp.where(oh, -jnp.inf, work)
    exps = [jnp.exp(v - vals[0]) for v in vals]
    denom = exps[0]
    for e in exps[1:]:
        denom = denom + e
    for k in range(TOP_K):
        prob_ref[k:k + 1, :] = exps[k] / denom

    cnt = onehots[0].astype(F32)
    for oh in onehots[1:]:
        cnt = cnt + oh.astype(F32)
    r = lax.broadcasted_iota(I32, (tm, tm), 0)
    c = lax.broadcasted_iota(I32, (tm, tm), 1)
    before = (r < c).astype(BF16)
    prefix = jnp.dot(cnt.astype(BF16), before, preferred_element_type=F32)
    prefix = prefix + base_ref[:, 0:1]
    for k in range(TOP_K):
        rk = jnp.sum(jnp.where(onehots[k], prefix, 0.0), axis=0, keepdims=True)
        rank_ref[k:k + 1, :] = rk.astype(I32)
    base_ref[...] = base_ref[...] + jnp.sum(cnt, axis=1, keepdims=True)
    cnt_ref[...] = base_ref[...]


def _router(hres, g, wr_t, br, tm):
    t, d = hres.shape
    ne = wr_t.shape[0]
    tok = lambda i: (0, i)
    return pl.pallas_call(
        _router_body,
        grid=(t // tm,),
        in_specs=[pl.BlockSpec((tm, d), lambda i: (i, 0)),
                  _full_spec((1, d)), _full_spec((ne, d)), _full_spec((ne, LANE))],
        out_specs=[pl.BlockSpec((tm, d // 2), lambda i: (i, 0)),
                   pl.BlockSpec((TOP_K, tm), tok),
                   pl.BlockSpec((TOP_K, tm), tok),
                   pl.BlockSpec((TOP_K, tm), tok),
                   _full_spec((ne, LANE))],
        out_shape=[jax.ShapeDtypeStruct((t, d // 2), U32),
                   jax.ShapeDtypeStruct((TOP_K, t), I32),
                   jax.ShapeDtypeStruct((TOP_K, t), F32),
                   jax.ShapeDtypeStruct((TOP_K, t), I32),
                   jax.ShapeDtypeStruct((ne, LANE), F32)],
        scratch_shapes=[pltpu.VMEM((ne, LANE), F32)],
        compiler_params=_cparams(1),
        name="router",
    )(hres, g.reshape(1, d), wr_t, jnp.broadcast_to(br.reshape(ne, 1), (ne, LANE)))


def _dispatch_body(dest_ref, cnt_ref, seg_ref, tot_ref, hp_ref, xs_ref, zrow_ref, zblk_ref,
                   row_sem, pad_sem, *, t_total, ne, n_blocks):
    i = pl.program_id(0)
    tm = hp_ref.shape[0] * SUBLANE

    def pad_copy(slot):
        return pltpu.make_async_copy(zrow_ref.at[pl.ds(0, 1)], xs_ref.at[pl.ds(slot, 1)], pad_sem)

    def tail_copy(b):
        return pltpu.make_async_copy(
            zblk_ref, xs_ref.at[pl.ds(pl.multiple_of(b * MOE_BLK, MOE_BLK), MOE_BLK)], pad_sem)

    @pl.when(i == 0)
    def _():
        zrow_ref[...] = jnp.zeros_like(zrow_ref)
        zblk_ref[...] = jnp.zeros_like(zblk_ref)

        def expert(e, carry):
            used = cnt_ref[e]
            end = (used + MOE_BLK - 1) // MOE_BLK * MOE_BLK

            def start(r, c):
                pad_copy(seg_ref[e] + r).start()
                return c
            lax.fori_loop(used, end, start, 0)

            def wait(r, c):
                pad_copy(seg_ref[e] + r).wait()
                return c
            lax.fori_loop(used, end, wait, 0)
            return carry
        lax.fori_loop(0, ne, expert, 0)

        def tail_start(b, c):
            tail_copy(b).start()
            return c
        lax.fori_loop(tot_ref[0], n_blocks, tail_start, 0)

        def tail_wait(b, c):
            tail_copy(b).wait()
            return c
        lax.fori_loop(tot_ref[0], n_blocks, tail_wait, 0)

    def row_copy(g, u, slot):
        return pltpu.make_async_copy(hp_ref.at[g, pl.ds(u, 1)], xs_ref.at[pl.ds(slot, 1)],
                                     row_sem)

    def issue(g, c):
        for k in range(TOP_K):
            base = k * t_total + i * tm + g * SUBLANE
            for u in range(SUBLANE):
                row_copy(g, u, dest_ref[base + u]).start()
        return c
    lax.fori_loop(0, tm // SUBLANE, issue, 0, unroll=4)

    def drain(g, c):
        for _ in range(TOP_K * SUBLANE):
            row_copy(0, 0, 0).wait()
        return c
    lax.fori_loop(0, tm // SUBLANE, drain, 0)


def _moe_dispatch(dest_flat, cnt, seg_start, total_blk, hp, n_rows, *, tm):
    t, w = hp.shape
    ne = cnt.shape[0]
    return pl.pallas_call(
        functools.partial(_dispatch_body, t_total=t, ne=ne, n_blocks=n_rows // MOE_BLK),
        grid_spec=pltpu.PrefetchScalarGridSpec(
            num_scalar_prefetch=4,
            grid=(t // tm,),
            in_specs=[pl.BlockSpec((tm // SUBLANE, SUBLANE, w), lambda i, *_: (i, 0, 0))],
            out_specs=pl.BlockSpec(memory_space=pl.ANY),
            scratch_shapes=[pltpu.VMEM((SUBLANE, w), U32),
                            pltpu.VMEM((MOE_BLK, w), U32),
                            pltpu.SemaphoreType.DMA,
                            pltpu.SemaphoreType.DMA],
        ),
        out_shape=jax.ShapeDtypeStruct((n_rows, w), U32),
        compiler_params=_cparams(1),
        name="moe_dispatch",
    )(dest_flat, cnt, seg_start, total_blk, hp.reshape(t // SUBLANE, SUBLANE, w))


def _sb_specs(rows, width_in, n_tiles):
    def x_map(s, j, e, start, nblk, tot):
        return (pl.multiple_of(start[s], MOE_BLK), 0)

    def wcol(s, j, nblk):
        return jnp.where(nblk[s] > 0, j, n_tiles - 1)

    return pl.BlockSpec((pl.Element(rows), pl.Element(width_in)), x_map), wcol


def _sb_run(start_ref, nblk_ref, tot_ref, o_ref, stage_ref, zblk_ref, out_sems, zero_sem,
            compute_rows, *, n_blocks):
    s = pl.program_id(0)
    j = pl.program_id(1)
    nj = pl.num_programs(1)
    q = s * nj + j
    slot = q % 2
    tile = stage_ref.shape[2]
    col = pl.multiple_of(j * tile, LANE)
    nb = nblk_ref[s]

    def out_copy(sl, b, row0):
        r0 = pl.multiple_of(b * MOE_BLK, MOE_BLK)
        return pltpu.make_async_copy(
            stage_ref.at[sl, pl.ds(r0, MOE_BLK)],
            o_ref.at[pl.ds(pl.multiple_of(row0 + r0, MOE_BLK), MOE_BLK), pl.ds(col, tile)],
            out_sems.at[sl])

    def drain(sl, count):
        def one(b, c):
            out_copy(sl, 0, 0).wait()
            return c
        lax.fori_loop(0, count, one, 0)

    def zero_copy(b):
        return pltpu.make_async_copy(
            zblk_ref,
            o_ref.at[pl.ds(pl.multiple_of(b * MOE_BLK, MOE_BLK), MOE_BLK), pl.ds(col, tile)],
            zero_sem)

    @pl.when(q >= 2)
    def _():
        drain(slot, nblk_ref[(q - 2) // nj])

    @pl.when(s == 0)
    def _():
        zblk_ref[...] = jnp.zeros_like(zblk_ref)

        def zs(b, c):
            zero_copy(b).start()
            return c
        lax.fori_loop(tot_ref[0], n_blocks, zs, 0)

        def zw(b, c):
            zero_copy(b).wait()
            return c
        lax.fori_loop(tot_ref[0], n_blocks, zw, 0)

    def run(b0, k):
        compute_rows(slot, pl.multiple_of(b0 * MOE_BLK, MOE_BLK), k * MOE_BLK)
        for i in range(k):
            out_copy(slot, b0 + i, start_ref[s]).start()

    k = 1 << (MOE_SB_BLKS.bit_length() - 1)
    while k >= 1:
        @pl.when(nb & k != 0)
        def _(k=k):
            run(nb & ~(2 * k - 1), k)
        k //= 2

    @pl.when(q == pl.num_programs(0) * nj - 1)
    def _():
        drain(slot, nb)

        @pl.when(q >= 1)
        def _():
            drain(1 - slot, nblk_ref[(q - 1) // nj])


def _gate_up_body(e_ref, start_ref, nblk_ref, tot_ref, x_ref, wg_ref, wu_ref, bg_ref, bu_ref,
                  o_ref, stage_ref, zblk_ref, out_sems, zero_sem, *, n_blocks):
    tj = wg_ref.shape[1]
    bias = jnp.concatenate([bg_ref[...], bu_ref[...]], axis=1)

    def compute_rows(slot, r0, n_rows):
        xp = x_ref[pl.ds(r0, n_rows), :]
        x_lo = lax.bitcast_convert_type(xp << 16, F32).astype(BF16)
        x_hi = lax.bitcast_convert_type(xp & jnp.uint32(0xFFFF0000), F32).astype(BF16)
        x = jnp.concatenate([x_lo, x_hi], axis=1)
        w = jnp.concatenate([wg_ref[...], wu_ref[...]], axis=1).astype(BF16)
        gu = jnp.dot(x, w, preferred_element_type=F32) + bias
        gate = jnp.minimum(gu[:, :tj], SWIGLU_LIMIT)
        up = jnp.clip(gu[:, tj:], -SWIGLU_LIMIT, SWIGLU_LIMIT)
        act = (up + 1.0) * (gate * _sigmoid(gate * SWIGLU_ALPHA))
        stage_ref[slot, pl.ds(r0, n_rows), :] = act.astype(stage_ref.dtype)

    _sb_run(start_ref, nblk_ref, tot_ref, o_ref, stage_ref, zblk_ref, out_sems, zero_sem,
            compute_rows, n_blocks=n_blocks)


def _sb_scratch(rows, tile, dtype):
    return [pltpu.VMEM((2, rows, tile), dtype), pltpu.VMEM((MOE_BLK, tile), dtype),
            pltpu.SemaphoreType.DMA((2,)), pltpu.SemaphoreType.DMA]


def _moe_gate_up(sb, n_sb, xs, w_gu, b_gu, *, tj):
    ne, d, de2 = w_gu.shape
    de = de2 // 2
    nj = de // tj
    rows = MOE_SB_BLKS * MOE_BLK
    n_rows, half = xs.shape
    x_spec, wcol = _sb_specs(rows, half, nj)
    in_specs = [
        x_spec,
        pl.BlockSpec((None, d, tj), lambda s, j, e, st, nb, tt: (e[s], 0, wcol(s, j, nb))),
        pl.BlockSpec((None, d, tj), lambda s, j, e, st, nb, tt: (e[s], 0, nj + wcol(s, j, nb))),
        pl.BlockSpec((None, 1, tj), lambda s, j, e, st, nb, tt: (e[s], 0, wcol(s, j, nb))),
        pl.BlockSpec((None, 1, tj), lambda s, j, e, st, nb, tt: (e[s], 0, nj + wcol(s, j, nb))),
    ]
    return pl.pallas_call(
        functools.partial(_gate_up_body, n_blocks=n_rows // MOE_BLK),
        grid_spec=pltpu.PrefetchScalarGridSpec(
            num_scalar_prefetch=4,
            grid=(n_sb, nj),
            in_specs=in_specs,
            out_specs=pl.BlockSpec(memory_space=pl.ANY),
            scratch_shapes=_sb_scratch(rows, tj, BF16),
        ),
        out_shape=jax.ShapeDtypeStruct((n_rows, de), BF16),
        compiler_params=_cparams(2),
        name="moe_gate_up",
    )(*sb, xs, w_gu, w_gu, b_gu.reshape(ne, 1, de2), b_gu.reshape(ne, 1, de2))


def _down_body(e_ref, start_ref, nblk_ref, tot_ref, a_ref, w_ref, b_ref, o_ref,
               stage_ref, zblk_ref, out_sems, zero_sem, *, n_blocks):
    bias = b_ref[...]

    def compute_rows(slot, r0, n_rows):
        stage_ref[slot, pl.ds(r0, n_rows), :] = jnp.dot(
            a_ref[pl.ds(r0, n_rows), :], w_ref[...].astype(BF16),
            preferred_element_type=F32) + bias

    _sb_run(start_ref, nblk_ref, tot_ref, o_ref, stage_ref, zblk_ref, out_sems, zero_sem,
            compute_rows, n_blocks=n_blocks)


def _moe_down(sb, n_sb, act, w_down, b_down, *, tn):
    ne, de, d = w_down.shape
    nn = d // tn
    rows = MOE_SB_BLKS * MOE_BLK
    n_rows = act.shape[0]
    a_spec, wcol = _sb_specs(rows, de, nn)
    in_specs = [
        a_spec,
        pl.BlockSpec((None, de, tn), lambda s, j, e, st, nb, tt: (e[s], 0, wcol(s, j, nb))),
        pl.BlockSpec((None, 1, tn), lambda s, j, e, st, nb, tt: (e[s], 0, wcol(s, j, nb))),
    ]
    return pl.pallas_call(
        functools.partial(_down_body, n_blocks=n_rows // MOE_BLK),
        grid_spec=pltpu.PrefetchScalarGridSpec(
            num_scalar_prefetch=4,
            grid=(n_sb, nn),
            in_specs=in_specs,
            out_specs=pl.BlockSpec(memory_space=pl.ANY),
            scratch_shapes=_sb_scratch(rows, tn, F32),
        ),
        out_shape=jax.ShapeDtypeStruct((n_rows, d), F32),
        compiler_params=_cparams(2),
        name="moe_down",
    )(*sb, act, w_down, b_down.reshape(ne, 1, d))


def _moe_schedule(cnt, n_pairs, ne):
    n_blocks = n_pairs // MOE_BLK + ne + MOE_SB_BLKS - 1
    s_max = -(-(n_pairs // MOE_BLK + ne) // MOE_SB_BLKS) + ne
    pblk = (cnt + MOE_BLK - 1) // MOE_BLK
    seg_end_blk = jnp.cumsum(pblk)
    seg_start = (seg_end_blk - pblk) * MOE_BLK
    total_blk = seg_end_blk[-1:]
    nsb_e = (pblk + MOE_SB_BLKS - 1) // MOE_SB_BLKS
    sb_end = jnp.cumsum(nsb_e)
    sb_first = sb_end - nsb_e
    n_sb = sb_end[-1]
    s_ids = jnp.arange(s_max, dtype=I32)
    used = s_ids < n_sb
    e_of_s = jnp.minimum(jnp.searchsorted(sb_end, s_ids, side="right"), ne - 1).astype(I32)
    i_in_e = s_ids - sb_first[e_of_s]
    start = seg_start[e_of_s] + i_in_e * (MOE_SB_BLKS * MOE_BLK)
    nblk = jnp.where(used, jnp.clip(pblk[e_of_s] - MOE_SB_BLKS * i_in_e, 0, MOE_SB_BLKS), 0)
    last = jnp.maximum(n_sb - 1, 0)
    sb = (jnp.where(used, e_of_s, e_of_s[last]).astype(I32),
          jnp.where(used, start, start[last]).astype(I32), nblk.astype(I32),
          total_blk.astype(I32))
    return sb, n_sb.astype(I32), seg_start.astype(I32), n_blocks * MOE_BLK


def _combine_body(dest_ref, y_ref, h_ref, p_ref, g_ref, o_ref, ybuf, sems, *, t_total):
    i = pl.program_id(0)
    n = pl.num_programs(0)
    tc, d = h_ref.shape
    groups = tc // SUBLANE

    def row_copy(slot, k, g, u, row):
        return pltpu.make_async_copy(y_ref.at[pl.ds(row, 1)], ybuf.at[slot, k, g, pl.ds(u, 1)],
                                     sems.at[slot])

    def issue(tile, slot):
        def one(g, carry):
            for k in range(TOP_K):
                base = k * t_total + tile * tc + g * SUBLANE
                for u in range(SUBLANE):
                    row_copy(slot, k, g, u, dest_ref[base + u]).start()
            return carry
        lax.fori_loop(0, groups, one, 0, unroll=4)

    @pl.when(i == 0)
    def _():
        for t0 in range(COMBINE_BUFS - 1):
            @pl.when(t0 < n)
            def _(t0=t0):
                issue(t0, t0)

    slot = i % COMBINE_BUFS

    def drain(g, carry):
        for _ in range(TOP_K * SUBLANE):
            row_copy(slot, 0, 0, 0, 0).wait()
        return carry
    lax.fori_loop(0, groups, drain, 0)

    @pl.when(i + COMBINE_BUFS - 1 < n)
    def _():
        issue(i + COMBINE_BUFS - 1, (i + COMBINE_BUFS - 1) % COMBINE_BUFS)

    p = p_ref[...]
    cw = min(d, 4 * LANE)
    ssq = jnp.zeros((tc, 1), F32)
    for c in range(d // cw):
        cols = slice(c * cw, (c + 1) * cw)
        acc = h_ref[:, cols]
        for k in range(TOP_K):
            acc = acc + p[:, k:k + 1] * ybuf[slot, k, :, :, cols].reshape(tc, cw)
        o_ref[:, cols] = acc
        ssq = ssq + jnp.sum(acc * acc, axis=-1, keepdims=True)
    scale = lax.rsqrt(ssq * (1.0 / d) + EPS)
    for c in range(d // cw):
        cols = slice(c * cw, (c + 1) * cw)
        o_ref[:, cols] = o_ref[:, cols] * scale * g_ref[:, cols]


def _combine(dest_flat, y_sorted, hres, p_t, final_g, *, tc):
    t, d = hres.shape
    return pl.pallas_call(
        functools.partial(_combine_body, t_total=t),
        grid_spec=pltpu.PrefetchScalarGridSpec(
            num_scalar_prefetch=1,
            grid=(t // tc,),
            in_specs=[pl.BlockSpec(memory_space=pl.ANY),
                      pl.BlockSpec((tc, d), lambda i, *_: (i, 0)),
                      pl.BlockSpec((tc, TOP_K), lambda i, *_: (i, 0)),
                      pl.BlockSpec((1, d), lambda i, *_: (0, 0))],
            out_specs=pl.BlockSpec((tc, d), lambda i, *_: (i, 0)),
            scratch_shapes=[pltpu.VMEM((COMBINE_BUFS, TOP_K, tc // SUBLANE, SUBLANE, d), F32),
                            pltpu.SemaphoreType.DMA((COMBINE_BUFS,))],
        ),
        out_shape=jax.ShapeDtypeStruct((t, d), F32),
        compiler_params=_cparams(1),
        name="moe_combine",
    )(dest_flat, y_sorted, hres, p_t, final_g.reshape(1, d))


def _pick(n, prefs):
    for p in prefs:
        if n % p == 0:
            return p
    raise ValueError(f"no tile in {prefs} divides {n}")


def kernel(x, meta_tokens, norm_mix_g, w_in, ssd_conv_w, ssd_conv_b, dt_bias, a_log, d_skip,
           ssd_norm_g, w_ssd_out, sc_conv_w, w_sc_out, w_o, norm_ffn_g, w_router, b_router,
           w_gate_up, b_gate_up, w_down, b_down, final_g):
    batch, seq, d = x.shape
    assert norm_mix_g.shape[0] == 1, "meta-token handling assumes a single layer"
    assert seq % SSD_CHUNK == 0 and N_META <= SSD_CHUNK
    t = batch * seq
    heads = d // SSD_HEAD_DIM
    gn = SSD_GROUPS * SSD_STATE
    conv_dim = d + 2 * gn
    xbc_off = d
    dt_off = xbc_off + conv_dim
    sc_off = dt_off + heads
    gate_off = sc_off + 3 * d
    ne = w_router.shape[-1]
    de = w_down.shape[2]

    x2 = x.reshape(t, d)
    w_in_nk = jnp.swapaxes(w_in[0], 0, 1)

    tm = _pick(seq, (1024, 512, 256, 128))
    tn = _pick(d, (512, 256, 128))

    h = _rmsnorm_bf16(x2, norm_mix_g[0], _pick(t, (256, 128)))
    h_meta = _rmsnorm_bf16(meta_tokens, norm_mix_g[0], N_META)

    proj = functools.partial(_matmul, h, w_in_nk, tm=tm, out_dtype=F32, w_is_nk=True)
    z = proj(0, d, tn=tn, name="proj_z")
    xbc, xbc_meta = _proj_conv(h, w_in_nk, [xbc_off], h_meta, ssd_conv_w[0],
                               ssd_conv_b[0].reshape(1, conv_dim), width=conv_dim, tm=tm,
                               tn=tn, seq=seq, mode="xbc", name="proj_xbc")
    dtr, dt_meta = proj(dt_off, LANE, tn=LANE, side=h_meta, name="proj_dt")
    u_sc = _proj_conv(h, w_in_nk, [sc_off, sc_off + d, sc_off + 2 * d], h_meta,
                      sc_conv_w[0], None, width=d, tm=tm, tn=_pick(d, (256, 128)), seq=seq,
                      mode="sc", name="proj_sc")
    gates = proj(gate_off, 2 * d, tn=tn, name="proj_gate")

    e_mat = (jnp.arange(d, dtype=I32)[None, :] // SSD_HEAD_DIM
             == jnp.arange(heads, dtype=I32)[:, None]).astype(BF16)
    e2 = jnp.concatenate([e_mat, e_mat], axis=0)
    pad = SSD_CHUNK - N_META
    dtb = dt_bias[0].reshape(1, heads)
    alog = a_log[0].reshape(1, heads)
    s0 = _ssd_meta(jnp.pad(xbc_meta, ((pad, 0), (0, 0))),
                   jnp.pad(dt_meta, ((pad, 0), (0, 0))), dtb, alog, e2, heads)
    dskip_x = jnp.repeat(d_skip[0], SSD_HEAD_DIM).reshape(1, d)
    g_ssd = _ssd_main(xbc, dtr, z, s0, dtb, alog, dskip_x, ssd_norm_g[0].reshape(1, d), e2,
                      heads=heads, batch=batch)

    y_a = _matmul(g_ssd, w_ssd_out[0], 0, d, tm=tm, tn=tn, out_dtype=F32,
                  extras=[(gates, 0)],
                  epilogue=lambda acc, ga: _sigmoid(ga) * acc, name="ssd_out")
    mixed = _matmul(u_sc, w_sc_out[0], 0, d, tm=tm, tn=tn, out_dtype=BF16,
                    extras=[(gates, d), (y_a, 0)],
                    epilogue=lambda acc, gb, ya: ya + _sigmoid(gb) * acc, name="sc_out")
    hres = _matmul(mixed, w_o[0], 0, d, tm=tm, tn=tn, out_dtype=F32,
                   extras=[(x2, 0)], epilogue=lambda acc, xr: xr + acc, name="w_o")

    rt = _pick(t, (512, 256, 128))
    hp, top_idx, probs, rank, counts = _router(hres, norm_ffn_g[0], w_router[0].T,
                                               b_router[0], rt)
    cnt = counts[:, 0].astype(I32)
    sb, n_sb, seg_start, n_slots = _moe_schedule(cnt, t * TOP_K, ne)
    e_ids = jnp.arange(ne, dtype=I32)[:, None, None]
    dest = rank + jnp.sum(jnp.where(top_idx[None] == e_ids, seg_start[:, None, None], 0),
                          axis=0)
    dest_flat = dest.reshape(-1)

    xs = _moe_dispatch(dest_flat, cnt, seg_start, sb[3], hp, n_slots,
                       tm=_pick(t, (1024, 512, 256, 128)))
    act = _moe_gate_up(sb, n_sb, xs, w_gate_up[0], b_gate_up[0], tj=_pick(de, (256, 128)))
    y_sorted = _moe_down(sb, n_sb, act, w_down[0], b_down[0],
                         tn=_pick(d, (1024, 512, 256, 128)))
    out = _combine(dest_flat, y_sorted, hres, probs.T, final_g, tc=_pick(t, (128,)))
    return out.reshape(batch, seq, d)
```

```python
import functools

import jax
import jax.numpy as jnp
from jax import lax
from jax.experimental import pallas as pl
from jax.experimental.pallas import tpu as pltpu

F32 = jnp.float32
BF16 = jnp.bfloat16
I32 = jnp.int32
U32 = jnp.uint32

N_META = 16
SSD_HEAD_DIM = 64
SSD_GROUPS = 8
SSD_STATE = 128
SSD_CONV = 4
SSD_CHUNK = 128
SC_CONV = 3
TOP_K = 4
SWIGLU_LIMIT = 7.0
SWIGLU_ALPHA = 1.702
EPS = 1e-5

LANE = 128
SUBLANE = 8
MOE_BLK = 128
MOE_SB_BLKS = 10
EPILOGUE_ROWS = 256
COMBINE_BUFS = 3
VMEM_MB = 56
VMEM_MB_WIDE = 60


def _cparams(n_axes, vmem_mb=VMEM_MB):
    return pltpu.CompilerParams(
        dimension_semantics=("arbitrary",) * n_axes,
        vmem_limit_bytes=vmem_mb * 1024 * 1024,
    )


def _sigmoid(v):
    return 1.0 / (1.0 + jnp.exp(-v))


def _rms_body(x_ref, g_ref, o_ref):
    x = x_ref[...]
    ms = jnp.mean(x * x, axis=-1, keepdims=True)
    o_ref[...] = (x * lax.rsqrt(ms + EPS) * g_ref[...]).astype(o_ref.dtype)


def _rmsnorm_bf16(x2, g, tm):
    m, d = x2.shape
    return pl.pallas_call(
        _rms_body,
        grid=(m // tm,),
        in_specs=[pl.BlockSpec((tm, d), lambda i: (i, 0)),
                  pl.BlockSpec((1, d), lambda i: (0, 0))],
        out_specs=pl.BlockSpec((tm, d), lambda i: (i, 0)),
        out_shape=jax.ShapeDtypeStruct((m, d), BF16),
        compiler_params=_cparams(1),
        name="rmsnorm",
    )(x2, g.reshape(1, d))


def _chunked_matmul(a_ref, wbf_ref, acc_ref, finish):
    tm = a_ref.shape[0]
    ch = acc_ref.shape[1]
    n = tm // ch

    def rows(c):
        return slice(c * ch, (c + 1) * ch)

    for c in range(n):
        if c > 0:
            finish(rows(c - 1), acc_ref[(c - 1) % 2])
        acc_ref[c % 2] = jnp.dot(a_ref[rows(c), :], wbf_ref[...], preferred_element_type=F32)
    finish(rows(n - 1), acc_ref[(n - 1) % 2])


def _mm_body(*refs, n_extra, has_side, epilogue, w_is_nk):
    it = iter(refs)
    a_ref = next(it)
    w_ref = next(it)
    side_ref = next(it) if has_side else None
    extras = [next(it) for _ in range(n_extra)]
    o_ref = next(it)
    oside_ref = next(it) if has_side else None
    plain = epilogue is None and not has_side
    wbf_ref = None if plain else next(it)
    acc_ref = next(it) if epilogue is not None else None

    def w_bf16():
        w = w_ref[...]
        return (w.T if w_is_nk else w).astype(BF16)

    if plain:
        o_ref[...] = jnp.dot(a_ref[...], w_bf16(),
                             preferred_element_type=F32).astype(o_ref.dtype)
        return

    @pl.when(pl.program_id(1) == 0)
    def _():
        wbf_ref[...] = w_bf16()
        if has_side:
            oside_ref[...] = jnp.dot(side_ref[...], wbf_ref[...],
                                     preferred_element_type=F32)

    if epilogue is None:
        o_ref[...] = jnp.dot(a_ref[...], wbf_ref[...],
                             preferred_element_type=F32).astype(o_ref.dtype)
    else:
        def finish(rows, acc):
            o_ref[rows, :] = epilogue(acc, *[e[rows, :] for e in extras]).astype(o_ref.dtype)
        _chunked_matmul(a_ref, wbf_ref, acc_ref, finish)


def _matmul(a, w, col_off, n_cols, *, tm, tn, out_dtype, side=None, extras=(),
            epilogue=None, w_is_nk=False, name="matmul"):
    m, k = a.shape
    assert m % tm == 0 and n_cols % tn == 0
    if w_is_nk:
        assert col_off % SUBLANE == 0
        w_spec = pl.BlockSpec((pl.Element(tn), pl.Element(k)),
                              lambda n, i: (pl.multiple_of(col_off + n * tn, SUBLANE), 0))
    else:
        assert col_off % tn == 0
        w_spec = pl.BlockSpec((k, tn), lambda n, i, cb=col_off // tn: (0, n + cb))
    in_specs = [pl.BlockSpec((tm, k), lambda n, i: (i, 0)), w_spec]
    args = [a, w]
    if side is not None:
        in_specs.append(pl.BlockSpec((side.shape[0], k), lambda n, i: (0, 0)))
        args.append(side)
    for arr, off in extras:
        assert off % tn == 0
        in_specs.append(pl.BlockSpec((tm, tn), lambda n, i, o=off // tn: (i, n + o)))
        args.append(arr)
    out_shape = [jax.ShapeDtypeStruct((m, n_cols), out_dtype)]
    out_specs = [pl.BlockSpec((tm, tn), lambda n, i: (i, n))]
    if side is not None:
        out_shape.append(jax.ShapeDtypeStruct((side.shape[0], n_cols), F32))
        out_specs.append(pl.BlockSpec((side.shape[0], tn), lambda n, i: (0, n)))
    res = pl.pallas_call(
        functools.partial(_mm_body, n_extra=len(extras), has_side=side is not None,
                          epilogue=epilogue, w_is_nk=w_is_nk),
        grid=(n_cols // tn, m // tm),
        in_specs=in_specs,
        out_specs=out_specs,
        out_shape=out_shape,
        scratch_shapes=([] if epilogue is None and side is None
                        else [pltpu.VMEM((k, tn), BF16)])
        + ([pltpu.VMEM((2, min(tm, EPILOGUE_ROWS), tn), F32)] if epilogue is not None else []),
        compiler_params=_cparams(2),
        name=name,
    )(*args)
    return res if side is not None else res[0]


def _causal_conv(raw, prev, w, n_tap):
    acc = raw * w[n_tap - 1:n_tap, :]
    row8 = lax.broadcasted_iota(I32, (SUBLANE, raw.shape[1]), 0)
    for s in range(1, n_tap):
        sh = pltpu.roll(raw, s, 0)
        shp = pltpu.roll(prev, s, 0)
        first = jnp.where(row8 < s, shp, sh[:SUBLANE])
        sh = jnp.concatenate([first, sh[SUBLANE:]], axis=0)
        acc = acc + sh * w[n_tap - 1 - s:n_tap - s, :]
    return acc


def _proj_conv_body(*refs, n_w, n_tap, tiles_per_batch, mode):
    it = iter(refs)
    a_ref = next(it)
    w_refs = [next(it) for _ in range(n_w)]
    side_ref = next(it)
    cw_ref = next(it)
    cb_ref = next(it) if mode == "xbc" else None
    o_ref = next(it)
    oside_ref = next(it) if mode == "xbc" else None
    wbf_ref = next(it)
    prev_ref = next(it)
    tail_ref = next(it)
    u_ref = next(it)
    tn = o_ref.shape[1]
    cw = cw_ref[...]

    def pre(acc):
        return acc if mode == "xbc" else acc[:, tn:2 * tn] * acc[:, 2 * tn:]

    def post(y, acc):
        if mode == "xbc":
            v = y + cb_ref[...]
            return v * _sigmoid(v)
        return acc[:, :tn] * y

    m = pl.program_id(1)

    @pl.when(m == 0)
    def _():
        for i, w_ref in enumerate(w_refs):
            wbf_ref[:, i * tn:(i + 1) * tn] = w_ref[...].T.astype(BF16)
        acc_m = jnp.dot(side_ref[...], wbf_ref[...], preferred_element_type=F32)
        u_m = pre(acc_m)
        tail_ref[...] = u_m[N_META - SUBLANE:, :]
        if mode == "xbc":
            y_m = _causal_conv(u_m, jnp.zeros((SUBLANE, tn), F32), cw, n_tap)
            oside_ref[...] = post(y_m, acc_m)

    @pl.when(m % tiles_per_batch == 0)
    def _():
        prev_ref[...] = tail_ref[...]

    ch = u_ref.shape[0] - SUBLANE
    for c in range(a_ref.shape[0] // ch):
        rows = slice(c * ch, (c + 1) * ch)
        acc = jnp.dot(a_ref[rows, :], wbf_ref[...], preferred_element_type=F32)
        u_ref[:SUBLANE, :] = prev_ref[...]
        u_ref[SUBLANE:, :] = pre(acc)
        y = None
        for k in range(n_tap):
            r0 = SUBLANE - (n_tap - 1) + k
            term = u_ref[r0:r0 + ch, :] * cw[k:k + 1, :]
            y = term if y is None else y + term
        prev_ref[...] = u_ref[ch:, :]
        o_ref[rows, :] = post(y, acc).astype(o_ref.dtype)


def _proj_conv(a, w_nk, row_offs, side, cw, cb, *, width, tm, tn, seq, mode, name):
    m, k = a.shape
    n_w = len(row_offs)
    n_tap = cw.shape[0]
    ms = side.shape[0]

    def w_spec(off):
        return pl.BlockSpec((pl.Element(tn), pl.Element(k)),
                            lambda n, i: (pl.multiple_of(off + n * tn, SUBLANE), 0))

    in_specs = [pl.BlockSpec((tm, k), lambda n, i: (i, 0))]
    in_specs += [w_spec(off) for off in row_offs]
    in_specs += [pl.BlockSpec((ms, k), lambda n, i: (0, 0)),
                 pl.BlockSpec((n_tap, tn), lambda n, i: (0, n))]
    args = [a] + [w_nk] * n_w + [side, cw]
    out_shape = [jax.ShapeDtypeStruct((m, width), F32 if mode == "xbc" else BF16)]
    out_specs = [pl.BlockSpec((tm, tn), lambda n, i: (i, n))]
    if mode == "xbc":
        in_specs.append(pl.BlockSpec((1, tn), lambda n, i: (0, n)))
        args.append(cb)
        out_shape.append(jax.ShapeDtypeStruct((ms, width), F32))
        out_specs.append(pl.BlockSpec((ms, tn), lambda n, i: (0, n)))
    res = pl.pallas_call(
        functools.partial(_proj_conv_body, n_w=n_w, n_tap=n_tap,
                          tiles_per_batch=seq // tm, mode=mode),
        grid=(width // tn, m // tm),
        in_specs=in_specs,
        out_specs=out_specs,
        out_shape=out_shape,
        scratch_shapes=[pltpu.VMEM((k, n_w * tn), BF16),
                        pltpu.VMEM((SUBLANE, tn), F32),
                        pltpu.VMEM((SUBLANE, tn), F32),
                        pltpu.VMEM((min(tm, EPILOGUE_ROWS) + SUBLANE, tn), F32)],
        compiler_params=_cparams(2, VMEM_MB if n_w == 1 else VMEM_MB_WIDE),
        name=name,
    )(*args)
    return res if mode == "xbc" else res[0]


def _ssd_chunk(xbc, dt_raw, dtb, alog, e2, *, heads, meta):
    L = xbc.shape[0]
    di = heads * SSD_HEAD_DIM
    gn = SSD_GROUPS * SSD_STATE
    dt_in = dt_raw[:, :heads] + dtb
    dtv = jnp.maximum(dt_in, 0.0) + jnp.log1p(jnp.exp(-jnp.abs(dt_in)))
    if meta:
        dtv = dtv * (lax.broadcasted_iota(I32, (L, 1), 0) >= L - N_META).astype(F32)
    xs = xbc[:, :di]
    bm = xbc[:, di:di + gn]
    cm = xbc[:, di + gn:]
    a_neg = -jnp.exp(alog)
    ad = dtv * a_neg
    r = lax.broadcasted_iota(I32, (L, L), 0)
    c = lax.broadcasted_iota(I32, (L, L), 1)
    tril = r >= c
    acum = jnp.dot(tril.astype(F32), ad, precision=lax.Precision.HIGHEST,
                   preferred_element_type=F32)
    stack = jnp.concatenate([dtv, dtv * jnp.exp(acum[L - 1:L, :] - acum), jnp.exp(acum)],
                            axis=0)
    hi = stack.astype(BF16)
    lo = (stack - hi.astype(F32)).astype(BF16)
    exp_x = jnp.dot(jnp.concatenate([hi, lo], axis=1), e2, preferred_element_type=F32)
    dt_x, dd_x, ea_x = exp_x[:L], exp_x[L:2 * L], exp_x[2 * L:]
    bt = bm.T.astype(BF16)
    return dict(xs=xs, cm=cm, bt=bt, acum=acum, ea_x=ea_x, x_dt=(xs * dt_x).astype(BF16),
                x_dec=(xs * dd_x).astype(BF16), tril=tril)


def _ssd_state_update(q, st_ref, heads):
    gw = heads * SSD_HEAD_DIM // SSD_GROUPS
    L = q["ea_x"].shape[0]
    for g in range(SSD_GROUPS):
        sl = slice(g * gw, (g + 1) * gw)
        btg = q["bt"][g * SSD_STATE:(g + 1) * SSD_STATE, :]
        new = jnp.dot(btg, q["x_dec"][:, sl], preferred_element_type=F32)
        st_ref[:, sl] = st_ref[:, sl] * q["ea_x"][L - 1:L, sl] + new


def _ssd_body(xbc_ref, dt_ref, z_ref, s0_ref, dtb_ref, alog_ref, dskip_ref, ng_ref, e_ref,
              o_ref, st_ref, *, heads):
    @pl.when(pl.program_id(1) == 0)
    def _():
        st_ref[...] = s0_ref[...]

    L = xbc_ref.shape[0]
    q = _ssd_chunk(xbc_ref[...], dt_ref[...], dtb_ref[...], alog_ref[...], e_ref[...],
                   heads=heads, meta=False)

    gw = heads * SSD_HEAD_DIM // SSD_GROUPS
    j_heads = heads // SSD_GROUPS
    acum = q["acum"]
    acum_t = acum.T
    cmb = q["cm"].astype(BF16)
    x_dt_b = q["x_dt"]
    lane = lax.broadcasted_iota(I32, (L, 2 * SSD_HEAD_DIM), 1)
    lo = lane < SSD_HEAD_DIM
    neg_inf = jnp.float32(-jnp.inf)

    def decay_mat(h):
        col = jnp.broadcast_to(acum[:, h:h + 1], (L, L))
        row = jnp.broadcast_to(acum_t[h:h + 1, :], (L, L))
        return jnp.exp(jnp.where(q["tril"], col - row, neg_inf))

    for g in range(SSD_GROUPS):
        sl = slice(g * gw, (g + 1) * gw)
        cg = cmb[:, g * SSD_STATE:(g + 1) * SSD_STATE]
        btg = q["bt"][g * SSD_STATE:(g + 1) * SSD_STATE, :]
        cb_mat = jnp.dot(cg, btg, preferred_element_type=F32)
        parts = []
        for p in range(j_heads // 2):
            h1 = g * j_heads + 2 * p
            slab = x_dt_b[:, h1 * SSD_HEAD_DIM:(h1 + 2) * SSD_HEAD_DIM]
            lhs = jnp.concatenate([(cb_mat * decay_mat(h1)).astype(BF16),
                                   (cb_mat * decay_mat(h1 + 1)).astype(BF16)], axis=1)
            zero = jnp.zeros_like(slab)
            rhs = jnp.concatenate([jnp.where(lo, slab, zero),
                                   jnp.where(lo, zero, slab)], axis=0)
            parts.append(jnp.dot(lhs, rhs, preferred_element_type=F32))
        y_diag = parts[0] if len(parts) == 1 else jnp.concatenate(parts, axis=1)
        st_g = st_ref[:, sl]
        y_off = jnp.dot(cg, st_g.astype(BF16), preferred_element_type=F32)
        y = y_diag + y_off * q["ea_x"][:, sl] + dskip_ref[:, sl] * q["xs"][:, sl]
        zg = z_ref[:, sl]
        gated = y * (zg * _sigmoid(zg))
        ms = jnp.mean(gated * gated, axis=-1, keepdims=True)
        o_ref[:, sl] = (gated * lax.rsqrt(ms + EPS) * ng_ref[:, sl]).astype(o_ref.dtype)
        new = jnp.dot(btg, q["x_dec"][:, sl], preferred_element_type=F32)
        st_ref[:, sl] = st_g * q["ea_x"][L - 1:L, sl] + new


def _ssd_meta_body(xbc_ref, dt_ref, dtb_ref, alog_ref, e_ref, st_ref, *, heads):
    st_ref[...] = jnp.zeros_like(st_ref)
    q = _ssd_chunk(xbc_ref[...], dt_ref[...], dtb_ref[...], alog_ref[...], e_ref[...],
                   heads=heads, meta=True)
    _ssd_state_update(q, st_ref, heads)


def _full_spec(shape):
    nd = len(shape)
    return pl.BlockSpec(shape, lambda *_: (0,) * nd)


def _ssd_meta(xbc_m, dt_m, dtb, alog, e2, heads):
    di = heads * SSD_HEAD_DIM
    args = [xbc_m, dt_m, dtb, alog, e2]
    return pl.pallas_call(
        functools.partial(_ssd_meta_body, heads=heads),
        grid=(1,),
        in_specs=[_full_spec(a.shape) for a in args],
        out_specs=_full_spec((SSD_STATE, di)),
        out_shape=jax.ShapeDtypeStruct((SSD_STATE, di), F32),
        compiler_params=_cparams(1),
        name="ssd_meta",
    )(*args)


def _ssd_main(xbc, dt, z, s0, dtb, alog, dskip_x, ng, e2, *, heads, batch):
    t, cd = xbc.shape
    di = heads * SSD_HEAD_DIM
    L = SSD_CHUNK
    nc = t // batch // L
    row = lambda b, c: (b * nc + c, 0)
    in_specs = [pl.BlockSpec((L, cd), row),
                pl.BlockSpec((L, LANE), row),
                pl.BlockSpec((L, di), row)]
    consts = [s0, dtb, alog, dskip_x, ng, e2]
    in_specs += [_full_spec(a.shape) for a in consts]
    return pl.pallas_call(
        functools.partial(_ssd_body, heads=heads),
        grid=(batch, nc),
        in_specs=in_specs,
        out_specs=pl.BlockSpec((L, di), row),
        out_shape=jax.ShapeDtypeStruct((t, di), BF16),
        scratch_shapes=[pltpu.VMEM((SSD_STATE, di), F32)],
        compiler_params=_cparams(2),
        name="ssd_main",
    )(xbc, dt, z, *consts)


def _router_body(x_ref, g_ref, wr_ref, br_ref, hp_ref, idx_ref, prob_ref, rank_ref,
                 cnt_ref, base_ref):
    @pl.when(pl.program_id(0) == 0)
    def _():
        base_ref[...] = jnp.zeros_like(base_ref)

    x = x_ref[...]
    tm, d = x.shape
    ne = wr_ref.shape[0]
    ms = jnp.mean(x * x, axis=-1, keepdims=True)
    h = x * lax.rsqrt(ms + EPS) * g_ref[...]
    half = d // 2
    h_hi = h.astype(BF16)
    h_hi32 = h_hi.astype(F32)
    lo_bits = lax.bitcast_convert_type(h_hi32[:, :half], U32)
    hi_bits = lax.bitcast_convert_type(h_hi32[:, half:], U32)
    hp_ref[...] = (lo_bits >> 16) | (hi_bits & jnp.uint32(0xFFFF0000))

    w = wr_ref[...]
    w_hi = w.astype(BF16)
    w_lo = (w - w_hi.astype(F32)).astype(BF16)
    h_lo = (h - h_hi32).astype(BF16)
    nt = (((1,), (1,)), ((), ()))
    part = lax.dot_general(jnp.concatenate([w_hi, w_lo], axis=0), h_hi, nt,
                           preferred_element_type=F32)
    logits = (part[:ne] + part[ne:]
              + lax.dot_general(w_hi, h_lo, nt, preferred_element_type=F32)
              + br_ref[:, 0:1])
    eidx = lax.broadcasted_iota(I32, (ne, tm), 0)
    work = logits
    vals, onehots = [], []
    for k in range(TOP_K):
        mx = jnp.max(work, axis=0, keepdims=True)
        sel = jnp.min(jnp.where(work == mx, eidx, ne), axis=0, keepdims=True)
        oh = eidx == sel
        idx_ref[k:k + 1, :] = sel
        vals.append(mx)
        onehots.append(oh)
        work = jnp.where(oh, -jnp.inf, work)
    exps = [jnp.exp(v - vals[0]) for v in vals]
    denom = exps[0]
    for e in exps[1:]:
        denom = denom + e
    for k in range(TOP_K):
        prob_ref[k:k + 1, :] = exps[k] / denom

    cnt = onehots[0].astype(F32)
    for oh in onehots[1:]:
        cnt = cnt + oh.astype(F32)
    r = lax.broadcasted_iota(I32, (tm, tm), 0)
    c = lax.broadcasted_iota(I32, (tm, tm), 1)
    before = (r < c).astype(BF16)
    prefix = jnp.dot(cnt.astype(BF16), before, preferred_element_type=F32)
    prefix = prefix + base_ref[:, 0:1]
    for k in range(TOP_K):
        rk = jnp.sum(jnp.where(onehots[k], prefix, 0.0), axis=0, keepdims=True)
        rank_ref[k:k + 1, :] = rk.astype(I32)
    base_ref[...] = base_ref[...] + jnp.sum(cnt, axis=1, keepdims=True)
    cnt_ref[...] = base_ref[...]


def _router(hres, g, wr_t, br, tm):
    t, d = hres.shape
    ne = wr_t.shape[0]
    tok = lambda i: (0, i)
    return pl.pallas_call(
        _router_body,
        grid=(t // tm,),
        in_specs=[pl.BlockSpec((tm, d), lambda i: (i, 0)),
                  _full_spec((1, d)), _full_spec((ne, d)), _full_spec((ne, LANE))],
        out_specs=[pl.BlockSpec((tm, d // 2), lambda i: (i, 0)),
                   pl.BlockSpec((TOP_K, tm), tok),
                   pl.BlockSpec((TOP_K, tm), tok),
                   pl.BlockSpec((TOP_K, tm), tok),
                   _full_spec((ne, LANE))],
        out_shape=[jax.ShapeDtypeStruct((t, d // 2), U32),
                   jax.ShapeDtypeStruct((TOP_K, t), I32),
                   jax.ShapeDtypeStruct((TOP_K, t), F32),
                   jax.ShapeDtypeStruct((TOP_K, t), I32),
                   jax.ShapeDtypeStruct((ne, LANE), F32)],
        scratch_shapes=[pltpu.VMEM((ne, LANE), F32)],
        compiler_params=_cparams(1),
        name="router",
    )(hres, g.reshape(1, d), wr_t, jnp.broadcast_to(br.reshape(ne, 1), (ne, LANE)))


def _dispatch_body(dest_ref, cnt_ref, seg_ref, tot_ref, hp_ref, xs_ref, zrow_ref, zblk_ref,
                   row_sem, pad_sem, *, t_total, ne, n_blocks):
    i = pl.program_id(0)
    tm = hp_ref.shape[0] * SUBLANE

    def pad_copy(slot):
        return pltpu.make_async_copy(zrow_ref.at[pl.ds(0, 1)], xs_ref.at[pl.ds(slot, 1)], pad_sem)

    def tail_copy(b):
        return pltpu.make_async_copy(
            zblk_ref, xs_ref.at[pl.ds(pl.multiple_of(b * MOE_BLK, MOE_BLK), MOE_BLK)], pad_sem)

    @pl.when(i == 0)
    def _():
        zrow_ref[...] = jnp.zeros_like(zrow_ref)
        zblk_ref[...] = jnp.zeros_like(zblk_ref)

        def expert(e, carry):
            used = cnt_ref[e]
            end = (used + MOE_BLK - 1) // MOE_BLK * MOE_BLK

            def start(r, c):
                pad_copy(seg_ref[e] + r).start()
                return c
            lax.fori_loop(used, end, start, 0)

            def wait(r, c):
                pad_copy(seg_ref[e] + r).wait()
                return c
            lax.fori_loop(used, end, wait, 0)
            return carry
        lax.fori_loop(0, ne, expert, 0)

        def tail_start(b, c):
            tail_copy(b).start()
            return c
        lax.fori_loop(tot_ref[0], n_blocks, tail_start, 0)

        def tail_wait(b, c):
            tail_copy(b).wait()
            return c
        lax.fori_loop(tot_ref[0], n_blocks, tail_wait, 0)

    def row_copy(g, u, slot):
        return pltpu.make_async_copy(hp_ref.at[g, pl.ds(u, 1)], xs_ref.at[pl.ds(slot, 1)],
                                     row_sem)

    def issue(g, c):
        for k in range(TOP_K):
            base = k * t_total + i * tm + g * SUBLANE
            for u in range(SUBLANE):
                row_copy(g, u, dest_ref[base + u]).start()
        return c
    lax.fori_loop(0, tm // SUBLANE, issue, 0, unroll=4)

    def drain(g, c):
        for _ in range(TOP_K * SUBLANE):
            row_copy(0, 0, 0).wait()
        return c
    lax.fori_loop(0, tm // SUBLANE, drain, 0)


def _moe_dispatch(dest_flat, cnt, seg_start, total_blk, hp, n_rows, *, tm):
    t, w = hp.shape
    ne = cnt.shape[0]
    return pl.pallas_call(
        functools.partial(_dispatch_body, t_total=t, ne=ne, n_blocks=n_rows // MOE_BLK),
        grid_spec=pltpu.PrefetchScalarGridSpec(
            num_scalar_prefetch=4,
            grid=(t // tm,),
            in_specs=[pl.BlockSpec((tm // SUBLANE, SUBLANE, w), lambda i, *_: (i, 0, 0))],
            out_specs=pl.BlockSpec(memory_space=pl.ANY),
            scratch_shapes=[pltpu.VMEM((SUBLANE, w), U32),
                            pltpu.VMEM((MOE_BLK, w), U32),
                            pltpu.SemaphoreType.DMA,
                            pltpu.SemaphoreType.DMA],
        ),
        out_shape=jax.ShapeDtypeStruct((n_rows, w), U32),
        compiler_params=_cparams(1),
        name="moe_dispatch",
    )(dest_flat, cnt, seg_start, total_blk, hp.reshape(t // SUBLANE, SUBLANE, w))


def _sb_specs(rows, width_in, n_tiles):
    def x_map(s, j, e, start, nblk, tot):
        return (pl.multiple_of(start[s], MOE_BLK), 0)

    def wcol(s, j, nblk):
        return jnp.where(nblk[s] > 0, j, n_tiles - 1)

    return pl.BlockSpec((pl.Element(rows), pl.Element(width_in)), x_map), wcol


def _sb_run(start_ref, nblk_ref, tot_ref, o_ref, stage_ref, zblk_ref, out_sems, zero_sem,
            compute_rows, *, n_blocks):
    s = pl.program_id(0)
    j = pl.program_id(1)
    nj = pl.num_programs(1)
    q = s * nj + j
    slot = q % 2
    tile = stage_ref.shape[2]
    col = pl.multiple_of(j * tile, LANE)
    nb = nblk_ref[s]

    def out_copy(sl, b, row0):
        r0 = pl.multiple_of(b * MOE_BLK, MOE_BLK)
        return pltpu.make_async_copy(
            stage_ref.at[sl, pl.ds(r0, MOE_BLK)],
            o_ref.at[pl.ds(pl.multiple_of(row0 + r0, MOE_BLK), MOE_BLK), pl.ds(col, tile)],
            out_sems.at[sl])

    def drain(sl, count):
        def one(b, c):
            out_copy(sl, 0, 0).wait()
            return c
        lax.fori_loop(0, count, one, 0)

    def zero_copy(b):
        return pltpu.make_async_copy(
            zblk_ref,
            o_ref.at[pl.ds(pl.multiple_of(b * MOE_BLK, MOE_BLK), MOE_BLK), pl.ds(col, tile)],
            zero_sem)

    @pl.when(q >= 2)
    def _():
        drain(slot, nblk_ref[(q - 2) // nj])

    @pl.when(s == 0)
    def _():
        zblk_ref[...] = jnp.zeros_like(zblk_ref)

        def zs(b, c):
            zero_copy(b).start()
            return c
        lax.fori_loop(tot_ref[0], n_blocks, zs, 0)

        def zw(b, c):
            zero_copy(b).wait()
            return c
        lax.fori_loop(tot_ref[0], n_blocks, zw, 0)

    def run(b0, k):
        compute_rows(slot, pl.multiple_of(b0 * MOE_BLK, MOE_BLK), k * MOE_BLK)
        for i in range(k):
            out_copy(slot, b0 + i, start_ref[s]).start()

    k = 1 << (MOE_SB_BLKS.bit_length() - 1)
    while k >= 1:
        @pl.when(nb & k != 0)
        def _(k=k):
            run(nb & ~(2 * k - 1), k)
        k //= 2

    @pl.when(q == pl.num_programs(0) * nj - 1)
    def _():
        drain(slot, nb)

        @pl.when(q >= 1)
        def _():
            drain(1 - slot, nblk_ref[(q - 1) // nj])


def _gate_up_body(e_ref, start_ref, nblk_ref, tot_ref, x_ref, wg_ref, wu_ref, bg_ref, bu_ref,
                  o_ref, stage_ref, zblk_ref, out_sems, zero_sem, *, n_blocks):
    tj = wg_ref.shape[1]
    bias = jnp.concatenate([bg_ref[...], bu_ref[...]], axis=1)

    def compute_rows(slot, r0, n_rows):
        xp = x_ref[pl.ds(r0, n_rows), :]
        x_lo = lax.bitcast_convert_type(xp << 16, F32).astype(BF16)
        x_hi = lax.bitcast_convert_type(xp & jnp.uint32(0xFFFF0000), F32).astype(BF16)
        x = jnp.concatenate([x_lo, x_hi], axis=1)
        w = jnp.concatenate([wg_ref[...], wu_ref[...]], axis=1).astype(BF16)
        gu = jnp.dot(x, w, preferred_element_type=F32) + bias
        gate = jnp.minimum(gu[:, :tj], SWIGLU_LIMIT)
        up = jnp.clip(gu[:, tj:], -SWIGLU_LIMIT, SWIGLU_LIMIT)
        act = (up + 1.0) * (gate * _sigmoid(gate * SWIGLU_ALPHA))
        stage_ref[slot, pl.ds(r0, n_rows), :] = act.astype(stage_ref.dtype)

    _sb_run(start_ref, nblk_ref, tot_ref, o_ref, stage_ref, zblk_ref, out_sems, zero_sem,
            compute_rows, n_blocks=n_blocks)


def _sb_scratch(rows, tile, dtype):
    return [pltpu.VMEM((2, rows, tile), dtype), pltpu.VMEM((MOE_BLK, tile), dtype),
            pltpu.SemaphoreType.DMA((2,)), pltpu.SemaphoreType.DMA]


def _moe_gate_up(sb, n_sb, xs, w_gu, b_gu, *, tj):
    ne, d, de2 = w_gu.shape
    de = de2 // 2
    nj = de // tj
    rows = MOE_SB_BLKS * MOE_BLK
    n_rows, half = xs.shape
    x_spec, wcol = _sb_specs(rows, half, nj)
    in_specs = [
        x_spec,
        pl.BlockSpec((None, d, tj), lambda s, j, e, st, nb, tt: (e[s], 0, wcol(s, j, nb))),
        pl.BlockSpec((None, d, tj), lambda s, j, e, st, nb, tt: (e[s], 0, nj + wcol(s, j, nb))),
        pl.BlockSpec((None, 1, tj), lambda s, j, e, st, nb, tt: (e[s], 0, wcol(s, j, nb))),
        pl.BlockSpec((None, 1, tj), lambda s, j, e, st, nb, tt: (e[s], 0, nj + wcol(s, j, nb))),
    ]
    return pl.pallas_call(
        functools.partial(_gate_up_body, n_blocks=n_rows // MOE_BLK),
        grid_spec=pltpu.PrefetchScalarGridSpec(
            num_scalar_prefetch=4,
            grid=(n_sb, nj),
            in_specs=in_specs,
            out_specs=pl.BlockSpec(memory_space=pl.ANY),
            scratch_shapes=_sb_scratch(rows, tj, BF16),
        ),
        out_shape=jax.ShapeDtypeStruct((n_rows, de), BF16),
        compiler_params=_cparams(2, VMEM_MB_WIDE),
        name="moe_gate_up",
    )(*sb, xs, w_gu, w_gu, b_gu.reshape(ne, 1, de2), b_gu.reshape(ne, 1, de2))


def _down_body(e_ref, start_ref, nblk_ref, tot_ref, a_ref, w_ref, b_ref, o_ref,
               stage_ref, zblk_ref, out_sems, zero_sem, *, n_blocks):
    bias = b_ref[...]

    def compute_rows(slot, r0, n_rows):
        stage_ref[slot, pl.ds(r0, n_rows), :] = jnp.dot(
            a_ref[pl.ds(r0, n_rows), :], w_ref[...].astype(BF16),
            preferred_element_type=F32) + bias

    _sb_run(start_ref, nblk_ref, tot_ref, o_ref, stage_ref, zblk_ref, out_sems, zero_sem,
            compute_rows, n_blocks=n_blocks)


def _moe_down(sb, n_sb, act, w_down, b_down, *, tn):
    ne, de, d = w_down.shape
    nn = d // tn
    rows = MOE_SB_BLKS * MOE_BLK
    n_rows = act.shape[0]
    a_spec, wcol = _sb_specs(rows, de, nn)
    in_specs = [
        a_spec,
        pl.BlockSpec((None, de, tn), lambda s, j, e, st, nb, tt: (e[s], 0, wcol(s, j, nb))),
        pl.BlockSpec((None, 1, tn), lambda s, j, e, st, nb, tt: (e[s], 0, wcol(s, j, nb))),
    ]
    return pl.pallas_call(
        functools.partial(_down_body, n_blocks=n_rows // MOE_BLK),
        grid_spec=pltpu.PrefetchScalarGridSpec(
            num_scalar_prefetch=4,
            grid=(n_sb, nn),
            in_specs=in_specs,
            out_specs=pl.BlockSpec(memory_space=pl.ANY),
            scratch_shapes=_sb_scratch(rows, tn, F32),
        ),
        out_shape=jax.ShapeDtypeStruct((n_rows, d), F32),
        compiler_params=_cparams(2),
        name="moe_down",
    )(*sb, act, w_down, b_down.reshape(ne, 1, d))


def _moe_schedule(cnt, n_pairs, ne):
    n_blocks = n_pairs // MOE_BLK + ne + MOE_SB_BLKS - 1
    s_max = -(-(n_pairs // MOE_BLK + ne) // MOE_SB_BLKS) + ne
    pblk = (cnt + MOE_BLK - 1) // MOE_BLK
    seg_end_blk = jnp.cumsum(pblk)
    seg_start = (seg_end_blk - pblk) * MOE_BLK
    total_blk = seg_end_blk[-1:]
    nsb_e = (pblk + MOE_SB_BLKS - 1) // MOE_SB_BLKS
    sb_end = jnp.cumsum(nsb_e)
    sb_first = sb_end - nsb_e
    n_sb = sb_end[-1]
    s_ids = jnp.arange(s_max, dtype=I32)
    used = s_ids < n_sb
    e_of_s = jnp.minimum(jnp.searchsorted(sb_end, s_ids, side="right"), ne - 1).astype(I32)
    i_in_e = s_ids - sb_first[e_of_s]
    start = seg_start[e_of_s] + i_in_e * (MOE_SB_BLKS * MOE_BLK)
    nblk = jnp.where(used, jnp.clip(pblk[e_of_s] - MOE_SB_BLKS * i_in_e, 0, MOE_SB_BLKS), 0)
    last = jnp.maximum(n_sb - 1, 0)
    sb = (jnp.where(used, e_of_s, e_of_s[last]).astype(I32),
          jnp.where(used, start, start[last]).astype(I32), nblk.astype(I32),
          total_blk.astype(I32))
    return sb, n_sb.astype(I32), seg_start.astype(I32), n_blocks * MOE_BLK


def _combine_body(dest_ref, y_ref, h_ref, p_ref, g_ref, o_ref, ybuf, sems, *, t_total):
    i = pl.program_id(0)
    n = pl.num_programs(0)
    tc, d = h_ref.shape
    groups = tc // SUBLANE

    def row_copy(slot, k, g, u, row):
        return pltpu.make_async_copy(y_ref.at[pl.ds(row, 1)], ybuf.at[slot, k, g, pl.ds(u, 1)],
                                     sems.at[slot])

    def issue(tile, slot):
        def one(g, carry):
            for k in range(TOP_K):
                base = k * t_total + tile * tc + g * SUBLANE
                for u in range(SUBLANE):
                    row_copy(slot, k, g, u, dest_ref[base + u]).start()
            return carry
        lax.fori_loop(0, groups, one, 0, unroll=4)

    @pl.when(i == 0)
    def _():
        for t0 in range(COMBINE_BUFS - 1):
            @pl.when(t0 < n)
            def _(t0=t0):
                issue(t0, t0)

    slot = i % COMBINE_BUFS

    def drain(g, carry):
        for _ in range(TOP_K * SUBLANE):
            row_copy(slot, 0, 0, 0, 0).wait()
        return carry
    lax.fori_loop(0, groups, drain, 0)

    @pl.when(i + COMBINE_BUFS - 1 < n)
    def _():
        issue(i + COMBINE_BUFS - 1, (i + COMBINE_BUFS - 1) % COMBINE_BUFS)

    p = p_ref[...]
    cw = min(d, 4 * LANE)
    ssq = jnp.zeros((tc, 1), F32)
    for c in range(d // cw):
        cols = slice(c * cw, (c + 1) * cw)
        acc = h_ref[:, cols]
        for k in range(TOP_K):
            acc = acc + p[:, k:k + 1] * ybuf[slot, k, :, :, cols].reshape(tc, cw)
        o_ref[:, cols] = acc
        ssq = ssq + jnp.sum(acc * acc, axis=-1, keepdims=True)
    scale = lax.rsqrt(ssq * (1.0 / d) + EPS)
    for c in range(d // cw):
        cols = slice(c * cw, (c + 1) * cw)
        o_ref[:, cols] = o_ref[:, cols] * scale * g_ref[:, cols]


def _combine(dest_flat, y_sorted, hres, p_t, final_g, *, tc):
    t, d = hres.shape
    return pl.pallas_call(
        functools.partial(_combine_body, t_total=t),
        grid_spec=pltpu.PrefetchScalarGridSpec(
            num_scalar_prefetch=1,
            grid=(t // tc,),
            in_specs=[pl.BlockSpec(memory_space=pl.ANY),
                      pl.BlockSpec((tc, d), lambda i, *_: (i, 0)),
                      pl.BlockSpec((tc, TOP_K), lambda i, *_: (i, 0)),
                      pl.BlockSpec((1, d), lambda i, *_: (0, 0))],
            out_specs=pl.BlockSpec((tc, d), lambda i, *_: (i, 0)),
            scratch_shapes=[pltpu.VMEM((COMBINE_BUFS, TOP_K, tc // SUBLANE, SUBLANE, d), F32),
                            pltpu.SemaphoreType.DMA((COMBINE_BUFS,))],
        ),
        out_shape=jax.ShapeDtypeStruct((t, d), F32),
        compiler_params=_cparams(1),
        name="moe_combine",
    )(dest_flat, y_sorted, hres, p_t, final_g.reshape(1, d))


def _pick(n, prefs):
    for p in prefs:
        if n % p == 0:
            return p
    raise ValueError(f"no tile in {prefs} divides {n}")


def kernel(x, meta_tokens, norm_mix_g, w_in, ssd_conv_w, ssd_conv_b, dt_bias, a_log, d_skip,
           ssd_norm_g, w_ssd_out, sc_conv_w, w_sc_out, w_o, norm_ffn_g, w_router, b_router,
           w_gate_up, b_gate_up, w_down, b_down, final_g):
    batch, seq, d = x.shape
    assert norm_mix_g.shape[0] == 1, "meta-token handling assumes a single layer"
    assert seq % SSD_CHUNK == 0 and N_META <= SSD_CHUNK
    t = batch * seq
    heads = d // SSD_HEAD_DIM
    gn = SSD_GROUPS * SSD_STATE
    conv_dim = d + 2 * gn
    xbc_off = d
    dt_off = xbc_off + conv_dim
    sc_off = dt_off + heads
    gate_off = sc_off + 3 * d
    ne = w_router.shape[-1]
    de = w_down.shape[2]

    x2 = x.reshape(t, d)
    w_in_nk = jnp.swapaxes(w_in[0], 0, 1)

    tm = _pick(seq, (1024, 512, 256, 128))
    tn = _pick(d, (512, 256, 128))

    h = _rmsnorm_bf16(x2, norm_mix_g[0], _pick(t, (256, 128)))
    h_meta = _rmsnorm_bf16(meta_tokens, norm_mix_g[0], N_META)

    proj = functools.partial(_matmul, h, w_in_nk, tm=tm, out_dtype=F32, w_is_nk=True)
    z = proj(0, d, tn=tn, name="proj_z")
    xbc, xbc_meta = _proj_conv(h, w_in_nk, [xbc_off], h_meta, ssd_conv_w[0],
                               ssd_conv_b[0].reshape(1, conv_dim), width=conv_dim, tm=tm,
                               tn=tn, seq=seq, mode="xbc", name="proj_xbc")
    dtr, dt_meta = proj(dt_off, LANE, tn=LANE, side=h_meta, name="proj_dt")
    u_sc = _proj_conv(h, w_in_nk, [sc_off, sc_off + d, sc_off + 2 * d], h_meta,
                      sc_conv_w[0], None, width=d, tm=tm, tn=_pick(d, (256, 128)), seq=seq,
                      mode="sc", name="proj_sc")
    gates = proj(gate_off, 2 * d, tn=tn, name="proj_gate")

    e_mat = (jnp.arange(d, dtype=I32)[None, :] // SSD_HEAD_DIM
             == jnp.arange(heads, dtype=I32)[:, None]).astype(BF16)
    e2 = jnp.concatenate([e_mat, e_mat], axis=0)
    pad = SSD_CHUNK - N_META
    dtb = dt_bias[0].reshape(1, heads)
    alog = a_log[0].reshape(1, heads)
    s0 = _ssd_meta(jnp.pad(xbc_meta, ((pad, 0), (0, 0))),
                   jnp.pad(dt_meta, ((pad, 0), (0, 0))), dtb, alog, e2, heads)
    dskip_x = jnp.repeat(d_skip[0], SSD_HEAD_DIM).reshape(1, d)
    g_ssd = _ssd_main(xbc, dtr, z, s0, dtb, alog, dskip_x, ssd_norm_g[0].reshape(1, d), e2,
                      heads=heads, batch=batch)

    y_a = _matmul(g_ssd, w_ssd_out[0], 0, d, tm=tm, tn=tn, out_dtype=F32,
                  extras=[(gates, 0)],
                  epilogue=lambda acc, ga: _sigmoid(ga) * acc, name="ssd_out")
    mixed = _matmul(u_sc, w_sc_out[0], 0, d, tm=tm, tn=tn, out_dtype=BF16,
                    extras=[(gates, d), (y_a, 0)],
                    epilogue=lambda acc, gb, ya: ya + _sigmoid(gb) * acc, name="sc_out")
    hres = _matmul(mixed, w_o[0], 0, d, tm=tm, tn=tn, out_dtype=F32,
                   extras=[(x2, 0)], epilogue=lambda acc, xr: xr + acc, name="w_o")

    rt = _pick(t, (512, 256, 128))
    hp, top_idx, probs, rank, counts = _router(hres, norm_ffn_g[0], w_router[0].T,
                                               b_router[0], rt)
    cnt = counts[:, 0].astype(I32)
    sb, n_sb, seg_start, n_slots = _moe_schedule(cnt, t * TOP_K, ne)
    e_ids = jnp.arange(ne, dtype=I32)[:, None, None]
    dest = rank + jnp.sum(jnp.where(top_idx[None] == e_ids, seg_start[:, None, None], 0),
                          axis=0)
    dest_flat = dest.reshape(-1)

    xs = _moe_dispatch(dest_flat, cnt, seg_start, sb[3], hp, n_slots,
                       tm=_pick(t, (1024, 512, 256, 128)))
    act = _moe_gate_up(sb, n_sb, xs, w_gate_up[0], b_gate_up[0],
                       tj=_pick(de, (384, 256, 128)))
    y_sorted = _moe_down(sb, n_sb, act, w_down[0], b_down[0],
                         tn=_pick(d, (1024, 512, 256, 128)))
    out = _combine(dest_flat, y_sorted, hres, probs.T, final_g, tc=_pick(t, (128,)))
    return out.reshape(batch, seq, d)
```

```python
import functools

import jax
import jax.numpy as jnp
from jax import lax
from jax.experimental import pallas as pl
from jax.experimental.pallas import tpu as pltpu

F32 = jnp.float32
BF16 = jnp.bfloat16
I32 = jnp.int32
U32 = jnp.uint32

N_META = 16
SSD_HEAD_DIM = 64
SSD_GROUPS = 8
SSD_STATE = 128
SSD_CONV = 4
SSD_CHUNK = 128
SC_CONV = 3
TOP_K = 4
SWIGLU_LIMIT = 7.0
SWIGLU_ALPHA = 1.702
EPS = 1e-5

LANE = 128
SUBLANE = 8
MOE_BLK = 128
MOE_SB_BLKS = 10
EPILOGUE_ROWS = 256
COMBINE_BUFS = 3
VMEM_MB = 56
VMEM_MB_WIDE = 60


def _cparams(n_axes, vmem_mb=VMEM_MB):
    return pltpu.CompilerParams(
        dimension_semantics=("arbitrary",) * n_axes,
        vmem_limit_bytes=vmem_mb * 1024 * 1024,
    )


def _sigmoid(v):
    return 1.0 / (1.0 + jnp.exp(-v))


def _rms_body(x_ref, g_ref, o_ref):
    x = x_ref[...]
    ms = jnp.mean(x * x, axis=-1, keepdims=True)
    o_ref[...] = (x * lax.rsqrt(ms + EPS) * g_ref[...]).astype(o_ref.dtype)


def _rmsnorm_bf16(x2, g, tm):
    m, d = x2.shape
    return pl.pallas_call(
        _rms_body,
        grid=(m // tm,),
        in_specs=[pl.BlockSpec((tm, d), lambda i: (i, 0)),
                  pl.BlockSpec((1, d), lambda i: (0, 0))],
        out_specs=pl.BlockSpec((tm, d), lambda i: (i, 0)),
        out_shape=jax.ShapeDtypeStruct((m, d), BF16),
        compiler_params=_cparams(1),
        name="rmsnorm",
    )(x2, g.reshape(1, d))


def _chunked_matmul(a_ref, wbf_ref, acc_ref, finish):
    tm = a_ref.shape[0]
    ch = acc_ref.shape[1]
    n = tm // ch

    def rows(c):
        return slice(c * ch, (c + 1) * ch)

    for c in range(n):
        if c > 0:
            finish(rows(c - 1), acc_ref[(c - 1) % 2])
        acc_ref[c % 2] = jnp.dot(a_ref[rows(c), :], wbf_ref[...], preferred_element_type=F32)
    finish(rows(n - 1), acc_ref[(n - 1) % 2])


def _mm_body(*refs, n_extra, has_side, epilogue, w_is_nk):
    it = iter(refs)
    a_ref = next(it)
    w_ref = next(it)
    side_ref = next(it) if has_side else None
    extras = [next(it) for _ in range(n_extra)]
    o_ref = next(it)
    oside_ref = next(it) if has_side else None
    plain = epilogue is None and not has_side
    wbf_ref = None if plain else next(it)
    acc_ref = next(it) if epilogue is not None else None

    def w_bf16():
        w = w_ref[...]
        return (w.T if w_is_nk else w).astype(BF16)

    if plain:
        o_ref[...] = jnp.dot(a_ref[...], w_bf16(),
                             preferred_element_type=F32).astype(o_ref.dtype)
        return

    @pl.when(pl.program_id(1) == 0)
    def _():
        wbf_ref[...] = w_bf16()
        if has_side:
            oside_ref[...] = jnp.dot(side_ref[...], wbf_ref[...],
                                     preferred_element_type=F32)

    if epilogue is None:
        o_ref[...] = jnp.dot(a_ref[...], wbf_ref[...],
                             preferred_element_type=F32).astype(o_ref.dtype)
    else:
        def finish(rows, acc):
            o_ref[rows, :] = epilogue(acc, *[e[rows, :] for e in extras]).astype(o_ref.dtype)
        _chunked_matmul(a_ref, wbf_ref, acc_ref, finish)


def _matmul(a, w, col_off, n_cols, *, tm, tn, out_dtype, side=None, extras=(),
            epilogue=None, w_is_nk=False, name="matmul"):
    m, k = a.shape
    assert m % tm == 0 and n_cols % tn == 0
    if w_is_nk:
        assert col_off % SUBLANE == 0
        w_spec = pl.BlockSpec((pl.Element(tn), pl.Element(k)),
                              lambda n, i: (pl.multiple_of(col_off + n * tn, SUBLANE), 0))
    else:
        assert col_off % tn == 0
        w_spec = pl.BlockSpec((k, tn), lambda n, i, cb=col_off // tn: (0, n + cb))
    in_specs = [pl.BlockSpec((tm, k), lambda n, i: (i, 0)), w_spec]
    args = [a, w]
    if side is not None:
        in_specs.append(pl.BlockSpec((side.shape[0], k), lambda n, i: (0, 0)))
        args.append(side)
    for arr, off in extras:
        assert off % tn == 0
        in_specs.append(pl.BlockSpec((tm, tn), lambda n, i, o=off // tn: (i, n + o)))
        args.append(arr)
    out_shape = [jax.ShapeDtypeStruct((m, n_cols), out_dtype)]
    out_specs = [pl.BlockSpec((tm, tn), lambda n, i: (i, n))]
    if side is not None:
        out_shape.append(jax.ShapeDtypeStruct((side.shape[0], n_cols), F32))
        out_specs.append(pl.BlockSpec((side.shape[0], tn), lambda n, i: (0, n)))
    res = pl.pallas_call(
        functools.partial(_mm_body, n_extra=len(extras), has_side=side is not None,
                          epilogue=epilogue, w_is_nk=w_is_nk),
        grid=(n_cols // tn, m // tm),
        in_specs=in_specs,
        out_specs=out_specs,
        out_shape=out_shape,
        scratch_shapes=([] if epilogue is None and side is None
                        else [pltpu.VMEM((k, tn), BF16)])
        + ([pltpu.VMEM((2, min(tm, EPILOGUE_ROWS), tn), F32)] if epilogue is not None else []),
        compiler_params=_cparams(2),
        name=name,
    )(*args)
    return res if side is not None else res[0]


def _causal_conv(raw, prev, w, n_tap):
    acc = raw * w[n_tap - 1:n_tap, :]
    row8 = lax.broadcasted_iota(I32, (SUBLANE, raw.shape[1]), 0)
    for s in range(1, n_tap):
        sh = pltpu.roll(raw, s, 0)
        shp = pltpu.roll(prev, s, 0)
        first = jnp.where(row8 < s, shp, sh[:SUBLANE])
        sh = jnp.concatenate([first, sh[SUBLANE:]], axis=0)
        acc = acc + sh * w[n_tap - 1 - s:n_tap - s, :]
    return acc


def _proj_conv_body(*refs, n_w, n_tap, tiles_per_batch, mode):
    it = iter(refs)
    a_ref = next(it)
    w_refs = [next(it) for _ in range(n_w)]
    side_ref = next(it)
    cw_ref = next(it)
    cb_ref = next(it) if mode == "xbc" else None
    o_ref = next(it)
    oside_ref = next(it) if mode == "xbc" else None
    wbf_ref = next(it)
    prev_ref = next(it)
    tail_ref = next(it)
    u_ref = next(it)
    tn = o_ref.shape[1]
    cw = cw_ref[...]

    def pre(acc):
        return acc if mode == "xbc" else acc[:, tn:2 * tn] * acc[:, 2 * tn:]

    def post(y, acc):
        if mode == "xbc":
            v = y + cb_ref[...]
            return v * _sigmoid(v)
        return acc[:, :tn] * y

    m = pl.program_id(1)

    @pl.when(m == 0)
    def _():
        for i, w_ref in enumerate(w_refs):
            wbf_ref[:, i * tn:(i + 1) * tn] = w_ref[...].T.astype(BF16)
        acc_m = jnp.dot(side_ref[...], wbf_ref[...], preferred_element_type=F32)
        u_m = pre(acc_m)
        tail_ref[...] = u_m[N_META - SUBLANE:, :]
        if mode == "xbc":
            y_m = _causal_conv(u_m, jnp.zeros((SUBLANE, tn), F32), cw, n_tap)
            oside_ref[...] = post(y_m, acc_m)

    @pl.when(m % tiles_per_batch == 0)
    def _():
        prev_ref[...] = tail_ref[...]

    ch = u_ref.shape[0] - SUBLANE
    for c in range(a_ref.shape[0] // ch):
        rows = slice(c * ch, (c + 1) * ch)
        acc = jnp.dot(a_ref[rows, :], wbf_ref[...], preferred_element_type=F32)
        u_ref[:SUBLANE, :] = prev_ref[...]
        u_ref[SUBLANE:, :] = pre(acc)
        y = None
        for k in range(n_tap):
            r0 = SUBLANE - (n_tap - 1) + k
            term = u_ref[r0:r0 + ch, :] * cw[k:k + 1, :]
            y = term if y is None else y + term
        prev_ref[...] = u_ref[ch:, :]
        o_ref[rows, :] = post(y, acc).astype(o_ref.dtype)


def _proj_conv(a, w_nk, row_offs, side, cw, cb, *, width, tm, tn, seq, mode, name):
    m, k = a.shape
    n_w = len(row_offs)
    n_tap = cw.shape[0]
    ms = side.shape[0]

    def w_spec(off):
        return pl.BlockSpec((pl.Element(tn), pl.Element(k)),
                            lambda n, i: (pl.multiple_of(off + n * tn, SUBLANE), 0))

    in_specs = [pl.BlockSpec((tm, k), lambda n, i: (i, 0))]
    in_specs += [w_spec(off) for off in row_offs]
    in_specs += [pl.BlockSpec((ms, k), lambda n, i: (0, 0)),
                 pl.BlockSpec((n_tap, tn), lambda n, i: (0, n))]
    args = [a] + [w_nk] * n_w + [side, cw]
    out_shape = [jax.ShapeDtypeStruct((m, width), F32 if mode == "xbc" else BF16)]
    out_specs = [pl.BlockSpec((tm, tn), lambda n, i: (i, n))]
    if mode == "xbc":
        in_specs.append(pl.BlockSpec((1, tn), lambda n, i: (0, n)))
        args.append(cb)
        out_shape.append(jax.ShapeDtypeStruct((ms, width), F32))
        out_specs.append(pl.BlockSpec((ms, tn), lambda n, i: (0, n)))
    res = pl.pallas_call(
        functools.partial(_proj_conv_body, n_w=n_w, n_tap=n_tap,
                          tiles_per_batch=seq // tm, mode=mode),
        grid=(width // tn, m // tm),
        in_specs=in_specs,
        out_specs=out_specs,
        out_shape=out_shape,
        scratch_shapes=[pltpu.VMEM((k, n_w * tn), BF16),
                        pltpu.VMEM((SUBLANE, tn), F32),
                        pltpu.VMEM((SUBLANE, tn), F32),
                        pltpu.VMEM((min(tm, EPILOGUE_ROWS) + SUBLANE, tn), F32)],
        compiler_params=_cparams(2, VMEM_MB_WIDE),
        name=name,
    )(*args)
    return res if mode == "xbc" else res[0]


def _ssd_chunk(xbc, dt_raw, dtb, alog, e2, *, heads, meta):
    L = xbc.shape[0]
    di = heads * SSD_HEAD_DIM
    gn = SSD_GROUPS * SSD_STATE
    dt_in = dt_raw[:, :heads] + dtb
    dtv = jnp.maximum(dt_in, 0.0) + jnp.log1p(jnp.exp(-jnp.abs(dt_in)))
    if meta:
        dtv = dtv * (lax.broadcasted_iota(I32, (L, 1), 0) >= L - N_META).astype(F32)
    xs = xbc[:, :di]
    bm = xbc[:, di:di + gn]
    cm = xbc[:, di + gn:]
    a_neg = -jnp.exp(alog)
    ad = dtv * a_neg
    r = lax.broadcasted_iota(I32, (L, L), 0)
    c = lax.broadcasted_iota(I32, (L, L), 1)
    tril = r >= c
    acum = jnp.dot(tril.astype(F32), ad, precision=lax.Precision.HIGHEST,
                   preferred_element_type=F32)
    stack = jnp.concatenate([dtv, dtv * jnp.exp(acum[L - 1:L, :] - acum), jnp.exp(acum)],
                            axis=0)
    hi = stack.astype(BF16)
    lo = (stack - hi.astype(F32)).astype(BF16)
    exp_x = jnp.dot(jnp.concatenate([hi, lo], axis=1), e2, preferred_element_type=F32)
    dt_x, dd_x, ea_x = exp_x[:L], exp_x[L:2 * L], exp_x[2 * L:]
    bt = bm.T.astype(BF16)
    return dict(xs=xs, cm=cm, bt=bt, acum=acum, ea_x=ea_x, x_dt=(xs * dt_x).astype(BF16),
                x_dec=(xs * dd_x).astype(BF16), tril=tril)


def _ssd_state_update(q, st_ref, heads):
    gw = heads * SSD_HEAD_DIM // SSD_GROUPS
    L = q["ea_x"].shape[0]
    for g in range(SSD_GROUPS):
        sl = slice(g * gw, (g + 1) * gw)
        btg = q["bt"][g * SSD_STATE:(g + 1) * SSD_STATE, :]
        new = jnp.dot(btg, q["x_dec"][:, sl], preferred_element_type=F32)
        st_ref[:, sl] = st_ref[:, sl] * q["ea_x"][L - 1:L, sl] + new


def _ssd_body(xbc_ref, dt_ref, z_ref, s0_ref, dtb_ref, alog_ref, dskip_ref, ng_ref, e_ref,
              o_ref, st_ref, *, heads):
    @pl.when(pl.program_id(1) == 0)
    def _():
        st_ref[...] = s0_ref[...]

    L = xbc_ref.shape[0]
    q = _ssd_chunk(xbc_ref[...], dt_ref[...], dtb_ref[...], alog_ref[...], e_ref[...],
                   heads=heads, meta=False)

    gw = heads * SSD_HEAD_DIM // SSD_GROUPS
    j_heads = heads // SSD_GROUPS
    acum = q["acum"]
    acum_t = acum.T
    cmb = q["cm"].astype(BF16)
    x_dt_b = q["x_dt"]
    lane = lax.broadcasted_iota(I32, (L, 2 * SSD_HEAD_DIM), 1)
    lo = lane < SSD_HEAD_DIM
    neg_inf = jnp.float32(-jnp.inf)

    def decay_mat(h):
        col = jnp.broadcast_to(acum[:, h:h + 1], (L, L))
        row = jnp.broadcast_to(acum_t[h:h + 1, :], (L, L))
        return jnp.exp(jnp.where(q["tril"], col - row, neg_inf))

    for g in range(SSD_GROUPS):
        sl = slice(g * gw, (g + 1) * gw)
        cg = cmb[:, g * SSD_STATE:(g + 1) * SSD_STATE]
        btg = q["bt"][g * SSD_STATE:(g + 1) * SSD_STATE, :]
        cb_mat = jnp.dot(cg, btg, preferred_element_type=F32)
        parts = []
        for p in range(j_heads // 2):
            h1 = g * j_heads + 2 * p
            slab = x_dt_b[:, h1 * SSD_HEAD_DIM:(h1 + 2) * SSD_HEAD_DIM]
            lhs = jnp.concatenate([(cb_mat * decay_mat(h1)).astype(BF16),
                                   (cb_mat * decay_mat(h1 + 1)).astype(BF16)], axis=1)
            zero = jnp.zeros_like(slab)
            rhs = jnp.concatenate([jnp.where(lo, slab, zero),
                                   jnp.where(lo, zero, slab)], axis=0)
            parts.append(jnp.dot(lhs, rhs, preferred_element_type=F32))
        y_diag = parts[0] if len(parts) == 1 else jnp.concatenate(parts, axis=1)
        st_g = st_ref[:, sl]
        y_off = jnp.dot(cg, st_g.astype(BF16), preferred_element_type=F32)
        y = y_diag + y_off * q["ea_x"][:, sl] + dskip_ref[:, sl] * q["xs"][:, sl]
        zg = z_ref[:, sl]
        gated = y * (zg * _sigmoid(zg))
        ms = jnp.mean(gated * gated, axis=-1, keepdims=True)
        o_ref[:, sl] = (gated * lax.rsqrt(ms + EPS) * ng_ref[:, sl]).astype(o_ref.dtype)
        new = jnp.dot(btg, q["x_dec"][:, sl], preferred_element_type=F32)
        st_ref[:, sl] = st_g * q["ea_x"][L - 1:L, sl] + new


def _ssd_meta_body(xbc_ref, dt_ref, dtb_ref, alog_ref, e_ref, st_ref, *, heads):
    st_ref[...] = jnp.zeros_like(st_ref)
    q = _ssd_chunk(xbc_ref[...], dt_ref[...], dtb_ref[...], alog_ref[...], e_ref[...],
                   heads=heads, meta=True)
    _ssd_state_update(q, st_ref, heads)


def _full_spec(shape):
    nd = len(shape)
    return pl.BlockSpec(shape, lambda *_: (0,) * nd)


def _ssd_meta(xbc_m, dt_m, dtb, alog, e2, heads):
    di = heads * SSD_HEAD_DIM
    args = [xbc_m, dt_m, dtb, alog, e2]
    return pl.pallas_call(
        functools.partial(_ssd_meta_body, heads=heads),
        grid=(1,),
        in_specs=[_full_spec(a.shape) for a in args],
        out_specs=_full_spec((SSD_STATE, di)),
        out_shape=jax.ShapeDtypeStruct((SSD_STATE, di), F32),
        compiler_params=_cparams(1),
        name="ssd_meta",
    )(*args)


def _ssd_main(xbc, dt, z, s0, dtb, alog, dskip_x, ng, e2, *, heads, batch):
    t, cd = xbc.shape
    di = heads * SSD_HEAD_DIM
    L = SSD_CHUNK
    nc = t // batch // L
    row = lambda b, c: (b * nc + c, 0)
    in_specs = [pl.BlockSpec((L, cd), row),
                pl.BlockSpec((L, LANE), row),
                pl.BlockSpec((L, di), row)]
    consts = [s0, dtb, alog, dskip_x, ng, e2]
    in_specs += [_full_spec(a.shape) for a in consts]
    return pl.pallas_call(
        functools.partial(_ssd_body, heads=heads),
        grid=(batch, nc),
        in_specs=in_specs,
        out_specs=pl.BlockSpec((L, di), row),
        out_shape=jax.ShapeDtypeStruct((t, di), BF16),
        scratch_shapes=[pltpu.VMEM((SSD_STATE, di), F32)],
        compiler_params=_cparams(2),
        name="ssd_main",
    )(xbc, dt, z, *consts)


def _router_body(x_ref, g_ref, wr_ref, br_ref, hp_ref, idx_ref, prob_ref, rank_ref,
                 cnt_ref, base_ref):
    @pl.when(pl.program_id(0) == 0)
    def _():
        base_ref[...] = jnp.zeros_like(base_ref)

    x = x_ref[...]
    tm, d = x.shape
    ne = wr_ref.shape[0]
    ms = jnp.mean(x * x, axis=-1, keepdims=True)
    h = x * lax.rsqrt(ms + EPS) * g_ref[...]
    half = d // 2
    h_hi = h.astype(BF16)
    h_hi32 = h_hi.astype(F32)
    lo_bits = lax.bitcast_convert_type(h_hi32[:, :half], U32)
    hi_bits = lax.bitcast_convert_type(h_hi32[:, half:], U32)
    hp_ref[...] = (lo_bits >> 16) | (hi_bits & jnp.uint32(0xFFFF0000))

    w = wr_ref[...]
    w_hi = w.astype(BF16)
    w_lo = (w - w_hi.astype(F32)).astype(BF16)
    h_lo = (h - h_hi32).astype(BF16)
    nt = (((1,), (1,)), ((), ()))
    part = lax.dot_general(jnp.concatenate([w_hi, w_lo], axis=0), h_hi, nt,
                           preferred_element_type=F32)
    logits = (part[:ne] + part[ne:]
              + lax.dot_general(w_hi, h_lo, nt, preferred_element_type=F32)
              + br_ref[:, 0:1])
    eidx = lax.broadcasted_iota(I32, (ne, tm), 0)
    work = logits
    vals, onehots = [], []
    for k in range(TOP_K):
        mx = jnp.max(work, axis=0, keepdims=True)
        sel = jnp.min(jnp.where(work == mx, eidx, ne), axis=0, keepdims=True)
        oh = eidx == sel
        idx_ref[k:k + 1, :] = sel
        vals.append(mx)
        onehots.append(oh)
        work = jnp.where(oh, -jnp.inf, work)
    exps = [jnp.exp(v - vals[0]) for v in vals]
    denom = exps[0]
    for e in exps[1:]:
        denom = denom + e
    for k in range(TOP_K):
        prob_ref[k:k + 1, :] = exps[k] / denom

    cnt = onehots[0].astype(F32)
    for oh in onehots[1:]:
        cnt = cnt + oh.astype(F32)
    r = lax.broadcasted_iota(I32, (tm, tm), 0)
    c = lax.broadcasted_iota(I32, (tm, tm), 1)
    before = (r < c).astype(BF16)
    prefix = jnp.dot(cnt.astype(BF16), before, preferred_element_type=F32)
    prefix = prefix + base_ref[:, 0:1]
    for k in range(TOP_K):
        rk = jnp.sum(jnp.where(onehots[k], prefix, 0.0), axis=0, keepdims=True)
        rank_ref[k:k + 1, :] = rk.astype(I32)
    base_ref[...] = base_ref[...] + jnp.sum(cnt, axis=1, keepdims=True)
    cnt_ref[...] = base_ref[...]


def _router(hres, g, wr_t, br, tm):
    t, d = hres.shape
    ne = wr_t.shape[0]
    tok = lambda i: (0, i)
    return pl.pallas_call(
        _router_body,
        grid=(t // tm,),
        in_specs=[pl.BlockSpec((tm, d), lambda i: (i, 0)),
                  _full_spec((1, d)), _full_spec((ne, d)), _full_spec((ne, LANE))],
        out_specs=[pl.BlockSpec((tm, d // 2), lambda i: (i, 0)),
                   pl.BlockSpec((TOP_K, tm), tok),
                   pl.BlockSpec((TOP_K, tm), tok),
                   pl.BlockSpec((TOP_K, tm), tok),
                   _full_spec((ne, LANE))],
        out_shape=[jax.ShapeDtypeStruct((t, d // 2), U32),
                   jax.ShapeDtypeStruct((TOP_K, t), I32),
                   jax.ShapeDtypeStruct((TOP_K, t), F32),
                   jax.ShapeDtypeStruct((TOP_K, t), I32),
                   jax.ShapeDtypeStruct((ne, LANE), F32)],
        scratch_shapes=[pltpu.VMEM((ne, LANE), F32)],
        compiler_params=_cparams(1),
        name="router",
    )(hres, g.reshape(1, d), wr_t, jnp.broadcast_to(br.reshape(ne, 1), (ne, LANE)))


def _dispatch_body(dest_ref, cnt_ref, seg_ref, tot_ref, hp_ref, xs_ref, zrow_ref, zblk_ref,
                   row_sem, pad_sem, *, t_total, ne, n_blocks):
    i = pl.program_id(0)
    tm = hp_ref.shape[0] * SUBLANE

    def pad_copy(slot):
        return pltpu.make_async_copy(zrow_ref.at[pl.ds(0, 1)], xs_ref.at[pl.ds(slot, 1)], pad_sem)

    def tail_copy(b):
        return pltpu.make_async_copy(
            zblk_ref, xs_ref.at[pl.ds(pl.multiple_of(b * MOE_BLK, MOE_BLK), MOE_BLK)], pad_sem)

    @pl.when(i == 0)
    def _():
        zrow_ref[...] = jnp.zeros_like(zrow_ref)
        zblk_ref[...] = jnp.zeros_like(zblk_ref)

        def expert(e, carry):
            used = cnt_ref[e]
            end = (used + MOE_BLK - 1) // MOE_BLK * MOE_BLK

            def start(r, c):
                pad_copy(seg_ref[e] + r).start()
                return c
            lax.fori_loop(used, end, start, 0)

            def wait(r, c):
                pad_copy(seg_ref[e] + r).wait()
                return c
            lax.fori_loop(used, end, wait, 0)
            return carry
        lax.fori_loop(0, ne, expert, 0)

        def tail_start(b, c):
            tail_copy(b).start()
            return c
        lax.fori_loop(tot_ref[0], n_blocks, tail_start, 0)

        def tail_wait(b, c):
            tail_copy(b).wait()
            return c
        lax.fori_loop(tot_ref[0], n_blocks, tail_wait, 0)

    def row_copy(g, u, slot):
        return pltpu.make_async_copy(hp_ref.at[g, pl.ds(u, 1)], xs_ref.at[pl.ds(slot, 1)],
                                     row_sem)

    def issue(g, c):
        for k in range(TOP_K):
            base = k * t_total + i * tm + g * SUBLANE
            for u in range(SUBLANE):
                row_copy(g, u, dest_ref[base + u]).start()
        return c
    lax.fori_loop(0, tm // SUBLANE, issue, 0, unroll=4)

    def drain(g, c):
        for _ in range(TOP_K * SUBLANE):
            row_copy(0, 0, 0).wait()
        return c
    lax.fori_loop(0, tm // SUBLANE, drain, 0)


def _moe_dispatch(dest_flat, cnt, seg_start, total_blk, hp, n_rows, *, tm):
    t, w = hp.shape
    ne = cnt.shape[0]
    return pl.pallas_call(
        functools.partial(_dispatch_body, t_total=t, ne=ne, n_blocks=n_rows // MOE_BLK),
        grid_spec=pltpu.PrefetchScalarGridSpec(
            num_scalar_prefetch=4,
            grid=(t // tm,),
            in_specs=[pl.BlockSpec((tm // SUBLANE, SUBLANE, w), lambda i, *_: (i, 0, 0))],
            out_specs=pl.BlockSpec(memory_space=pl.ANY),
            scratch_shapes=[pltpu.VMEM((SUBLANE, w), U32),
                            pltpu.VMEM((MOE_BLK, w), U32),
                            pltpu.SemaphoreType.DMA,
                            pltpu.SemaphoreType.DMA],
        ),
        out_shape=jax.ShapeDtypeStruct((n_rows, w), U32),
        compiler_params=_cparams(1),
        name="moe_dispatch",
    )(dest_flat, cnt, seg_start, total_blk, hp.reshape(t // SUBLANE, SUBLANE, w))


def _sb_specs(rows, width_in, n_tiles):
    def x_map(s, j, e, start, nblk, tot):
        return (pl.multiple_of(start[s], MOE_BLK), 0)

    def wcol(s, j, nblk):
        return jnp.where(nblk[s] > 0, j, n_tiles - 1)

    return pl.BlockSpec((pl.Element(rows), pl.Element(width_in)), x_map), wcol


def _sb_run(start_ref, nblk_ref, tot_ref, o_ref, stage_ref, zblk_ref, out_sems, zero_sem,
            compute_rows, *, n_blocks):
    s = pl.program_id(0)
    j = pl.program_id(1)
    nj = pl.num_programs(1)
    q = s * nj + j
    slot = q % 2
    tile = stage_ref.shape[2]
    col = pl.multiple_of(j * tile, LANE)
    nb = nblk_ref[s]

    def out_copy(sl, b, row0):
        r0 = pl.multiple_of(b * MOE_BLK, MOE_BLK)
        return pltpu.make_async_copy(
            stage_ref.at[sl, pl.ds(r0, MOE_BLK)],
            o_ref.at[pl.ds(pl.multiple_of(row0 + r0, MOE_BLK), MOE_BLK), pl.ds(col, tile)],
            out_sems.at[sl])

    def drain(sl, count):
        def one(b, c):
            out_copy(sl, 0, 0).wait()
            return c
        lax.fori_loop(0, count, one, 0)

    def zero_copy(b):
        return pltpu.make_async_copy(
            zblk_ref,
            o_ref.at[pl.ds(pl.multiple_of(b * MOE_BLK, MOE_BLK), MOE_BLK), pl.ds(col, tile)],
            zero_sem)

    @pl.when(q >= 2)
    def _():
        drain(slot, nblk_ref[(q - 2) // nj])

    @pl.when(s == 0)
    def _():
        zblk_ref[...] = jnp.zeros_like(zblk_ref)

        def zs(b, c):
            zero_copy(b).start()
            return c
        lax.fori_loop(tot_ref[0], n_blocks, zs, 0)

        def zw(b, c):
            zero_copy(b).wait()
            return c
        lax.fori_loop(tot_ref[0], n_blocks, zw, 0)

    def run(b0, k):
        compute_rows(slot, pl.multiple_of(b0 * MOE_BLK, MOE_BLK), k * MOE_BLK)
        for i in range(k):
            out_copy(slot, b0 + i, start_ref[s]).start()

    k = 1 << (MOE_SB_BLKS.bit_length() - 1)
    while k >= 1:
        @pl.when(nb & k != 0)
        def _(k=k):
            run(nb & ~(2 * k - 1), k)
        k //= 2

    @pl.when(q == pl.num_programs(0) * nj - 1)
    def _():
        drain(slot, nb)

        @pl.when(q >= 1)
        def _():
            drain(1 - slot, nblk_ref[(q - 1) // nj])


def _gate_up_body(e_ref, start_ref, nblk_ref, tot_ref, x_ref, wg_ref, wu_ref, bg_ref, bu_ref,
                  o_ref, stage_ref, zblk_ref, out_sems, zero_sem, *, n_blocks):
    tj = wg_ref.shape[1]
    bias = jnp.concatenate([bg_ref[...], bu_ref[...]], axis=1)

    def compute_rows(slot, r0, n_rows):
        xp = x_ref[pl.ds(r0, n_rows), :]
        x_lo = lax.bitcast_convert_type(xp << 16, F32).astype(BF16)
        x_hi = lax.bitcast_convert_type(xp & jnp.uint32(0xFFFF0000), F32).astype(BF16)
        x = jnp.concatenate([x_lo, x_hi], axis=1)
        w = jnp.concatenate([wg_ref[...], wu_ref[...]], axis=1).astype(BF16)
        gu = jnp.dot(x, w, preferred_element_type=F32) + bias
        gate = jnp.minimum(gu[:, :tj], SWIGLU_LIMIT)
        up = jnp.clip(gu[:, tj:], -SWIGLU_LIMIT, SWIGLU_LIMIT)
        act = (up + 1.0) * (gate * _sigmoid(gate * SWIGLU_ALPHA))
        stage_ref[slot, pl.ds(r0, n_rows), :] = act.astype(stage_ref.dtype)

    _sb_run(start_ref, nblk_ref, tot_ref, o_ref, stage_ref, zblk_ref, out_sems, zero_sem,
            compute_rows, n_blocks=n_blocks)


def _sb_scratch(rows, tile, dtype):
    return [pltpu.VMEM((2, rows, tile), dtype), pltpu.VMEM((MOE_BLK, tile), dtype),
            pltpu.SemaphoreType.DMA((2,)), pltpu.SemaphoreType.DMA]


def _moe_gate_up(sb, n_sb, xs, w_gu, b_gu, *, tj):
    ne, d, de2 = w_gu.shape
    de = de2 // 2
    nj = de // tj
    rows = MOE_SB_BLKS * MOE_BLK
    n_rows, half = xs.shape
    x_spec, wcol = _sb_specs(rows, half, nj)
    in_specs = [
        x_spec,
        pl.BlockSpec((None, d, tj), lambda s, j, e, st, nb, tt: (e[s], 0, wcol(s, j, nb))),
        pl.BlockSpec((None, d, tj), lambda s, j, e, st, nb, tt: (e[s], 0, nj + wcol(s, j, nb))),
        pl.BlockSpec((None, 1, tj), lambda s, j, e, st, nb, tt: (e[s], 0, wcol(s, j, nb))),
        pl.BlockSpec((None, 1, tj), lambda s, j, e, st, nb, tt: (e[s], 0, nj + wcol(s, j, nb))),
    ]
    return pl.pallas_call(
        functools.partial(_gate_up_body, n_blocks=n_rows // MOE_BLK),
        grid_spec=pltpu.PrefetchScalarGridSpec(
            num_scalar_prefetch=4,
            grid=(n_sb, nj),
            in_specs=in_specs,
            out_specs=pl.BlockSpec(memory_space=pl.ANY),
            scratch_shapes=_sb_scratch(rows, tj, BF16),
        ),
        out_shape=jax.ShapeDtypeStruct((n_rows, de), BF16),
        compiler_params=_cparams(2, VMEM_MB_WIDE),
        name="moe_gate_up",
    )(*sb, xs, w_gu, w_gu, b_gu.reshape(ne, 1, de2), b_gu.reshape(ne, 1, de2))


def _down_body(e_ref, start_ref, nblk_ref, tot_ref, a_ref, w_ref, b_ref, o_ref,
               stage_ref, zblk_ref, out_sems, zero_sem, *, n_blocks):
    bias = b_ref[...]

    def compute_rows(slot, r0, n_rows):
        stage_ref[slot, pl.ds(r0, n_rows), :] = jnp.dot(
            a_ref[pl.ds(r0, n_rows), :], w_ref[...].astype(BF16),
            preferred_element_type=F32) + bias

    _sb_run(start_ref, nblk_ref, tot_ref, o_ref, stage_ref, zblk_ref, out_sems, zero_sem,
            compute_rows, n_blocks=n_blocks)


def _moe_down(sb, n_sb, act, w_down, b_down, *, tn):
    ne, de, d = w_down.shape
    nn = d // tn
    rows = MOE_SB_BLKS * MOE_BLK
    n_rows = act.shape[0]
    a_spec, wcol = _sb_specs(rows, de, nn)
    in_specs = [
        a_spec,
        pl.BlockSpec((None, de, tn), lambda s, j, e, st, nb, tt: (e[s], 0, wcol(s, j, nb))),
        pl.BlockSpec((None, 1, tn), lambda s, j, e, st, nb, tt: (e[s], 0, wcol(s, j, nb))),
    ]
    return pl.pallas_call(
        functools.partial(_down_body, n_blocks=n_rows // MOE_BLK),
        grid_spec=pltpu.PrefetchScalarGridSpec(
            num_scalar_prefetch=4,
            grid=(n_sb, nn),
            in_specs=in_specs,
            out_specs=pl.BlockSpec(memory_space=pl.ANY),
            scratch_shapes=_sb_scratch(rows, tn, F32),
        ),
        out_shape=jax.ShapeDtypeStruct((n_rows, d), F32),
        compiler_params=_cparams(2),
        name="moe_down",
    )(*sb, act, w_down, b_down.reshape(ne, 1, d))


def _moe_schedule(cnt, n_pairs, ne):
    n_blocks = n_pairs // MOE_BLK + ne + MOE_SB_BLKS - 1
    s_max = -(-(n_pairs // MOE_BLK + ne) // MOE_SB_BLKS) + ne
    pblk = (cnt + MOE_BLK - 1) // MOE_BLK
    seg_end_blk = jnp.cumsum(pblk)
    seg_start = (seg_end_blk - pblk) * MOE_BLK
    total_blk = seg_end_blk[-1:]
    nsb_e = (pblk + MOE_SB_BLKS - 1) // MOE_SB_BLKS
    sb_end = jnp.cumsum(nsb_e)
    sb_first = sb_end - nsb_e
    n_sb = sb_end[-1]
    s_ids = jnp.arange(s_max, dtype=I32)
    used = s_ids < n_sb
    e_of_s = jnp.minimum(jnp.searchsorted(sb_end, s_ids, side="right"), ne - 1).astype(I32)
    i_in_e = s_ids - sb_first[e_of_s]
    start = seg_start[e_of_s] + i_in_e * (MOE_SB_BLKS * MOE_BLK)
    nblk = jnp.where(used, jnp.clip(pblk[e_of_s] - MOE_SB_BLKS * i_in_e, 0, MOE_SB_BLKS), 0)
    last = jnp.maximum(n_sb - 1, 0)
    sb = (jnp.where(used, e_of_s, e_of_s[last]).astype(I32),
          jnp.where(used, start, start[last]).astype(I32), nblk.astype(I32),
          total_blk.astype(I32))
    return sb, n_sb.astype(I32), seg_start.astype(I32), n_blocks * MOE_BLK


def _combine_body(dest_ref, y_ref, h_ref, p_ref, g_ref, o_ref, ybuf, sems, *, t_total):
    i = pl.program_id(0)
    n = pl.num_programs(0)
    tc, d = h_ref.shape
    groups = tc // SUBLANE

    def row_copy(slot, k, g, u, row):
        return pltpu.make_async_copy(y_ref.at[pl.ds(row, 1)], ybuf.at[slot, k, g, pl.ds(u, 1)],
                                     sems.at[slot])

    def issue(tile, slot):
        def one(g, carry):
            for k in range(TOP_K):
                base = k * t_total + tile * tc + g * SUBLANE
                for u in range(SUBLANE):
                    row_copy(slot, k, g, u, dest_ref[base + u]).start()
            return carry
        lax.fori_loop(0, groups, one, 0, unroll=4)

    @pl.when(i == 0)
    def _():
        for t0 in range(COMBINE_BUFS - 1):
            @pl.when(t0 < n)
            def _(t0=t0):
                issue(t0, t0)

    slot = i % COMBINE_BUFS

    def drain(g, carry):
        for _ in range(TOP_K * SUBLANE):
            row_copy(slot, 0, 0, 0, 0).wait()
        return carry
    lax.fori_loop(0, groups, drain, 0)

    @pl.when(i + COMBINE_BUFS - 1 < n)
    def _():
        issue(i + COMBINE_BUFS - 1, (i + COMBINE_BUFS - 1) % COMBINE_BUFS)

    p = p_ref[...]
    cw = min(d, 4 * LANE)
    ssq = jnp.zeros((tc, 1), F32)
    for c in range(d // cw):
        cols = slice(c * cw, (c + 1) * cw)
        acc = h_ref[:, cols]
        for k in range(TOP_K):
            acc = acc + p[:, k:k + 1] * ybuf[slot, k, :, :, cols].reshape(tc, cw)
        o_ref[:, cols] = acc
        ssq = ssq + jnp.sum(acc * acc, axis=-1, keepdims=True)
    scale = lax.rsqrt(ssq * (1.0 / d) + EPS)
    for c in range(d // cw):
        cols = slice(c * cw, (c + 1) * cw)
        o_ref[:, cols] = o_ref[:, cols] * scale * g_ref[:, cols]


def _combine(dest_flat, y_sorted, hres, p_t, final_g, *, tc):
    t, d = hres.shape
    return pl.pallas_call(
        functools.partial(_combine_body, t_total=t),
        grid_spec=pltpu.PrefetchScalarGridSpec(
            num_scalar_prefetch=1,
            grid=(t // tc,),
            in_specs=[pl.BlockSpec(memory_space=pl.ANY),
                      pl.BlockSpec((tc, d), lambda i, *_: (i, 0)),
                      pl.BlockSpec((tc, TOP_K), lambda i, *_: (i, 0)),
                      pl.BlockSpec((1, d), lambda i, *_: (0, 0))],
            out_specs=pl.BlockSpec((tc, d), lambda i, *_: (i, 0)),
            scratch_shapes=[pltpu.VMEM((COMBINE_BUFS, TOP_K, tc // SUBLANE, SUBLANE, d), F32),
                            pltpu.SemaphoreType.DMA((COMBINE_BUFS,))],
        ),
        out_shape=jax.ShapeDtypeStruct((t, d), F32),
        compiler_params=_cparams(1),
        name="moe_combine",
    )(dest_flat, y_sorted, hres, p_t, final_g.reshape(1, d))


def _pick(n, prefs):
    for p in prefs:
        if n % p == 0:
            return p
    raise ValueError(f"no tile in {prefs} divides {n}")


def kernel(x, meta_tokens, norm_mix_g, w_in, ssd_conv_w, ssd_conv_b, dt_bias, a_log, d_skip,
           ssd_norm_g, w_ssd_out, sc_conv_w, w_sc_out, w_o, norm_ffn_g, w_router, b_router,
           w_gate_up, b_gate_up, w_down, b_down, final_g):
    batch, seq, d = x.shape
    assert norm_mix_g.shape[0] == 1, "meta-token handling assumes a single layer"
    assert seq % SSD_CHUNK == 0 and N_META <= SSD_CHUNK
    t = batch * seq
    heads = d // SSD_HEAD_DIM
    gn = SSD_GROUPS * SSD_STATE
    conv_dim = d + 2 * gn
    xbc_off = d
    dt_off = xbc_off + conv_dim
    sc_off = dt_off + heads
    gate_off = sc_off + 3 * d
    ne = w_router.shape[-1]
    de = w_down.shape[2]

    x2 = x.reshape(t, d)
    w_in_nk = jnp.swapaxes(w_in[0], 0, 1)

    tm = _pick(seq, (1024, 512, 256, 128))
    tn = _pick(d, (512, 256, 128))

    h = _rmsnorm_bf16(x2, norm_mix_g[0], _pick(t, (256, 128)))
    h_meta = _rmsnorm_bf16(meta_tokens, norm_mix_g[0], N_META)

    proj = functools.partial(_matmul, h, w_in_nk, tm=tm, out_dtype=F32, w_is_nk=True)
    z = proj(0, d, tn=tn, name="proj_z")
    xbc, xbc_meta = _proj_conv(h, w_in_nk, [xbc_off], h_meta, ssd_conv_w[0],
                               ssd_conv_b[0].reshape(1, conv_dim), width=conv_dim, tm=tm,
                               tn=_pick(conv_dim, (768, 512, 256, 128)), seq=seq, mode="xbc",
                               name="proj_xbc")
    dtr, dt_meta = proj(dt_off, LANE, tn=LANE, side=h_meta, name="proj_dt")
    u_sc = _proj_conv(h, w_in_nk, [sc_off, sc_off + d, sc_off + 2 * d], h_meta,
                      sc_conv_w[0], None, width=d, tm=tm, tn=_pick(d, (256, 128)), seq=seq,
                      mode="sc", name="proj_sc")
    gates = proj(gate_off, 2 * d, tn=tn, name="proj_gate")

    e_mat = (jnp.arange(d, dtype=I32)[None, :] // SSD_HEAD_DIM
             == jnp.arange(heads, dtype=I32)[:, None]).astype(BF16)
    e2 = jnp.concatenate([e_mat, e_mat], axis=0)
    pad = SSD_CHUNK - N_META
    dtb = dt_bias[0].reshape(1, heads)
    alog = a_log[0].reshape(1, heads)
    s0 = _ssd_meta(jnp.pad(xbc_meta, ((pad, 0), (0, 0))),
                   jnp.pad(dt_meta, ((pad, 0), (0, 0))), dtb, alog, e2, heads)
    dskip_x = jnp.repeat(d_skip[0], SSD_HEAD_DIM).reshape(1, d)
    g_ssd = _ssd_main(xbc, dtr, z, s0, dtb, alog, dskip_x, ssd_norm_g[0].reshape(1, d), e2,
                      heads=heads, batch=batch)

    y_a = _matmul(g_ssd, w_ssd_out[0], 0, d, tm=tm, tn=tn, out_dtype=F32,
                  extras=[(gates, 0)],
                  epilogue=lambda acc, ga: _sigmoid(ga) * acc, name="ssd_out")
    mixed = _matmul(u_sc, w_sc_out[0], 0, d, tm=tm, tn=tn, out_dtype=BF16,
                    extras=[(gates, d), (y_a, 0)],
                    epilogue=lambda acc, gb, ya: ya + _sigmoid(gb) * acc, name="sc_out")
    hres = _matmul(mixed, w_o[0], 0, d, tm=tm, tn=tn, out_dtype=F32,
                   extras=[(x2, 0)], epilogue=lambda acc, xr: xr + acc, name="w_o")

    rt = _pick(t, (512, 256, 128))
    hp, top_idx, probs, rank, counts = _router(hres, norm_ffn_g[0], w_router[0].T,
                                               b_router[0], rt)
    cnt = counts[:, 0].astype(I32)
    sb, n_sb, seg_start, n_slots = _moe_schedule(cnt, t * TOP_K, ne)
    e_ids = jnp.arange(ne, dtype=I32)[:, None, None]
    dest = rank + jnp.sum(jnp.where(top_idx[None] == e_ids, seg_start[:, None, None], 0),
                          axis=0)
    dest_flat = dest.reshape(-1)

    xs = _moe_dispatch(dest_flat, cnt, seg_start, sb[3], hp, n_slots,
                       tm=_pick(t, (1024, 512, 256, 128)))
    act = _moe_gate_up(sb, n_sb, xs, w_gate_up[0], b_gate_up[0],
                       tj=_pick(de, (384, 256, 128)))
    y_sorted = _moe_down(sb, n_sb, act, w_down[0], b_down[0],
                         tn=_pick(d, (1024, 512, 256, 128)))
    out = _combine(dest_flat, y_sorted, hres, probs.T, final_g, tc=_pick(t, (128,)))
    return out.reshape(batch, seq, d)
```

```python
import functools

import jax
import jax.numpy as jnp
from jax import lax
from jax.experimental import pallas as pl
from jax.experimental.pallas import tpu as pltpu

F32 = jnp.float32
BF16 = jnp.bfloat16
I32 = jnp.int32
U32 = jnp.uint32

N_META = 16
SSD_HEAD_DIM = 64
SSD_GROUPS = 8
SSD_STATE = 128
SSD_CONV = 4
SSD_CHUNK = 128
SC_CONV = 3
TOP_K = 4
SWIGLU_LIMIT = 7.0
SWIGLU_ALPHA = 1.702
EPS = 1e-5

LANE = 128
SUBLANE = 8
MOE_BLK = 128
MOE_SB_BLKS = 10
EPILOGUE_ROWS = 256
COMBINE_BUFS = 3
VMEM_MB = 56
VMEM_MB_WIDE = 60


def _cparams(n_axes, vmem_mb=VMEM_MB):
    return pltpu.CompilerParams(
        dimension_semantics=("arbitrary",) * n_axes,
        vmem_limit_bytes=vmem_mb * 1024 * 1024,
    )


def _sigmoid(v):
    return 1.0 / (1.0 + jnp.exp(-v))


def _rms_body(x_ref, g_ref, o_ref):
    x = x_ref[...]
    ms = jnp.mean(x * x, axis=-1, keepdims=True)
    o_ref[...] = (x * lax.rsqrt(ms + EPS) * g_ref[...]).astype(o_ref.dtype)


def _rmsnorm_bf16(x2, g, tm):
    m, d = x2.shape
    return pl.pallas_call(
        _rms_body,
        grid=(m // tm,),
        in_specs=[pl.BlockSpec((tm, d), lambda i: (i, 0)),
                  pl.BlockSpec((1, d), lambda i: (0, 0))],
        out_specs=pl.BlockSpec((tm, d), lambda i: (i, 0)),
        out_shape=jax.ShapeDtypeStruct((m, d), BF16),
        compiler_params=_cparams(1),
        name="rmsnorm",
    )(x2, g.reshape(1, d))


def _chunked_matmul(a_ref, wbf_ref, acc_ref, finish):
    tm = a_ref.shape[0]
    ch = acc_ref.shape[1]
    n = tm // ch

    def rows(c):
        return slice(c * ch, (c + 1) * ch)

    for c in range(n):
        if c > 0:
            finish(rows(c - 1), acc_ref[(c - 1) % 2])
        acc_ref[c % 2] = jnp.dot(a_ref[rows(c), :], wbf_ref[...], preferred_element_type=F32)
    finish(rows(n - 1), acc_ref[(n - 1) % 2])


def _mm_body(*refs, n_extra, has_side, epilogue, w_is_nk):
    it = iter(refs)
    a_ref = next(it)
    w_ref = next(it)
    side_ref = next(it) if has_side else None
    extras = [next(it) for _ in range(n_extra)]
    o_ref = next(it)
    oside_ref = next(it) if has_side else None
    plain = epilogue is None and not has_side
    wbf_ref = None if plain else next(it)
    acc_ref = next(it) if epilogue is not None else None

    def w_bf16():
        w = w_ref[...]
        return (w.T if w_is_nk else w).astype(BF16)

    if plain:
        o_ref[...] = jnp.dot(a_ref[...], w_bf16(),
                             preferred_element_type=F32).astype(o_ref.dtype)
        return

    @pl.when(pl.program_id(1) == 0)
    def _():
        wbf_ref[...] = w_bf16()
        if has_side:
            oside_ref[...] = jnp.dot(side_ref[...], wbf_ref[...],
                                     preferred_element_type=F32)

    if epilogue is None:
        o_ref[...] = jnp.dot(a_ref[...], wbf_ref[...],
                             preferred_element_type=F32).astype(o_ref.dtype)
    else:
        def finish(rows, acc):
            o_ref[rows, :] = epilogue(acc, *[e[rows, :] for e in extras]).astype(o_ref.dtype)
        _chunked_matmul(a_ref, wbf_ref, acc_ref, finish)


def _matmul(a, w, col_off, n_cols, *, tm, tn, out_dtype, side=None, extras=(),
            epilogue=None, w_is_nk=False, name="matmul"):
    m, k = a.shape
    assert m % tm == 0 and n_cols % tn == 0
    if w_is_nk:
        assert col_off % SUBLANE == 0
        w_spec = pl.BlockSpec((pl.Element(tn), pl.Element(k)),
                              lambda n, i: (pl.multiple_of(col_off + n * tn, SUBLANE), 0))
    else:
        assert col_off % tn == 0
        w_spec = pl.BlockSpec((k, tn), lambda n, i, cb=col_off // tn: (0, n + cb))
    in_specs = [pl.BlockSpec((tm, k), lambda n, i: (i, 0)), w_spec]
    args = [a, w]
    if side is not None:
        in_specs.append(pl.BlockSpec((side.shape[0], k), lambda n, i: (0, 0)))
        args.append(side)
    for arr, off in extras:
        assert off % tn == 0
        in_specs.append(pl.BlockSpec((tm, tn), lambda n, i, o=off // tn: (i, n + o)))
        args.append(arr)
    out_shape = [jax.ShapeDtypeStruct((m, n_cols), out_dtype)]
    out_specs = [pl.BlockSpec((tm, tn), lambda n, i: (i, n))]
    if side is not None:
        out_shape.append(jax.ShapeDtypeStruct((side.shape[0], n_cols), F32))
        out_specs.append(pl.BlockSpec((side.shape[0], tn), lambda n, i: (0, n)))
    res = pl.pallas_call(
        functools.partial(_mm_body, n_extra=len(extras), has_side=side is not None,
                          epilogue=epilogue, w_is_nk=w_is_nk),
        grid=(n_cols // tn, m // tm),
        in_specs=in_specs,
        out_specs=out_specs,
        out_shape=out_shape,
        scratch_shapes=([] if epilogue is None and side is None
                        else [pltpu.VMEM((k, tn), BF16)])
        + ([pltpu.VMEM((2, min(tm, EPILOGUE_ROWS), tn), F32)] if epilogue is not None else []),
        compiler_params=_cparams(2),
        name=name,
    )(*args)
    return res if side is not None else res[0]


def _causal_conv(raw, prev, w, n_tap):
    acc = raw * w[n_tap - 1:n_tap, :]
    row8 = lax.broadcasted_iota(I32, (SUBLANE, raw.shape[1]), 0)
    for s in range(1, n_tap):
        sh = pltpu.roll(raw, s, 0)
        shp = pltpu.roll(prev, s, 0)
        first = jnp.where(row8 < s, shp, sh[:SUBLANE])
        sh = jnp.concatenate([first, sh[SUBLANE:]], axis=0)
        acc = acc + sh * w[n_tap - 1 - s:n_tap - s, :]
    return acc


def _proj_conv_body(*refs, n_w, n_tap, tiles_per_batch, mode):
    it = iter(refs)
    a_ref = next(it)
    w_refs = [next(it) for _ in range(n_w)]
    side_ref = next(it)
    cw_ref = next(it)
    cb_ref = next(it) if mode == "xbc" else None
    o_ref = next(it)
    oside_ref = next(it) if mode == "xbc" else None
    wbf_ref = next(it)
    prev_ref = next(it)
    tail_ref = next(it)
    u_ref = next(it)
    tn = o_ref.shape[1]
    cw = cw_ref[...]

    def pre(acc):
        return acc if mode == "xbc" else acc[:, tn:2 * tn] * acc[:, 2 * tn:]

    def post(y, acc):
        if mode == "xbc":
            v = y + cb_ref[...]
            return v * _sigmoid(v)
        return acc[:, :tn] * y

    m = pl.program_id(1)

    @pl.when(m == 0)
    def _():
        for i, w_ref in enumerate(w_refs):
            wbf_ref[:, i * tn:(i + 1) * tn] = w_ref[...].T.astype(BF16)
        acc_m = jnp.dot(side_ref[...], wbf_ref[...], preferred_element_type=F32)
        u_m = pre(acc_m)
        tail_ref[...] = u_m[N_META - SUBLANE:, :]
        if mode == "xbc":
            y_m = _causal_conv(u_m, jnp.zeros((SUBLANE, tn), F32), cw, n_tap)
            oside_ref[...] = post(y_m, acc_m)

    @pl.when(m % tiles_per_batch == 0)
    def _():
        prev_ref[...] = tail_ref[...]

    ch = u_ref.shape[0] - SUBLANE
    for c in range(a_ref.shape[0] // ch):
        rows = slice(c * ch, (c + 1) * ch)
        acc = jnp.dot(a_ref[rows, :], wbf_ref[...], preferred_element_type=F32)
        u_ref[:SUBLANE, :] = prev_ref[...]
        u_ref[SUBLANE:, :] = pre(acc)
        y = None
        for k in range(n_tap):
            r0 = SUBLANE - (n_tap - 1) + k
            term = u_ref[r0:r0 + ch, :] * cw[k:k + 1, :]
            y = term if y is None else y + term
        prev_ref[...] = u_ref[ch:, :]
        o_ref[rows, :] = post(y, acc).astype(o_ref.dtype)


def _proj_conv(a, w_nk, row_offs, side, cw, cb, *, width, tm, tn, seq, mode, name):
    m, k = a.shape
    n_w = len(row_offs)
    n_tap = cw.shape[0]
    ms = side.shape[0]

    def w_spec(off):
        return pl.BlockSpec((pl.Element(tn), pl.Element(k)),
                            lambda n, i: (pl.multiple_of(off + n * tn, SUBLANE), 0))

    in_specs = [pl.BlockSpec((tm, k), lambda n, i: (i, 0))]
    in_specs += [w_spec(off) for off in row_offs]
    in_specs += [pl.BlockSpec((ms, k), lambda n, i: (0, 0)),
                 pl.BlockSpec((n_tap, tn), lambda n, i: (0, n))]
    args = [a] + [w_nk] * n_w + [side, cw]
    out_shape = [jax.ShapeDtypeStruct((m, width), F32 if mode == "xbc" else BF16)]
    out_specs = [pl.BlockSpec((tm, tn), lambda n, i: (i, n))]
    if mode == "xbc":
        in_specs.append(pl.BlockSpec((1, tn), lambda n, i: (0, n)))
        args.append(cb)
        out_shape.append(jax.ShapeDtypeStruct((ms, width), F32))
        out_specs.append(pl.BlockSpec((ms, tn), lambda n, i: (0, n)))
    res = pl.pallas_call(
        functools.partial(_proj_conv_body, n_w=n_w, n_tap=n_tap,
                          tiles_per_batch=seq // tm, mode=mode),
        grid=(width // tn, m // tm),
        in_specs=in_specs,
        out_specs=out_specs,
        out_shape=out_shape,
        scratch_shapes=[pltpu.VMEM((k, n_w * tn), BF16),
                        pltpu.VMEM((SUBLANE, tn), F32),
                        pltpu.VMEM((SUBLANE, tn), F32),
                        pltpu.VMEM((min(tm, EPILOGUE_ROWS) + SUBLANE, tn), F32)],
        compiler_params=_cparams(2, VMEM_MB_WIDE),
        name=name,
    )(*args)
    return res if mode == "xbc" else res[0]


def _ssd_chunk(xbc, dt_raw, dtb, alog, e2, *, heads, meta):
    L = xbc.shape[0]
    di = heads * SSD_HEAD_DIM
    gn = SSD_GROUPS * SSD_STATE
    dt_in = dt_raw[:, :heads] + dtb
    dtv = jnp.maximum(dt_in, 0.0) + jnp.log1p(jnp.exp(-jnp.abs(dt_in)))
    if meta:
        dtv = dtv * (lax.broadcasted_iota(I32, (L, 1), 0) >= L - N_META).astype(F32)
    xs = xbc[:, :di]
    bm = xbc[:, di:di + gn]
    cm = xbc[:, di + gn:]
    a_neg = -jnp.exp(alog)
    ad = dtv * a_neg
    r = lax.broadcasted_iota(I32, (L, L), 0)
    c = lax.broadcasted_iota(I32, (L, L), 1)
    tril = r >= c
    acum = jnp.dot(tril.astype(F32), ad, precision=lax.Precision.HIGHEST,
                   preferred_element_type=F32)
    stack = jnp.concatenate([dtv, dtv * jnp.exp(acum[L - 1:L, :] - acum), jnp.exp(acum)],
                            axis=0)
    hi = stack.astype(BF16)
    lo = (stack - hi.astype(F32)).astype(BF16)
    exp_x = jnp.dot(jnp.concatenate([hi, lo], axis=1), e2, preferred_element_type=F32)
    dt_x, dd_x, ea_x = exp_x[:L], exp_x[L:2 * L], exp_x[2 * L:]
    bt = bm.T.astype(BF16)
    return dict(xs=xs, cm=cm, bt=bt, acum=acum, ea_x=ea_x, x_dt=(xs * dt_x).astype(BF16),
                x_dec=(xs * dd_x).astype(BF16), tril=tril)


def _ssd_state_update(q, st_ref, heads):
    gw = heads * SSD_HEAD_DIM // SSD_GROUPS
    L = q["ea_x"].shape[0]
    for g in range(SSD_GROUPS):
        sl = slice(g * gw, (g + 1) * gw)
        btg = q["bt"][g * SSD_STATE:(g + 1) * SSD_STATE, :]
        new = jnp.dot(btg, q["x_dec"][:, sl], preferred_element_type=F32)
        st_ref[:, sl] = st_ref[:, sl] * q["ea_x"][L - 1:L, sl] + new


def _ssd_body(xbc_ref, dt_ref, z_ref, s0_ref, dtb_ref, alog_ref, dskip_ref, ng_ref, e_ref,
              o_ref, st_ref, *, heads):
    @pl.when(pl.program_id(1) == 0)
    def _():
        st_ref[...] = s0_ref[...]

    L = xbc_ref.shape[0]
    q = _ssd_chunk(xbc_ref[...], dt_ref[...], dtb_ref[...], alog_ref[...], e_ref[...],
                   heads=heads, meta=False)

    gw = heads * SSD_HEAD_DIM // SSD_GROUPS
    j_heads = heads // SSD_GROUPS
    acum = q["acum"]
    acum_t = acum.T
    cmb = q["cm"].astype(BF16)
    x_dt_b = q["x_dt"]
    lane = lax.broadcasted_iota(I32, (L, 2 * SSD_HEAD_DIM), 1)
    lo = lane < SSD_HEAD_DIM
    neg_inf = jnp.float32(-jnp.inf)

    def decay_mat(h):
        col = jnp.broadcast_to(acum[:, h:h + 1], (L, L))
        row = jnp.broadcast_to(acum_t[h:h + 1, :], (L, L))
        return jnp.exp(jnp.where(q["tril"], col - row, neg_inf))

    for g in range(SSD_GROUPS):
        sl = slice(g * gw, (g + 1) * gw)
        cg = cmb[:, g * SSD_STATE:(g + 1) * SSD_STATE]
        btg = q["bt"][g * SSD_STATE:(g + 1) * SSD_STATE, :]
        cb_mat = jnp.dot(cg, btg, preferred_element_type=F32)
        parts = []
        for p in range(j_heads // 2):
            h1 = g * j_heads + 2 * p
            slab = x_dt_b[:, h1 * SSD_HEAD_DIM:(h1 + 2) * SSD_HEAD_DIM]
            lhs = jnp.concatenate([(cb_mat * decay_mat(h1)).astype(BF16),
                                   (cb_mat * decay_mat(h1 + 1)).astype(BF16)], axis=1)
            zero = jnp.zeros_like(slab)
            rhs = jnp.concatenate([jnp.where(lo, slab, zero),
                                   jnp.where(lo, zero, slab)], axis=0)
            parts.append(jnp.dot(lhs, rhs, preferred_element_type=F32))
        y_diag = parts[0] if len(parts) == 1 else jnp.concatenate(parts, axis=1)
        st_g = st_ref[:, sl]
        y_off = jnp.dot(cg, st_g.astype(BF16), preferred_element_type=F32)
        y = y_diag + y_off * q["ea_x"][:, sl] + dskip_ref[:, sl] * q["xs"][:, sl]
        zg = z_ref[:, sl]
        gated = y * (zg * _sigmoid(zg))
        ms = jnp.mean(gated * gated, axis=-1, keepdims=True)
        o_ref[:, sl] = (gated * lax.rsqrt(ms + EPS) * ng_ref[:, sl]).astype(o_ref.dtype)
        new = jnp.dot(btg, q["x_dec"][:, sl], preferred_element_type=F32)
        st_ref[:, sl] = st_g * q["ea_x"][L - 1:L, sl] + new


def _ssd_meta_body(xbc_ref, dt_ref, dtb_ref, alog_ref, e_ref, st_ref, *, heads):
    st_ref[...] = jnp.zeros_like(st_ref)
    q = _ssd_chunk(xbc_ref[...], dt_ref[...], dtb_ref[...], alog_ref[...], e_ref[...],
                   heads=heads, meta=True)
    _ssd_state_update(q, st_ref, heads)


def _full_spec(shape):
    nd = len(shape)
    return pl.BlockSpec(shape, lambda *_: (0,) * nd)


def _ssd_meta(xbc_m, dt_m, dtb, alog, e2, heads):
    di = heads * SSD_HEAD_DIM
    args = [xbc_m, dt_m, dtb, alog, e2]
    return pl.pallas_call(
        functools.partial(_ssd_meta_body, heads=heads),
        grid=(1,),
        in_specs=[_full_spec(a.shape) for a in args],
        out_specs=_full_spec((SSD_STATE, di)),
        out_shape=jax.ShapeDtypeStruct((SSD_STATE, di), F32),
        compiler_params=_cparams(1),
        name="ssd_meta",
    )(*args)


def _ssd_main(xbc, dt, z, s0, dtb, alog, dskip_x, ng, e2, *, heads, batch):
    t, cd = xbc.shape
    di = heads * SSD_HEAD_DIM
    L = SSD_CHUNK
    nc = t // batch // L
    row = lambda b, c: (b * nc + c, 0)
    in_specs = [pl.BlockSpec((L, cd), row),
                pl.BlockSpec((L, LANE), row),
                pl.BlockSpec((L, di), row)]
    consts = [s0, dtb, alog, dskip_x, ng, e2]
    in_specs += [_full_spec(a.shape) for a in consts]
    return pl.pallas_call(
        functools.partial(_ssd_body, heads=heads),
        grid=(batch, nc),
        in_specs=in_specs,
        out_specs=pl.BlockSpec((L, di), row),
        out_shape=jax.ShapeDtypeStruct((t, di), BF16),
        scratch_shapes=[pltpu.VMEM((SSD_STATE, di), F32)],
        compiler_params=_cparams(2),
        name="ssd_main",
    )(xbc, dt, z, *consts)


def _router_body(x_ref, g_ref, wr_ref, br_ref, hp_ref, idx_ref, prob_ref, rank_ref,
                 cnt_ref, base_ref):
    @pl.when(pl.program_id(0) == 0)
    def _():
        base_ref[...] = jnp.zeros_like(base_ref)

    x = x_ref[...]
    tm, d = x.shape
    ne = wr_ref.shape[0]
    ms = jnp.mean(x * x, axis=-1, keepdims=True)
    h = x * lax.rsqrt(ms + EPS) * g_ref[...]
    half = d // 2
    h_hi = h.astype(BF16)
    h_hi32 = h_hi.astype(F32)
    lo_bits = lax.bitcast_convert_type(h_hi32[:, :half], U32)
    hi_bits = lax.bitcast_convert_type(h_hi32[:, half:], U32)
    hp_ref[...] = (lo_bits >> 16) | (hi_bits & jnp.uint32(0xFFFF0000))

    w = wr_ref[...]
    w_hi = w.astype(BF16)
    w_lo = (w - w_hi.astype(F32)).astype(BF16)
    h_lo = (h - h_hi32).astype(BF16)
    nt = (((1,), (1,)), ((), ()))
    part = lax.dot_general(jnp.concatenate([w_hi, w_lo], axis=0), h_hi, nt,
                           preferred_element_type=F32)
    logits = (part[:ne] + part[ne:]
              + lax.dot_general(w_hi, h_lo, nt, preferred_element_type=F32)
              + br_ref[:, 0:1])
    eidx = lax.broadcasted_iota(I32, (ne, tm), 0)
    work = logits
    vals, onehots = [], []
    for k in range(TOP_K):
        mx = jnp.max(work, axis=0, keepdims=True)
        sel = jnp.min(jnp.where(work == mx, eidx, ne), axis=0, keepdims=True)
        oh = eidx == sel
        idx_ref[k:k + 1, :] = sel
        vals.append(mx)
        onehots.append(oh)
        work = jnp.where(oh, -jnp.inf, work)
    exps = [jnp.exp(v - vals[0]) for v in vals]
    denom = exps[0]
    for e in exps[1:]:
        denom = denom + e
    for k in range(TOP_K):
        prob_ref[k:k + 1, :] = exps[k] / denom

    cnt = onehots[0].astype(F32)
    for oh in onehots[1:]:
        cnt = cnt + oh.astype(F32)
    r = lax.broadcasted_iota(I32, (tm, tm), 0)
    c = lax.broadcasted_iota(I32, (tm, tm), 1)
    before = (r < c).astype(BF16)
    prefix = jnp.dot(cnt.astype(BF16), before, preferred_element_type=F32)
    prefix = prefix + base_ref[:, 0:1]
    for k in range(TOP_K):
        rk = jnp.sum(jnp.where(onehots[k], prefix, 0.0), axis=0, keepdims=True)
        rank_ref[k:k + 1, :] = rk.astype(I32)
    base_ref[...] = base_ref[...] + jnp.sum(cnt, axis=1, keepdims=True)
    cnt_ref[...] = base_ref[...]


def _router(hres, g, wr_t, br, tm):
    t, d = hres.shape
    ne = wr_t.shape[0]
    tok = lambda i: (0, i)
    return pl.pallas_call(
        _router_body,
        grid=(t // tm,),
        in_specs=[pl.BlockSpec((tm, d), lambda i: (i, 0)),
                  _full_spec((1, d)), _full_spec((ne, d)), _full_spec((ne, LANE))],
        out_specs=[pl.BlockSpec((tm, d // 2), lambda i: (i, 0)),
                   pl.BlockSpec((TOP_K, tm), tok),
                   pl.BlockSpec((TOP_K, tm), tok),
                   pl.BlockSpec((TOP_K, tm), tok),
                   _full_spec((ne, LANE))],
        out_shape=[jax.ShapeDtypeStruct((t, d // 2), U32),
                   jax.ShapeDtypeStruct((TOP_K, t), I32),
                   jax.ShapeDtypeStruct((TOP_K, t), F32),
                   jax.ShapeDtypeStruct((TOP_K, t), I32),
                   jax.ShapeDtypeStruct((ne, LANE), F32)],
        scratch_shapes=[pltpu.VMEM((ne, LANE), F32)],
        compiler_params=_cparams(1),
        name="router",
    )(hres, g.reshape(1, d), wr_t, jnp.broadcast_to(br.reshape(ne, 1), (ne, LANE)))


def _dispatch_body(dest_ref, cnt_ref, seg_ref, tot_ref, hp_ref, xs_ref, zrow_ref, zblk_ref,
                   row_sem, pad_sem, *, t_total, ne, n_blocks):
    i = pl.program_id(0)
    tm = hp_ref.shape[0] * SUBLANE

    def pad_copy(slot):
        return pltpu.make_async_copy(zrow_ref.at[pl.ds(0, 1)], xs_ref.at[pl.ds(slot, 1)], pad_sem)

    def tail_copy(b):
        return pltpu.make_async_copy(
            zblk_ref, xs_ref.at[pl.ds(pl.multiple_of(b * MOE_BLK, MOE_BLK), MOE_BLK)], pad_sem)

    @pl.when(i == 0)
    def _():
        zrow_ref[...] = jnp.zeros_like(zrow_ref)
        zblk_ref[...] = jnp.zeros_like(zblk_ref)

        def expert(e, carry):
            used = cnt_ref[e]
            end = (used + MOE_BLK - 1) // MOE_BLK * MOE_BLK

            def start(r, c):
                pad_copy(seg_ref[e] + r).start()
                return c
            lax.fori_loop(used, end, start, 0)

            def wait(r, c):
                pad_copy(seg_ref[e] + r).wait()
                return c
            lax.fori_loop(used, end, wait, 0)
            return carry
        lax.fori_loop(0, ne, expert, 0)

        def tail_start(b, c):
            tail_copy(b).start()
            return c
        lax.fori_loop(tot_ref[0], n_blocks, tail_start, 0)

        def tail_wait(b, c):
            tail_copy(b).wait()
            return c
        lax.fori_loop(tot_ref[0], n_blocks, tail_wait, 0)

    def row_copy(g, u, slot):
        return pltpu.make_async_copy(hp_ref.at[g, pl.ds(u, 1)], xs_ref.at[pl.ds(slot, 1)],
                                     row_sem)

    def issue(g, c):
        for k in range(TOP_K):
            base = k * t_total + i * tm + g * SUBLANE
            for u in range(SUBLANE):
                row_copy(g, u, dest_ref[base + u]).start(priority=u % 2)
        return c
    lax.fori_loop(0, tm // SUBLANE, issue, 0, unroll=4)

    def drain(g, c):
        for _ in range(TOP_K * SUBLANE):
            row_copy(0, 0, 0).wait()
        return c
    lax.fori_loop(0, tm // SUBLANE, drain, 0)


def _moe_dispatch(dest_flat, cnt, seg_start, total_blk, hp, n_rows, *, tm):
    t, w = hp.shape
    ne = cnt.shape[0]
    return pl.pallas_call(
        functools.partial(_dispatch_body, t_total=t, ne=ne, n_blocks=n_rows // MOE_BLK),
        grid_spec=pltpu.PrefetchScalarGridSpec(
            num_scalar_prefetch=4,
            grid=(t // tm,),
            in_specs=[pl.BlockSpec((tm // SUBLANE, SUBLANE, w), lambda i, *_: (i, 0, 0))],
            out_specs=pl.BlockSpec(memory_space=pl.ANY),
            scratch_shapes=[pltpu.VMEM((SUBLANE, w), U32),
                            pltpu.VMEM((MOE_BLK, w), U32),
                            pltpu.SemaphoreType.DMA,
                            pltpu.SemaphoreType.DMA],
        ),
        out_shape=jax.ShapeDtypeStruct((n_rows, w), U32),
        compiler_params=_cparams(1),
        name="moe_dispatch",
    )(dest_flat, cnt, seg_start, total_blk, hp.reshape(t // SUBLANE, SUBLANE, w))


def _sb_specs(rows, width_in, n_tiles):
    def x_map(s, j, e, start, nblk, tot):
        return (pl.multiple_of(start[s], MOE_BLK), 0)

    def wcol(s, j, nblk):
        return jnp.where(nblk[s] > 0, j, n_tiles - 1)

    return pl.BlockSpec((pl.Element(rows), pl.Element(width_in)), x_map), wcol


def _sb_run(start_ref, nblk_ref, tot_ref, o_ref, stage_ref, zblk_ref, out_sems, zero_sem,
            compute_rows, *, n_blocks):
    s = pl.program_id(0)
    j = pl.program_id(1)
    nj = pl.num_programs(1)
    q = s * nj + j
    slot = q % 2
    tile = stage_ref.shape[2]
    col = pl.multiple_of(j * tile, LANE)
    nb = nblk_ref[s]

    def out_copy(sl, b, row0):
        r0 = pl.multiple_of(b * MOE_BLK, MOE_BLK)
        return pltpu.make_async_copy(
            stage_ref.at[sl, pl.ds(r0, MOE_BLK)],
            o_ref.at[pl.ds(pl.multiple_of(row0 + r0, MOE_BLK), MOE_BLK), pl.ds(col, tile)],
            out_sems.at[sl])

    def drain(sl, count):
        def one(b, c):
            out_copy(sl, 0, 0).wait()
            return c
        lax.fori_loop(0, count, one, 0)

    def zero_copy(b):
        return pltpu.make_async_copy(
            zblk_ref,
            o_ref.at[pl.ds(pl.multiple_of(b * MOE_BLK, MOE_BLK), MOE_BLK), pl.ds(col, tile)],
            zero_sem)

    @pl.when(q >= 2)
    def _():
        drain(slot, nblk_ref[(q - 2) // nj])

    @pl.when(s == 0)
    def _():
        zblk_ref[...] = jnp.zeros_like(zblk_ref)

        def zs(b, c):
            zero_copy(b).start()
            return c
        lax.fori_loop(tot_ref[0], n_blocks, zs, 0)

        def zw(b, c):
            zero_copy(b).wait()
            return c
        lax.fori_loop(tot_ref[0], n_blocks, zw, 0)

    def run(b0, k):
        compute_rows(slot, pl.multiple_of(b0 * MOE_BLK, MOE_BLK), k * MOE_BLK)
        for i in range(k):
            out_copy(slot, b0 + i, start_ref[s]).start()

    k = 1 << (MOE_SB_BLKS.bit_length() - 1)
    while k >= 1:
        @pl.when(nb & k != 0)
        def _(k=k):
            run(nb & ~(2 * k - 1), k)
        k //= 2

    @pl.when(q == pl.num_programs(0) * nj - 1)
    def _():
        drain(slot, nb)

        @pl.when(q >= 1)
        def _():
            drain(1 - slot, nblk_ref[(q - 1) // nj])


def _gate_up_body(e_ref, start_ref, nblk_ref, tot_ref, x_ref, wg_ref, wu_ref, bg_ref, bu_ref,
                  o_ref, stage_ref, zblk_ref, out_sems, zero_sem, *, n_blocks):
    tj = wg_ref.shape[1]
    bias = jnp.concatenate([bg_ref[...], bu_ref[...]], axis=1)

    def compute_rows(slot, r0, n_rows):
        xp = x_ref[pl.ds(r0, n_rows), :]
        x_lo = lax.bitcast_convert_type(xp << 16, F32).astype(BF16)
        x_hi = lax.bitcast_convert_type(xp & jnp.uint32(0xFFFF0000), F32).astype(BF16)
        x = jnp.concatenate([x_lo, x_hi], axis=1)
        w = jnp.concatenate([wg_ref[...], wu_ref[...]], axis=1).astype(BF16)
        gu = jnp.dot(x, w, preferred_element_type=F32) + bias
        gate = jnp.minimum(gu[:, :tj], SWIGLU_LIMIT)
        up = jnp.clip(gu[:, tj:], -SWIGLU_LIMIT, SWIGLU_LIMIT)
        act = (up + 1.0) * (gate * _sigmoid(gate * SWIGLU_ALPHA))
        stage_ref[slot, pl.ds(r0, n_rows), :] = act.astype(stage_ref.dtype)

    _sb_run(start_ref, nblk_ref, tot_ref, o_ref, stage_ref, zblk_ref, out_sems, zero_sem,
            compute_rows, n_blocks=n_blocks)


def _sb_scratch(rows, tile, dtype):
    return [pltpu.VMEM((2, rows, tile), dtype), pltpu.VMEM((MOE_BLK, tile), dtype),
            pltpu.SemaphoreType.DMA((2,)), pltpu.SemaphoreType.DMA]


def _moe_gate_up(sb, n_sb, xs, w_gu, b_gu, *, tj):
    ne, d, de2 = w_gu.shape
    de = de2 // 2
    nj = de // tj
    rows = MOE_SB_BLKS * MOE_BLK
    n_rows, half = xs.shape
    x_spec, wcol = _sb_specs(rows, half, nj)
    in_specs = [
        x_spec,
        pl.BlockSpec((None, d, tj), lambda s, j, e, st, nb, tt: (e[s], 0, wcol(s, j, nb))),
        pl.BlockSpec((None, d, tj), lambda s, j, e, st, nb, tt: (e[s], 0, nj + wcol(s, j, nb))),
        pl.BlockSpec((None, 1, tj), lambda s, j, e, st, nb, tt: (e[s], 0, wcol(s, j, nb))),
        pl.BlockSpec((None, 1, tj), lambda s, j, e, st, nb, tt: (e[s], 0, nj + wcol(s, j, nb))),
    ]
    return pl.pallas_call(
        functools.partial(_gate_up_body, n_blocks=n_rows // MOE_BLK),
        grid_spec=pltpu.PrefetchScalarGridSpec(
            num_scalar_prefetch=4,
            grid=(n_sb, nj),
            in_specs=in_specs,
            out_specs=pl.BlockSpec(memory_space=pl.ANY),
            scratch_shapes=_sb_scratch(rows, tj, BF16),
        ),
        out_shape=jax.ShapeDtypeStruct((n_rows, de), BF16),
        compiler_params=_cparams(2, VMEM_MB_WIDE),
        name="moe_gate_up",
    )(*sb, xs, w_gu, w_gu, b_gu.reshape(ne, 1, de2), b_gu.reshape(ne, 1, de2))


def _down_body(e_ref, start_ref, nblk_ref, tot_ref, a_ref, w_ref, b_ref, o_ref,
               stage_ref, zblk_ref, out_sems, zero_sem, *, n_blocks):
    bias = b_ref[...]

    def compute_rows(slot, r0, n_rows):
        stage_ref[slot, pl.ds(r0, n_rows), :] = jnp.dot(
            a_ref[pl.ds(r0, n_rows), :], w_ref[...].astype(BF16),
            preferred_element_type=F32) + bias

    _sb_run(start_ref, nblk_ref, tot_ref, o_ref, stage_ref, zblk_ref, out_sems, zero_sem,
            compute_rows, n_blocks=n_blocks)


def _moe_down(sb, n_sb, act, w_down, b_down, *, tn):
    ne, de, d = w_down.shape
    nn = d // tn
    rows = MOE_SB_BLKS * MOE_BLK
    n_rows = act.shape[0]
    a_spec, wcol = _sb_specs(rows, de, nn)
    in_specs = [
        a_spec,
        pl.BlockSpec((None, de, tn), lambda s, j, e, st, nb, tt: (e[s], 0, wcol(s, j, nb))),
        pl.BlockSpec((None, 1, tn), lambda s, j, e, st, nb, tt: (e[s], 0, wcol(s, j, nb))),
    ]
    return pl.pallas_call(
        functools.partial(_down_body, n_blocks=n_rows // MOE_BLK),
        grid_spec=pltpu.PrefetchScalarGridSpec(
            num_scalar_prefetch=4,
            grid=(n_sb, nn),
            in_specs=in_specs,
            out_specs=pl.BlockSpec(memory_space=pl.ANY),
            scratch_shapes=_sb_scratch(rows, tn, F32),
        ),
        out_shape=jax.ShapeDtypeStruct((n_rows, d), F32),
        compiler_params=_cparams(2),
        name="moe_down",
    )(*sb, act, w_down, b_down.reshape(ne, 1, d))


def _moe_schedule(cnt, n_pairs, ne):
    n_blocks = n_pairs // MOE_BLK + ne + MOE_SB_BLKS - 1
    s_max = -(-(n_pairs // MOE_BLK + ne) // MOE_SB_BLKS) + ne
    pblk = (cnt + MOE_BLK - 1) // MOE_BLK
    seg_end_blk = jnp.cumsum(pblk)
    seg_start = (seg_end_blk - pblk) * MOE_BLK
    total_blk = seg_end_blk[-1:]
    nsb_e = (pblk + MOE_SB_BLKS - 1) // MOE_SB_BLKS
    sb_end = jnp.cumsum(nsb_e)
    sb_first = sb_end - nsb_e
    n_sb = sb_end[-1]
    s_ids = jnp.arange(s_max, dtype=I32)
    used = s_ids < n_sb
    e_of_s = jnp.minimum(jnp.searchsorted(sb_end, s_ids, side="right"), ne - 1).astype(I32)
    i_in_e = s_ids - sb_first[e_of_s]
    start = seg_start[e_of_s] + i_in_e * (MOE_SB_BLKS * MOE_BLK)
    nblk = jnp.where(used, jnp.clip(pblk[e_of_s] - MOE_SB_BLKS * i_in_e, 0, MOE_SB_BLKS), 0)
    last = jnp.maximum(n_sb - 1, 0)
    sb = (jnp.where(used, e_of_s, e_of_s[last]).astype(I32),
          jnp.where(used, start, start[last]).astype(I32), nblk.astype(I32),
          total_blk.astype(I32))
    return sb, n_sb.astype(I32), seg_start.astype(I32), n_blocks * MOE_BLK


def _combine_body(dest_ref, y_ref, h_ref, p_ref, g_ref, o_ref, ybuf, sems, *, t_total):
    i = pl.program_id(0)
    n = pl.num_programs(0)
    tc, d = h_ref.shape
    groups = tc // SUBLANE

    def row_copy(slot, k, g, u, row):
        return pltpu.make_async_copy(y_ref.at[pl.ds(row, 1)], ybuf.at[slot, k, g, pl.ds(u, 1)],
                                     sems.at[slot])

    def issue(tile, slot):
        def one(g, carry):
            for k in range(TOP_K):
                base = k * t_total + tile * tc + g * SUBLANE
                for u in range(SUBLANE):
                    row_copy(slot, k, g, u, dest_ref[base + u]).start(priority=u % 2)
            return carry
        lax.fori_loop(0, groups, one, 0, unroll=4)

    @pl.when(i == 0)
    def _():
        for t0 in range(COMBINE_BUFS - 1):
            @pl.when(t0 < n)
            def _(t0=t0):
                issue(t0, t0)

    slot = i % COMBINE_BUFS

    def drain(g, carry):
        for _ in range(TOP_K * SUBLANE):
            row_copy(slot, 0, 0, 0, 0).wait()
        return carry
    lax.fori_loop(0, groups, drain, 0)

    @pl.when(i + COMBINE_BUFS - 1 < n)
    def _():
        issue(i + COMBINE_BUFS - 1, (i + COMBINE_BUFS - 1) % COMBINE_BUFS)

    p = p_ref[...]
    cw = min(d, 4 * LANE)
    ssq = jnp.zeros((tc, 1), F32)
    for c in range(d // cw):
        cols = slice(c * cw, (c + 1) * cw)
        acc = h_ref[:, cols]
        for k in range(TOP_K):
            acc = acc + p[:, k:k + 1] * ybuf[slot, k, :, :, cols].reshape(tc, cw)
        o_ref[:, cols] = acc
        ssq = ssq + jnp.sum(acc * acc, axis=-1, keepdims=True)
    scale = lax.rsqrt(ssq * (1.0 / d) + EPS)
    for c in range(d // cw):
        cols = slice(c * cw, (c + 1) * cw)
        o_ref[:, cols] = o_ref[:, cols] * scale * g_ref[:, cols]


def _combine(dest_flat, y_sorted, hres, p_t, final_g, *, tc):
    t, d = hres.shape
    return pl.pallas_call(
        functools.partial(_combine_body, t_total=t),
        grid_spec=pltpu.PrefetchScalarGridSpec(
            num_scalar_prefetch=1,
            grid=(t // tc,),
            in_specs=[pl.BlockSpec(memory_space=pl.ANY),
                      pl.BlockSpec((tc, d), lambda i, *_: (i, 0)),
                      pl.BlockSpec((tc, TOP_K), lambda i, *_: (i, 0)),
                      pl.BlockSpec((1, d), lambda i, *_: (0, 0))],
            out_specs=pl.BlockSpec((tc, d), lambda i, *_: (i, 0)),
            scratch_shapes=[pltpu.VMEM((COMBINE_BUFS, TOP_K, tc // SUBLANE, SUBLANE, d), F32),
                            pltpu.SemaphoreType.DMA((COMBINE_BUFS,))],
        ),
        out_shape=jax.ShapeDtypeStruct((t, d), F32),
        compiler_params=_cparams(1),
        name="moe_combine",
    )(dest_flat, y_sorted, hres, p_t, final_g.reshape(1, d))


def _pick(n, prefs):
    for p in prefs:
        if n % p == 0:
            return p
    raise ValueError(f"no tile in {prefs} divides {n}")


def kernel(x, meta_tokens, norm_mix_g, w_in, ssd_conv_w, ssd_conv_b, dt_bias, a_log, d_skip,
           ssd_norm_g, w_ssd_out, sc_conv_w, w_sc_out, w_o, norm_ffn_g, w_router, b_router,
           w_gate_up, b_gate_up, w_down, b_down, final_g):
    batch, seq, d = x.shape
    assert norm_mix_g.shape[0] == 1, "meta-token handling assumes a single layer"
    assert seq % SSD_CHUNK == 0 and N_META <= SSD_CHUNK
    t = batch * seq
    heads = d // SSD_HEAD_DIM
    gn = SSD_GROUPS * SSD_STATE
    conv_dim = d + 2 * gn
    xbc_off = d
    dt_off = xbc_off + conv_dim
    sc_off = dt_off + heads
    gate_off = sc_off + 3 * d
    ne = w_router.shape[-1]
    de = w_down.shape[2]

    x2 = x.reshape(t, d)
    w_in_nk = jnp.swapaxes(w_in[0], 0, 1)

    tm = _pick(seq, (1024, 512, 256, 128))
    tn = _pick(d, (512, 256, 128))

    h = _rmsnorm_bf16(x2, norm_mix_g[0], _pick(t, (256, 128)))
    h_meta = _rmsnorm_bf16(meta_tokens, norm_mix_g[0], N_META)

    proj = functools.partial(_matmul, h, w_in_nk, tm=tm, out_dtype=F32, w_is_nk=True)
    z = proj(0, d, tn=tn, name="proj_z")
    xbc, xbc_meta = _proj_conv(h, w_in_nk, [xbc_off], h_meta, ssd_conv_w[0],
                               ssd_conv_b[0].reshape(1, conv_dim), width=conv_dim, tm=tm,
                               tn=_pick(conv_dim, (768, 512, 256, 128)), seq=seq, mode="xbc",
                               name="proj_xbc")
    dtr, dt_meta = proj(dt_off, LANE, tn=LANE, side=h_meta, name="proj_dt")
    u_sc = _proj_conv(h, w_in_nk, [sc_off, sc_off + d, sc_off + 2 * d], h_meta,
                      sc_conv_w[0], None, width=d, tm=tm, tn=_pick(d, (256, 128)), seq=seq,
                      mode="sc", name="proj_sc")
    gates = proj(gate_off, 2 * d, tn=tn, name="proj_gate")

    e_mat = (jnp.arange(d, dtype=I32)[None, :] // SSD_HEAD_DIM
             == jnp.arange(heads, dtype=I32)[:, None]).astype(BF16)
    e2 = jnp.concatenate([e_mat, e_mat], axis=0)
    pad = SSD_CHUNK - N_META
    dtb = dt_bias[0].reshape(1, heads)
    alog = a_log[0].reshape(1, heads)
    s0 = _ssd_meta(jnp.pad(xbc_meta, ((pad, 0), (0, 0))),
                   jnp.pad(dt_meta, ((pad, 0), (0, 0))), dtb, alog, e2, heads)
    dskip_x = jnp.repeat(d_skip[0], SSD_HEAD_DIM).reshape(1, d)
    g_ssd = _ssd_main(xbc, dtr, z, s0, dtb, alog, dskip_x, ssd_norm_g[0].reshape(1, d), e2,
                      heads=heads, batch=batch)

    y_a = _matmul(g_ssd, w_ssd_out[0], 0, d, tm=tm, tn=tn, out_dtype=F32,
                  extras=[(gates, 0)],
                  epilogue=lambda acc, ga: _sigmoid(ga) * acc, name="ssd_out")
    mixed = _matmul(u_sc, w_sc_out[0], 0, d, tm=tm, tn=tn, out_dtype=BF16,
                    extras=[(gates, d), (y_a, 0)],
                    epilogue=lambda acc, gb, ya: ya + _sigmoid(gb) * acc, name="sc_out")
    hres = _matmul(mixed, w_o[0], 0, d, tm=tm, tn=tn, out_dtype=F32,
                   extras=[(x2, 0)], epilogue=lambda acc, xr: xr + acc, name="w_o")

    rt = _pick(t, (512, 256, 128))
    hp, top_idx, probs, rank, counts = _router(hres, norm_ffn_g[0], w_router[0].T,
                                               b_router[0], rt)
    cnt = counts[:, 0].astype(I32)
    sb, n_sb, seg_start, n_slots = _moe_schedule(cnt, t * TOP_K, ne)
    e_ids = jnp.arange(ne, dtype=I32)[:, None, None]
    dest = rank + jnp.sum(jnp.where(top_idx[None] == e_ids, seg_start[:, None, None], 0),
                          axis=0)
    dest_flat = dest.reshape(-1)

    xs = _moe_dispatch(dest_flat, cnt, seg_start, sb[3], hp, n_slots,
                       tm=_pick(t, (1024, 512, 256, 128)))
    act = _moe_gate_up(sb, n_sb, xs, w_gate_up[0], b_gate_up[0],
                       tj=_pick(de, (384, 256, 128)))
    y_sorted = _moe_down(sb, n_sb, act, w_down[0], b_down[0],
                         tn=_pick(d, (1024, 512, 256, 128)))
    out = _combine(dest_flat, y_sorted, hres, probs.T, final_g, tc=_pick(t, (128,)))
    return out.reshape(batch, seq, d)
```
